```python
import jax
import jax.numpy as jnp
from jax import lax
import numpy as np

D_MODEL = 1024
BATCH = 2
SEQ = 8192
DEPTH = 2

GRID_W = 64
CTX_LEN = 256
HEAD_DIM = 64
N_GROUPS = 4
GROUP_WIDTH = D_MODEL // N_GROUPS
MIX_WIDTH = N_GROUPS * GROUP_WIDTH
N_HEADS = GROUP_WIDTH // HEAD_DIM
N_KV_HEADS = N_HEADS // 2
KV_WIDTH = N_KV_HEADS * HEAD_DIM
WIN = 128
BLOCK = 128
NA_ROWS = 8
NA_COLS = 16
HGRN_CHUNK = 64
D_FF = 256 * ((8 * D_MODEL // 3 + 255) // 256)
CONV_W = 3
ROPE_BASE = 10000.0
ALPHA = (2 * DEPTH) ** 0.25
BETA = (8 * DEPTH) ** -0.25
EPS = 1e-6
IN_SPLIT = (GROUP_WIDTH, KV_WIDTH, KV_WIDTH,
            GROUP_WIDTH, KV_WIDTH, KV_WIDTH,
            GROUP_WIDTH, GROUP_WIDTH, GROUP_WIDTH, GROUP_WIDTH, GROUP_WIDTH,
            GROUP_WIDTH, GROUP_WIDTH, GROUP_WIDTH)
IN_WIDTH = sum(IN_SPLIT)

kernel_name = 'hybrid_parallel_groups_dit_block'


def _layer_norm(x, g, b):
    xf = x.astype(jnp.float32)
    mu = jnp.mean(xf, -1, keepdims=True)
    var = jnp.mean(jnp.square(xf - mu), -1, keepdims=True)
    return ((xf - mu) * lax.rsqrt(var + EPS) * g + b).astype(x.dtype)


def _rms_norm(x, g):
    xf = x.astype(jnp.float32)
    return (xf * lax.rsqrt(jnp.mean(jnp.square(xf), -1, keepdims=True) + EPS) * g).astype(x.dtype)


def _heads(a):
    return a.reshape(a.shape[0], a.shape[1], -1, HEAD_DIM)


def _split_cols(p):
    return jnp.split(p, np.cumsum(IN_SPLIT)[:-1].tolist(), axis=-1)


def _axial_rope_tables(n):
    t = jnp.arange(n)
    pos = jnp.stack([t // GRID_W, t % GRID_W], -1).astype(jnp.float32)
    n_freq = HEAD_DIM // 4
    inv = ROPE_BASE ** (-jnp.arange(n_freq, dtype=jnp.float32) / n_freq)
    ang = pos[:, :, None] * inv
    return jnp.cos(ang), jnp.sin(ang)


def _rope2d(x, cos, sin):
    B, S, H, D = x.shape
    xa = x.reshape(B, S, H, 2, 2, D // 4)
    x1, x2 = xa[..., 0, :], xa[..., 1, :]
    cs, sn = cos[:, None], sin[:, None]
    out = jnp.stack([x1 * cs - x2 * sn, x2 * cs + x1 * sn], axis=-2)
    return out.reshape(B, S, H, D).astype(x.dtype)


def _dense_attn(q, k, v, sink=None):
    B, T, H, dh = q.shape
    kvh = k.shape[2]
    g = H // kvh
    n_k = k.shape[1]
    s = jnp.einsum('btkgd,bskd->bkgts', q.reshape(B, T, kvh, g, dh), k).astype(jnp.float32) * dh ** -0.5
    if sink is not None:
        s = jnp.concatenate([s, jnp.broadcast_to(sink.astype(jnp.float32).reshape(1, kvh, g, 1, 1), (B, kvh, g, T, 1))], -1)
    p = jax.nn.softmax(s, -1)[..., :n_k].astype(v.dtype)
    return jnp.einsum('bkgts,bskd->btkgd', p, v).reshape(B, T, H * dh)


def _window_attn_latent(q, k, v, k_ctx, v_ctx, sink):
    B, S, H, dh = q.shape
    kvh = k.shape[2]
    g = H // kvh
    nb = S // BLOCK
    n_ctx = k_ctx.shape[1]
    qb = q.reshape(B, nb, BLOCK, kvh, g, dh)

    def band(a):
        ap = jnp.pad(a, ((0, 0), (WIN, WIN), (0, 0), (0, 0))).reshape(B, nb + 2, BLOCK, kvh, dh)
        return jnp.concatenate([ap[:, :-2], ap[:, 1:-1], ap[:, 2:]], axis=2)

    kb, vb = band(k), band(v)
    qpos = jnp.arange(nb)[:, None] * BLOCK + jnp.arange(BLOCK)[None]
    kpos = jnp.arange(nb)[:, None] * BLOCK - WIN + jnp.arange(3 * BLOCK)[None]
    kp = kpos[:, None, :]
    valid = (jnp.abs(qpos[:, :, None] - kp) <= WIN) & (kp >= 0) & (kp < S)
    scale = dh ** -0.5
    s_loc = jnp.einsum('bnqkgd,bnjkd->bkgnqj', qb, kb).astype(jnp.float32) * scale
    s_loc = jnp.where(valid, s_loc, -jnp.inf)
    s_ctx = jnp.einsum('bnqkgd,bckd->bkgnqc', qb, k_ctx).astype(jnp.float32) * scale
    s_sink = jnp.broadcast_to(sink.astype(jnp.float32).reshape(1, kvh, g, 1, 1, 1), (B, kvh, g, nb, BLOCK, 1))
    p = jax.nn.softmax(jnp.concatenate([s_loc, s_ctx, s_sink], -1), axis=-1).astype(v.dtype)
    n_loc = 3 * BLOCK
    o = (jnp.einsum('bkgnqj,bnjkd->bnqkgd', p[..., :n_loc], vb)
         + jnp.einsum('bkgnqc,bckd->bnqkgd', p[..., n_loc:n_loc + n_ctx], v_ctx))
    return o.reshape(B, S, H * dh)


def _global_attn_latent(q, k, v, k_ctx, v_ctx):
    B, S, H, dh = q.shape
    kvh = k.shape[2]
    g = H // kvh
    nb = S // BLOCK
    qb = q.reshape(B, nb, BLOCK, kvh, g, dh).transpose(1, 0, 2, 3, 4, 5)
    k_all = jnp.concatenate([k, k_ctx], 1)
    v_all = jnp.concatenate([v, v_ctx], 1)
    scale = dh ** -0.5

    def one_block(q_blk):
        s = jnp.einsum('bqkgd,bskd->bkgqs', q_blk, k_all).astype(jnp.float32) * scale
        p = jax.nn.softmax(s, -1).astype(v_all.dtype)
        return jnp.einsum('bkgqs,bskd->bqkgd', p, v_all)

    o = lax.map(one_block, qb)
    return o.transpose(1, 0, 2, 3, 4, 5).reshape(B, S, H * dh)


def _neigh_attn_latent(q, k, v, k_ctx, v_ctx, rpb):
    B, S, H, dh = q.shape
    rows = S // GRID_W
    kr = min(NA_ROWS, rows)
    qg = q.reshape(B, rows, GRID_W, H, dh)
    kg = k.reshape(B, rows, GRID_W, H, dh)
    vg = v.reshape(B, rows, GRID_W, H, dh)
    r = jnp.arange(rows)
    r0 = jnp.clip(r - kr // 2, 0, rows - kr)
    row_idx = r0[:, None] + jnp.arange(kr)[None]
    k_nb = kg[:, row_idx]
    v_nb = vg[:, row_idx]
    col = jnp.arange(GRID_W)
    c0 = jnp.clip(col - NA_COLS // 2, 0, GRID_W - NA_COLS)
    col_ok = (col[None] >= c0[:, None]) & (col[None] < c0[:, None] + NA_COLS)
    roff = row_idx - r[:, None] + (NA_ROWS - 1)
    coff = jnp.clip(col[None] - col[:, None], -(NA_COLS - 1), NA_COLS - 1) + (NA_COLS - 1)
    bias = rpb[:, roff[:, None, :, None], coff[None, :, None, :]]
    scale = dh ** -0.5
    s_loc = jnp.einsum('brqhd,brjchd->bhrqjc', qg, k_nb).astype(jnp.float32) * scale + bias.astype(jnp.float32)
    s_loc = jnp.where(col_ok[:, None, :], s_loc, -jnp.inf).reshape(B, H, rows, GRID_W, kr * GRID_W)
    s_ctx = jnp.einsum('brqhd,bthd->bhrqt', qg, k_ctx).astype(jnp.float32) * scale
    p = jax.nn.softmax(jnp.concatenate([s_loc, s_ctx], -1), axis=-1).astype(v.dtype)
    n_loc = kr * GRID_W
    p_loc = p[..., :n_loc].reshape(B, H, rows, GRID_W, kr, GRID_W)
    o = (jnp.einsum('bhrqjc,brjchd->brqhd', p_loc, v_nb)
         + jnp.einsum('bhrqt,bthd->brqhd', p[..., n_loc:], v_ctx))
    return o.reshape(B, S, H * dh)


def _hgrn_scan(q, k, v, logf, s0, with_output):
    B, T, H, DK = q.shape
    DV = v.shape[-1]
    L = HGRN_CHUNK
    n = T // L
    tril = jnp.tril(jnp.ones((L, L), dtype=bool))[None, :, :, None, None]

    def chunks(a):
        return a.reshape(B, n, L, H, a.shape[-1]).swapaxes(0, 1)

    def step(s, inp):
        qc, kc, vc, gc = inp
        b = jnp.cumsum(gc, axis=1)
        b_last = b[:, -1]
        s_new = jnp.exp(b_last)[..., None] * s + jnp.einsum('blhk,blhv->bhkv', kc * jnp.exp(b_last[:, None] - b), vc)
        if not with_output:
            return s_new, None
        o_inter = jnp.einsum('blhk,bhkv->blhv', qc * jnp.exp(b), s)
        diff = b[:, :, None] - b[:, None, :]
        dec = jnp.where(tril, jnp.exp(jnp.where(tril, diff, 0.0)), 0.0)
        a = jnp.sum(qc[:, :, None] * kc[:, None, :] * dec, axis=-1)
        o_intra = jnp.einsum('btsh,bshv->bthv', a, vc)
        return s_new, o_inter + o_intra

    s_fin, o = lax.scan(step, s0, (chunks(q), chunks(k), chunks(v), chunks(logf)))
    if with_output:
        o = o.swapaxes(0, 1).reshape(B, T, H, DV)
    return s_fin, o


def _hgrn_mixer(xs, cs, lb, onorm, need_ctx):
    q_x, i_x, zf_fx, zf_bx, g_x = xs
    q_c, i_c, zf_fc, zf_bc, g_c = cs
    B = q_x.shape[0]
    f32 = jnp.float32
    qh_x, ih_x = _heads(jax.nn.silu(q_x.astype(f32))), _heads(i_x.astype(f32))
    qh_c, ih_c = _heads(jax.nn.silu(q_c.astype(f32))), _heads(i_c.astype(f32))

    def gates(zf, lbd):
        f = lbd + (1.0 - lbd) * jax.nn.sigmoid(zf.astype(f32))
        return _heads(1.0 - f), _heads(jnp.log(f))

    outs_x, outs_c = [], []
    for d, (zf_x, zf_c) in enumerate(((zf_fx, zf_fc), (zf_bx, zf_bc))):
        k_x, lf_x = gates(zf_x, lb[d])
        k_c, lf_c = gates(zf_c, lb[d])
        lat = (qh_x, k_x, ih_x, lf_x)
        con = (qh_c, k_c, ih_c, lf_c)
        if d == 1:
            lat = tuple(a[:, ::-1] for a in lat)
            con = tuple(a[:, ::-1] for a in con)
        s0 = jnp.zeros((B, N_HEADS, HEAD_DIM, HEAD_DIM), f32)
        s_ctx, o_ctx = _hgrn_scan(*con, s0, need_ctx)
        _, o_lat = _hgrn_scan(*lat, s_ctx, True)
        if d == 1:
            o_lat = o_lat[:, ::-1]
            if need_ctx:
                o_ctx = o_ctx[:, ::-1]
        outs_x.append(o_lat)
        outs_c.append(o_ctx)

    def readout(o, gate):
        y = _rms_norm(o, onorm.reshape(N_HEADS, HEAD_DIM).astype(f32))
        y = y.reshape(o.shape[0], o.shape[1], GROUP_WIDTH) * jax.nn.silu(gate.astype(f32))
        return y.astype(gate.dtype)

    out_x = readout(outs_x[0] + outs_x[1], g_x)
    out_c = readout(outs_c[0] + outs_c[1], g_c) if need_ctx else None
    return out_x, out_c


def _token_mixers(px, pc, cos, sin, sink, qk_gain, lb, onorm, rpb, need_ctx):
    (qa, ka, va, qb, kb, vb, qc, ic, fcf, fcb, gc, qd, kd, vd) = _split_cols(px)
    (qa_c, ka_c, va_c, qb_c, kb_c, vb_c, qc_c, ic_c, fcf_c, fcb_c, gc_c, qd_c, kd_c, vd_c) = _split_cols(pc)
    ka_c, va_c = _heads(ka_c), _heads(va_c)
    o_a = _window_attn_latent(_rope2d(_heads(qa), cos, sin), _rope2d(_heads(ka), cos, sin), _heads(va), ka_c, va_c, sink)
    nq = lambda a: _rms_norm(_heads(a), qk_gain[0])
    nk = lambda a: _rms_norm(_heads(a), qk_gain[1])
    kb_c, vb_c = nk(kb_c), _heads(vb_c)
    o_b = _global_attn_latent(_rope2d(nq(qb), cos, sin), _rope2d(nk(kb), cos, sin), _heads(vb), kb_c, vb_c)
    o_c, oc_ctx = _hgrn_mixer((qc, ic, fcf, fcb, gc), (qc_c, ic_c, fcf_c, fcb_c, gc_c), lb, onorm, need_ctx)
    kd_c, vd_c = _heads(kd_c), _heads(vd_c)
    o_d = _neigh_attn_latent(_heads(qd), _heads(kd), _heads(vd), kd_c, vd_c, rpb)
    mix_x = jnp.concatenate([o_a, o_b, o_c, o_d], -1)
    mix_c = None
    if need_ctx:
        mix_c = jnp.concatenate([
            _dense_attn(_heads(qa_c), ka_c, va_c, sink),
            _dense_attn(nq(qb_c), kb_c, vb_c),
            oc_ctx,
            _dense_attn(_heads(qd_c), kd_c, vd_c)], -1)
    return mix_x, mix_c


def _conv_ffn(h, w_up, conv_w, conv_b, w_down):
    u = h @ w_up
    up = jnp.pad(u, ((0, 0), (1, 1), (0, 0)))
    u = up[:, :-2] * conv_w[0] + up[:, 1:-1] * conv_w[1] + up[:, 2:] * conv_w[2] + conv_b
    a, v = jnp.split(u, 2, axis=-1)
    return (jax.nn.silu(a) * v) @ w_down


def setup_inputs(seed: int = 0) -> dict:
    key = jax.random.key(seed)
    ks = jax.random.split(key, 19)
    f32 = jnp.float32
    nrm = lambda k, shape, s: jax.random.normal(k, shape, f32) * s
    return {
        'x': nrm(ks[0], (BATCH, SEQ, D_MODEL), 1.0),
        'c': nrm(ks[1], (BATCH, D_MODEL), 1.0),
        'ctx': nrm(ks[2], (BATCH, CTX_LEN, D_MODEL), 1.0),
        'c_ctx': nrm(ks[3], (D_MODEL,), 1.0),
        'w_mod': nrm(ks[4], (DEPTH, D_MODEL, 6 * D_MODEL), 0.5 * D_MODEL ** -0.5),
        'b_mod': nrm(ks[5], (DEPTH, 6 * D_MODEL), 0.02),
        'w_in': nrm(ks[6], (DEPTH, D_MODEL, IN_WIDTH), D_MODEL ** -0.5),
        'w_out': nrm(ks[7], (DEPTH, MIX_WIDTH, D_MODEL), BETA * MIX_WIDTH ** -0.5),
        'ln_g': 1.0 + nrm(ks[8], (DEPTH, 2, D_MODEL), 0.02),
        'ln_b': nrm(ks[9], (DEPTH, 2, D_MODEL), 0.02),
        'attn_sink': nrm(ks[10], (DEPTH, N_HEADS), 1.0),
        'qk_norm': 1.0 + nrm(ks[11], (DEPTH, 2, HEAD_DIM), 0.02),
        'hgrn_lb_logits': nrm(ks[12], (2, DEPTH, GROUP_WIDTH), 1.0),
        'hgrn_onorm': 1.0 + nrm(ks[13], (DEPTH, GROUP_WIDTH), 0.02),
        'na_rpb': nrm(ks[14], (DEPTH, N_HEADS, 2 * NA_ROWS - 1, 2 * NA_COLS - 1), 0.1),
        'ffn_w_up': nrm(ks[15], (DEPTH, D_MODEL, 2 * D_FF), D_MODEL ** -0.5),
        'ffn_conv_w': nrm(ks[16], (DEPTH, CONV_W, 2 * D_FF), CONV_W ** -0.5),
        'ffn_conv_b': nrm(ks[17], (DEPTH, 2 * D_FF), 0.02),
        'ffn_w_down': nrm(ks[18], (DEPTH, D_FF, D_MODEL), BETA * D_FF ** -0.5),
    }


def reference(x, c, ctx, c_ctx, w_mod, b_mod, w_in, w_out, ln_g, ln_b, attn_sink, qk_norm,
              hgrn_lb_logits, hgrn_onorm, na_rpb, ffn_w_up, ffn_conv_w, ffn_conv_b, ffn_w_down):
    cos, sin = _axial_rope_tables(x.shape[1])
    p_lb = jax.nn.softmax(hgrn_lb_logits.astype(jnp.float32), axis=1)
    lb_all = jnp.cumsum(p_lb, axis=1) - p_lb[:, :1]
    s_c = jax.nn.silu(c)
    s_cc = jax.nn.silu(c_ctx)
    h_x, h_c = x, ctx
    for l in range(DEPTH):
        need_ctx = l < DEPTH - 1
        mod_x = (s_c @ w_mod[l] + b_mod[l])[:, None, :]
        mod_c = s_cc @ w_mod[l] + b_mod[l]
        sa_x, ca_x, ga_x, sf_x, cf_x, gf_x = jnp.split(mod_x, 6, axis=-1)
        sa_c, ca_c, ga_c, sf_c, cf_c, gf_c = jnp.split(mod_c, 6, axis=-1)
        px = (h_x * (1.0 + ca_x) + sa_x) @ w_in[l]
        pc = (h_c * (1.0 + ca_c) + sa_c) @ w_in[l]
        mix_x, mix_c = _token_mixers(px, pc, cos, sin, attn_sink[l], qk_norm[l], lb_all[:, l],
                                     hgrn_onorm[l], na_rpb[l], need_ctx)
        h_x = _layer_norm(ALPHA * h_x + ga_x * (mix_x @ w_out[l]), ln_g[l, 0], ln_b[l, 0])
        f_x = _conv_ffn(h_x * (1.0 + cf_x) + sf_x, ffn_w_up[l], ffn_conv_w[l], ffn_conv_b[l], ffn_w_down[l])
        h_x = _layer_norm(ALPHA * h_x + gf_x * f_x, ln_g[l, 1], ln_b[l, 1])
        if need_ctx:
            h_c = _layer_norm(ALPHA * h_c + ga_c * (mix_c @ w_out[l]), ln_g[l, 0], ln_b[l, 0])
            f_c = _conv_ffn(h_c * (1.0 + cf_c) + sf_c, ffn_w_up[l], ffn_conv_w[l], ffn_conv_b[l], ffn_w_down[l])
            h_c = _layer_norm(ALPHA * h_c + gf_c * f_c, ln_g[l, 1], ln_b[l, 1])
    return h_x
```

```python
import functools

import numpy as np
import jax
import jax.numpy as jnp
from jax import lax
from jax.experimental import pallas as pl
from jax.experimental.pallas import tpu as pltpu

F32 = jnp.float32
BF16 = jnp.bfloat16

HEAD_DIM = 64
N_HEADS = 4
N_KV = 2
GROUP_W = N_HEADS * HEAD_DIM
KV_W = N_KV * HEAD_DIM
GRID_W = 64
WIN = 128
BLOCK = 128
NA_ROWS = 8
NA_COLS = 16
ROPE_BASE = 10000.0
EPS = 1e-6
HGRN_L = 64
HGRN_LEVELS = 6
ONES_ROWS = 16

V7X_LANES = 128
V7X_VMEM_LIMIT_BYTES = 56 * 1024 * 1024

COL_QA, COL_KA, COL_VA = 0, 256, 384
COL_QB, COL_KB, COL_VB = 512, 768, 896
COL_C = 1024
COL_QD, COL_KD, COL_VD = 2304, 2560, 2816
IN_WIDTH = 3072
C_WIDTH = 5 * GROUP_W

NEG_INF = float("-inf")


def _cparams(sem):
    return pltpu.CompilerParams(dimension_semantics=sem, vmem_limit_bytes=V7X_VMEM_LIMIT_BYTES)


def _const_spec(shape):
    nd = len(shape)
    return pl.BlockSpec(shape, lambda *_: (0,) * nd, pipeline_mode=pl.Buffered(1))


def _silu(x):
    return x * jax.nn.sigmoid(x)


def _dot(a, b):
    return jnp.dot(a, b, preferred_element_type=F32)


def _dot_nt(a, b):
    return lax.dot_general(a, b, (((1,), (1,)), ((), ())), preferred_element_type=F32)


def _head_mask(shape, h):
    lane = lax.broadcasted_iota(jnp.int32, shape, len(shape) - 1)
    return (lane // HEAD_DIM) == h


def _stack_heads(q):
    zero = jnp.zeros_like(q)
    return jnp.concatenate([jnp.where(_head_mask(q.shape, h), q, zero) for h in range(N_HEADS)], axis=0)


def _unstack_heads(o, r):
    out = jnp.zeros((r, o.shape[1]), o.dtype)
    for h in range(N_HEADS):
        blk = o[h * r:(h + 1) * r]
        out = jnp.where(_head_mask(blk.shape, h), blk, out)
    return out


def _head_mean_sq(x, bd):
    sq = x * x
    hi = sq.astype(BF16)
    lo = (sq - hi.astype(F32)).astype(BF16)
    return _dot(hi, bd) + _dot(lo, bd)


def _layer_norm(z, g, b):
    mu = jnp.mean(z, axis=-1, keepdims=True)
    zc = z - mu
    var = jnp.mean(zc * zc, axis=-1, keepdims=True)
    return zc * lax.rsqrt(var + EPS) * g + b


def _mod_kernel(s_ref, w_ref, b_ref, o_ref):
    s = _silu(s_ref[...])
    o_ref[0] = jnp.dot(s, w_ref[0], preferred_element_type=F32,
                       precision=lax.Precision.HIGHEST) + b_ref[0]


def _modulation(c, c_ctx, w_mod, b_mod):
    depth, d, d6 = w_mod.shape
    b = c.shape[0]
    rows = 8
    assert b < rows
    s_in = jnp.zeros((rows, d), F32).at[:b].set(c).at[b].set(c_ctx)
    tn = d6 // 4
    return pl.pallas_call(
        _mod_kernel,
        grid=(depth, d6 // tn),
        in_specs=[pl.BlockSpec((rows, d), lambda l, j: (0, 0)),
                  pl.BlockSpec((1, d, tn), lambda l, j: (l, 0, j)),
                  pl.BlockSpec((1, 1, tn), lambda l, j: (l, 0, j))],
        out_specs=pl.BlockSpec((1, rows, tn), lambda l, j: (l, 0, j)),
        out_shape=jax.ShapeDtypeStruct((depth, rows, d6), F32),
        compiler_params=_cparams(("arbitrary", "arbitrary")),
        name="modulation",
    )(s_in, w_mod, b_mod.reshape(depth, 1, d6))


def _dup_heads(x):
    r = pltpu.roll(x, HEAD_DIM, 1)
    lo = lax.broadcasted_iota(jnp.int32, x.shape, 1) < HEAD_DIM
    return jnp.concatenate([jnp.where(lo, x, r), jnp.where(lo, r, x)], axis=1)


def _inproj_kernel(h_ref, sc_ref, sh_ref, w_ref, rc_ref, ra_ref, rb_ref, gq_ref, gk_ref, bd_ref,
                   *outs, is_ctx):
    if is_ctx:
        (qa_ref, ka_ref, va_ref, qb_ref, kb_ref, vbt_ref, c_ref, qd_ref, kd_ref, vd_ref,
         kbx_ref, vbx_ref) = outs
    else:
        qa_ref, ka_ref, va_ref, qbt_ref, kb_ref, vbt_ref, c_ref, qd_ref, kd_ref, vd_ref = outs
    hm = (h_ref[0] * (1.0 + sc_ref[0]) + sh_ref[0]).astype(BF16)
    rc, ra, rb = rc_ref[...], ra_ref[...], rb_ref[...]
    bd = bd_ref[...]
    scale = HEAD_DIM ** -0.5

    def proj(c0, width):
        return _dot(hm, w_ref[:, c0:c0 + width])

    def rope(x):
        return x * rc + pltpu.roll(x, V7X_LANES - 16, 1) * ra + pltpu.roll(x, 16, 1) * rb

    def rms(x, gain):
        return x * lax.rsqrt(_head_mean_sq(x, bd) + EPS) * gain

    xq = proj(COL_QA, GROUP_W)
    for j in range(2):
        sl = slice(j * V7X_LANES, (j + 1) * V7X_LANES)
        qa_ref[0, :, sl] = (rope(xq[:, sl]) * scale).astype(BF16)
    ka_ref[0] = _dup_heads(rope(proj(COL_KA, KV_W))).astype(BF16)
    va_ref[0] = _dup_heads(proj(COL_VA, KV_W)).astype(BF16)

    xq = proj(COL_QB, GROUP_W)
    gq, gk = gq_ref[...], gk_ref[...]
    for j in range(2):
        sl = slice(j * V7X_LANES, (j + 1) * V7X_LANES)
        y = rope(rms(xq[:, sl], gq)) * scale
        if is_ctx:
            qb_ref[0, :, sl] = y.astype(BF16)
        else:
            qbt_ref[0, sl, :] = y.T.astype(BF16)
    kb = rope(rms(proj(COL_KB, KV_W), gk))
    kb_ref[0] = kb.astype(BF16)
    vb = proj(COL_VB, KV_W)
    vbt_ref[0] = vb.T.astype(BF16)
    if is_ctx:
        kbx_ref[0] = _dup_heads(kb).astype(BF16)
        vbx_ref[0] = _dup_heads(vb).astype(BF16)

    for j in range(C_WIDTH // GROUP_W):
        c_ref[0, :, j * GROUP_W:(j + 1) * GROUP_W] = proj(COL_C + j * GROUP_W, GROUP_W)

    qd_ref[0] = (proj(COL_QD, GROUP_W) * scale).astype(BF16)
    kd_ref[0] = proj(COL_KD, GROUP_W).astype(BF16)
    vd_ref[0] = proj(COL_VD, GROUP_W).astype(BF16)


def _inproj(h, shift, scale, w_in, rope_tabs, gq, gk, bd128, *, is_ctx, tm):
    b, s, d = h.shape
    assert s % tm == 0
    rc, ra, rb = rope_tabs
    row = lambda width: pl.BlockSpec((1, tm, width), lambda bi, i: (bi, i, 0))
    col = lambda height: pl.BlockSpec((1, height, tm), lambda bi, i: (bi, 0, i))
    sds = lambda shape, dt=BF16: jax.ShapeDtypeStruct(shape, dt)
    if is_ctx:
        out_specs = [row(GROUP_W), row(GROUP_W), row(GROUP_W), row(GROUP_W), row(KV_W), col(KV_W),
                     row(C_WIDTH), row(GROUP_W), row(GROUP_W), row(GROUP_W), row(GROUP_W), row(GROUP_W)]
        out_shape = [sds((b, s, GROUP_W)), sds((b, s, GROUP_W)), sds((b, s, GROUP_W)), sds((b, s, GROUP_W)),
                     sds((b, s, KV_W)), sds((b, KV_W, s)), sds((b, s, C_WIDTH), F32),
                     sds((b, s, GROUP_W)), sds((b, s, GROUP_W)), sds((b, s, GROUP_W)),
                     sds((b, s, GROUP_W)), sds((b, s, GROUP_W))]
    else:
        out_specs = [row(GROUP_W), row(GROUP_W), row(GROUP_W), col(GROUP_W), row(KV_W), col(KV_W),
                     row(C_WIDTH), row(GROUP_W), row(GROUP_W), row(GROUP_W)]
        out_shape = [sds((b, s, GROUP_W)), sds((b, s, GROUP_W)), sds((b, s, GROUP_W)), sds((b, GROUP_W, s)),
                     sds((b, s, KV_W)), sds((b, KV_W, s)), sds((b, s, C_WIDTH), F32),
                     sds((b, s, GROUP_W)), sds((b, s, GROUP_W)), sds((b, s, GROUP_W))]
    tab = pl.BlockSpec((tm, V7X_LANES), lambda bi, i: (i, 0))
    vec = pl.BlockSpec((1, 1, d), lambda bi, i: (bi, 0, 0))
    return pl.pallas_call(
        functools.partial(_inproj_kernel, is_ctx=is_ctx),
        grid=(b, s // tm),
        in_specs=[pl.BlockSpec((1, tm, d), lambda bi, i: (bi, i, 0)), vec, vec,
                  _const_spec(w_in.shape), tab, tab, tab,
                  _const_spec(gq.shape), _const_spec(gk.shape), _const_spec(bd128.shape)],
        out_specs=out_specs,
        out_shape=out_shape,
        compiler_params=_cparams(("parallel", "arbitrary")),
        name="inproj_ctx" if is_ctx else "inproj",
    )(h, scale, shift, w_in, rc, ra, rb, gq, gk, bd128)


def _attend(qs, parts, sink_col):
    scores = []
    m = sink_col
    for k, _, bias in parts:
        s = _dot_nt(qs, k)
        if bias is not None:
            s = s + bias
        scores.append(s)
        smax = jnp.max(s, axis=1, keepdims=True)
        m = smax if m is None else jnp.maximum(m, smax)
    den = None if sink_col is None else jnp.exp(sink_col - m)
    out = None
    for s, (_, v, _) in zip(scores, parts):
        p = jnp.exp(s - m)
        psum = jnp.sum(p, axis=1, keepdims=True)
        den = psum if den is None else den + psum
        pv = _dot(p.astype(BF16), v)
        out = pv if out is None else out + pv
    return out * (1.0 / den)


def _window_kernel(sink_ref, q_ref, k_ref, v_ref, kc_ref, vc_ref, o_ref, *, seq):
    n = pl.program_id(1)
    span = 3 * BLOCK
    start = pl.multiple_of(jnp.clip((n - 1) * BLOCK, 0, seq - span), BLOCK)
    kw = k_ref[0, pl.ds(start, span), :]
    vw = v_ref[0, pl.ds(start, span), :]
    qs = _stack_heads(q_ref[0])
    qpos = n * BLOCK + lax.broadcasted_iota(jnp.int32, (BLOCK, span), 0)
    kpos = start + lax.broadcasted_iota(jnp.int32, (BLOCK, span), 1)
    bias = jnp.where(jnp.abs(qpos - kpos) <= WIN, 0.0, NEG_INF).astype(F32)
    bias = jnp.concatenate([bias] * N_HEADS, axis=0)
    sink_col = jnp.concatenate([jnp.full((BLOCK, 1), sink_ref[0, h], F32) for h in range(N_HEADS)], axis=0)
    o = _attend(qs, [(kw, vw, bias), (kc_ref[0], vc_ref[0], None)], sink_col)
    o_ref[0] = _unstack_heads(o, BLOCK).astype(BF16)


def _window_attn(q, k, v, kc, vc, sink):
    b, s, w = q.shape
    t = kc.shape[1]
    assert s % BLOCK == 0 and s >= 3 * BLOCK
    full = lambda n: pl.BlockSpec((1, n, w), lambda bi, i: (bi, 0, 0))
    return pl.pallas_call(
        functools.partial(_window_kernel, seq=s),
        grid=(b, s // BLOCK),
        in_specs=[pl.BlockSpec(memory_space=pltpu.SMEM),
                  pl.BlockSpec((1, BLOCK, w), lambda bi, i: (bi, i, 0)),
                  full(s), full(s), full(t), full(t)],
        out_specs=pl.BlockSpec((1, BLOCK, w), lambda bi, i: (bi, i, 0)),
        out_shape=jax.ShapeDtypeStruct((b, s, w), BF16),
        compiler_params=_cparams(("parallel", "arbitrary")),
        name="window_attn",
    )(sink.reshape(1, N_HEADS).astype(F32), q, k, v, kc, vc)


def _na_kernel(q_ref, k_ref, v_ref, kc_ref, vc_ref, bias_ref, o_ref, *, rows):
    r = pl.program_id(1)
    r0 = jnp.clip(r - NA_ROWS // 2, 0, rows - NA_ROWS)
    span = NA_ROWS * GRID_W
    start = pl.multiple_of(r0 * GRID_W, GRID_W)
    kw = k_ref[0, pl.ds(start, span), :]
    vw = v_ref[0, pl.ds(start, span), :]
    qs = _stack_heads(q_ref[0])
    o = _attend(qs, [(kw, vw, bias_ref[0]), (kc_ref[0], vc_ref[0], None)], None)
    o_ref[0] = _unstack_heads(o, GRID_W).astype(BF16)


def _na_bias_table(rpb):
    h = rpb.shape[0]
    q = np.arange(GRID_W)
    kc = np.arange(GRID_W)
    coff = np.clip(kc[None, :] - q[:, None], -(NA_COLS - 1), NA_COLS - 1) + (NA_COLS - 1)
    c0 = np.clip(q - NA_COLS // 2, 0, GRID_W - NA_COLS)
    ok = (kc[None, :] >= c0[:, None]) & (kc[None, :] < c0[:, None] + NA_COLS)
    tab = jnp.where(ok[None, None], rpb.astype(F32)[:, :, coff], NEG_INF)
    cls = np.arange(NA_ROWS)[:, None] + np.arange(NA_ROWS)[None, :]
    tab = tab[:, cls]
    return tab.transpose(1, 0, 3, 2, 4).reshape(NA_ROWS, h * GRID_W, NA_ROWS * GRID_W)


def _na_attn(q, k, v, kc, vc, bias_tab):
    b, s, w = q.shape
    t = kc.shape[1]
    rows = s // GRID_W
    assert s % GRID_W == 0 and rows >= NA_ROWS
    full = lambda n: pl.BlockSpec((1, n, w), lambda bi, i: (bi, 0, 0))

    def bias_idx(bi, r):
        return (jnp.clip(r - NA_ROWS // 2, 0, rows - NA_ROWS) - r + (NA_ROWS - 1), 0, 0)

    return pl.pallas_call(
        functools.partial(_na_kernel, rows=rows),
        grid=(b, rows),
        in_specs=[pl.BlockSpec((1, GRID_W, w), lambda bi, r: (bi, r, 0)),
                  full(s), full(s), full(t), full(t),
                  pl.BlockSpec((1,) + bias_tab.shape[1:], bias_idx)],
        out_specs=pl.BlockSpec((1, GRID_W, w), lambda bi, r: (bi, r, 0)),
        out_shape=jax.ShapeDtypeStruct((b, s, w), BF16),
        compiler_params=_cparams(("parallel", "arbitrary")),
        name="na_attn",
    )(q, k, v, kc, vc, bias_tab)


def _ctx_attn_kernel(sink_ref, qa_ref, ka_ref, va_ref, qb_ref, kb_ref, vb_ref, qd_ref, kd_ref, vd_ref,
                     oa_ref, ob_ref, od_ref):
    t = qa_ref.shape[1]
    sink_col = jnp.concatenate([jnp.full((t, 1), sink_ref[0, h], F32) for h in range(N_HEADS)], axis=0)
    for q_ref, k_ref, v_ref, o_ref, sink in ((qa_ref, ka_ref, va_ref, oa_ref, sink_col),
                                             (qb_ref, kb_ref, vb_ref, ob_ref, None),
                                             (qd_ref, kd_ref, vd_ref, od_ref, None)):
        o = _attend(_stack_heads(q_ref[0]), [(k_ref[0], v_ref[0], None)], sink)
        o_ref[0] = _unstack_heads(o, t).astype(BF16)


def _ctx_attn(sink, qa, ka, va, qb, kb, vb, qd, kd, vd):
    b, t, w = qa.shape
    blk = pl.BlockSpec((1, t, w), lambda bi: (bi, 0, 0))
    out = jax.ShapeDtypeStruct((b, t, w), BF16)
    return pl.pallas_call(
        _ctx_attn_kernel,
        grid=(b,),
        in_specs=[pl.BlockSpec(memory_space=pltpu.SMEM)] + [blk] * 9,
        out_specs=[blk] * 3,
        out_shape=[out] * 3,
        compiler_params=_cparams(("arbitrary",)),
        name="ctx_attn",
    )(sink.reshape(1, N_HEADS).astype(F32), qa, ka, va, qb, kb, vb, qd, kd, vd)


def _global_kernel(q_ref, k_ref, vt_ref, o_ref, *, kt, unroll):
    kv = pl.program_id(1)
    qt = q_ref[0]
    tq = qt.shape[1]
    n_cols = 2 * tq
    sel = (lax.broadcasted_iota(jnp.int32, qt.shape, 0) // HEAD_DIM) == kv
    zero = jnp.zeros_like(qt)

    def padded(g):
        qg = qt[g * HEAD_DIM:(g + 1) * HEAD_DIM, :]
        return jnp.where(sel, jnp.concatenate([qg, qg], axis=0), zero)

    rhs = jnp.concatenate([padded(0), padded(1)], axis=1)
    n_tiles = k_ref.shape[1] // kt

    def body(j, carry):
        m, acc = carry
        ks = k_ref[0, pl.ds(pl.multiple_of(j * kt, kt), kt), :]
        s = _dot(ks, rhs)
        m_new = jnp.maximum(m, jnp.max(s, axis=0, keepdims=True))
        alpha = jnp.exp(m - m_new)
        p = jnp.exp(s - m_new).astype(BF16)
        acc = acc * alpha + _dot(vt_ref[0, 0, j], p)
        return m_new, acc

    m0 = jnp.full((1, n_cols), NEG_INF, F32)
    acc0 = jnp.zeros((HEAD_DIM + ONES_ROWS, n_cols), F32)
    _, acc = lax.fori_loop(0, n_tiles, body, (m0, acc0), unroll=unroll)
    o = acc[0:HEAD_DIM] * (1.0 / acc[HEAD_DIM:HEAD_DIM + 1])
    ot = jnp.concatenate([o[:, :tq], o[:, tq:]], axis=0)
    o_ref[0] = ot.T.astype(BF16)


def _global_attn(qt, k_all, vt_tiles, *, tq, kt):
    b, w, s = qt.shape
    s_all = k_all.shape[1]
    n_tiles = s_all // kt
    assert s % tq == 0 and s_all % kt == 0
    unroll = 3 if n_tiles % 3 == 0 else 1
    return pl.pallas_call(
        functools.partial(_global_kernel, kt=kt, unroll=unroll),
        grid=(b, N_KV, s // tq),
        in_specs=[pl.BlockSpec((1, 2 * HEAD_DIM, tq), lambda bi, kv, i: (bi, kv, i)),
                  pl.BlockSpec((1, s_all, KV_W), lambda bi, kv, i: (bi, 0, 0)),
                  pl.BlockSpec((1, 1, n_tiles, HEAD_DIM + ONES_ROWS, kt), lambda bi, kv, i: (bi, kv, 0, 0, 0))],
        out_specs=pl.BlockSpec((1, tq, 2 * HEAD_DIM), lambda bi, kv, i: (bi, i, kv)),
        out_shape=jax.ShapeDtypeStruct((b, s, w), BF16),
        compiler_params=_cparams(("parallel", "arbitrary", "arbitrary")),
        name="global_attn",
    )(qt, k_all, vt_tiles)


def _hgrn_constants():
    n = HGRN_L
    t_rows = (HGRN_LEVELS + 2) * n + ONES_ROWS
    tm = np.zeros((2, t_rows, n), np.float32)
    mk = np.zeros((2, HGRN_LEVELS + 1, n, n), np.float32)
    for d in range(2):
        pos = np.arange(n) if d == 0 else n - 1 - np.arange(n)
        p, q = pos[:, None], pos[None, :]
        for li in range(HGRN_LEVELS):
            h = n >> (li + 1)
            same_blk = (p // (2 * h)) == (q // (2 * h))
            up_p, up_q = (p % (2 * h)) >= h, (q % (2 * h)) >= h
            same_half = same_blk & (up_p == up_q)
            tm[d, li * n:(li + 1) * n] = np.where(up_p, same_half & (q <= p), same_half & (q > p))
            mk[d, li] = same_blk & up_p & ~up_q
        mk[d, HGRN_LEVELS] = np.eye(n)
        tm[d, HGRN_LEVELS * n:(HGRN_LEVELS + 1) * n] = q <= p
        tm[d, (HGRN_LEVELS + 1) * n:(HGRN_LEVELS + 2) * n] = q > p
        tm[d, (HGRN_LEVELS + 2) * n:] = 1.0
    mk = np.tile(mk, (1, 1, N_HEADS, 1))
    return jnp.asarray(tm, BF16), jnp.asarray(mk, F32)


def _hgrn_kernel(lq_ref, li_ref, lz_ref, cq_ref, ci_ref, cz_ref, lb_ref, t_ref, mk_ref, bd_ref, *rest,
                 n_ctx_chunks, need_ctx):
    if need_ctx:
        ol_ref, oc_ref, st_ref = rest
    else:
        ol_ref, st_ref = rest
    step = pl.program_id(2)
    is_ctx = step < n_ctx_chunks
    n = HGRN_L

    @pl.when(step == 0)
    def _():
        st_ref[...] = jnp.zeros_like(st_ref)

    q = jnp.where(is_ctx, cq_ref[0], lq_ref[0])
    v = jnp.where(is_ctx, ci_ref[0], li_ref[0])
    z = jnp.where(is_ctx, cz_ref[0], lz_ref[0])
    lb = lb_ref[0]
    f = lb + (1.0 - lb) * jax.nn.sigmoid(z)
    kk = 1.0 - f
    g = jnp.log(f)
    qq = _silu(q)

    g_hi = g.astype(BF16)
    r1 = g - g_hi.astype(F32)
    g_mid = r1.astype(BF16)
    g_lo = (r1 - g_mid.astype(F32)).astype(BF16)
    tmat = t_ref[0]
    e = _dot(tmat, g_hi) + _dot(tmat, g_mid) + _dot(tmat, g_lo)

    a = jnp.zeros((N_HEADS * n, n), F32)
    for li in range(HGRN_LEVELS + 1):
        if li < HGRN_LEVELS:
            sc = jnp.exp(e[li * n:(li + 1) * n])
            qs, ks = qq * sc, kk * sc
        else:
            qs, ks = qq, kk
        a = a + _dot_nt(_stack_heads(qs.astype(BF16)), ks.astype(BF16)) * mk_ref[0, li]
    b_incl = e[HGRN_LEVELS * n:(HGRN_LEVELS + 1) * n]
    b_rem = e[(HGRN_LEVELS + 1) * n:(HGRN_LEVELS + 2) * n]
    b_tot = e[(HGRN_LEVELS + 2) * n:(HGRN_LEVELS + 2) * n + 1]
    q_hat = (qq * jnp.exp(b_incl)).astype(BF16)
    k_hat = (kk * jnp.exp(b_rem)).astype(BF16)
    vb = v.astype(BF16)

    o = _unstack_heads(_dot(a.astype(BF16), vb), n)
    st = st_ref[...]
    o = o + _dot_nt(q_hat, st.astype(BF16))
    upd = _dot(v.T.astype(BF16), k_hat)
    st_ref[...] = st * jnp.exp(b_tot) + upd * bd_ref[...]

    @pl.when(jnp.logical_not(is_ctx))
    def _():
        ol_ref[0, 0] = o

    if need_ctx:
        @pl.when(is_ctx)
        def _():
            oc_ref[0, 0] = o


def _hgrn(c_lat, c_ctx, lb, consts, *, need_ctx):
    tmat, masks, bd = consts
    b, s, _ = c_lat.shape
    t = c_ctx.shape[1]
    n = HGRN_L
    assert s % n == 0 and t % n == 0
    n_lat, n_ctx = s // n, t // n

    def lat_chunk(d, st):
        i = jnp.clip(st - n_ctx, 0, n_lat - 1)
        return i + d * (n_lat - 1 - 2 * i)

    def ctx_chunk(d, st):
        i = jnp.clip(st, 0, n_ctx - 1)
        return i + d * (n_ctx - 1 - 2 * i)

    def lat_spec(colf):
        return pl.BlockSpec((1, n, GROUP_W), lambda bi, d, st: (bi, lat_chunk(d, st), colf(d)))

    def ctx_spec(colf):
        return pl.BlockSpec((1, n, GROUP_W), lambda bi, d, st: (bi, ctx_chunk(d, st), colf(d)))

    cols = (lambda d: 0, lambda d: 1, lambda d: 2 + d)
    out_specs = [pl.BlockSpec((1, 1, n, GROUP_W), lambda bi, d, st: (d, bi, lat_chunk(d, st), 0))]
    out_shape = [jax.ShapeDtypeStruct((2, b, s, GROUP_W), F32)]
    if need_ctx:
        out_specs.append(pl.BlockSpec((1, 1, n, GROUP_W), lambda bi, d, st: (d, bi, ctx_chunk(d, st), 0)))
        out_shape.append(jax.ShapeDtypeStruct((2, b, t, GROUP_W), F32))
    outs = pl.pallas_call(
        functools.partial(_hgrn_kernel, n_ctx_chunks=n_ctx, need_ctx=need_ctx),
        grid=(b, 2, n_ctx + n_lat),
        in_specs=[lat_spec(cols[0]), lat_spec(cols[1]), lat_spec(cols[2]),
                  ctx_spec(cols[0]), ctx_spec(cols[1]), ctx_spec(cols[2]),
                  pl.BlockSpec((1, 1, GROUP_W), lambda bi, d, st: (d, 0, 0)),
                  pl.BlockSpec((1,) + tmat.shape[1:], lambda bi, d, st: (d, 0, 0)),
                  pl.BlockSpec((1,) + masks.shape[1:], lambda bi, d, st: (d, 0, 0, 0)),
                  _const_spec(bd.shape)],
        out_specs=out_specs,
        out_shape=out_shape,
        scratch_shapes=[pltpu.VMEM((GROUP_W, GROUP_W), F32)],
        compiler_params=_cparams(("parallel", "arbitrary", "arbitrary")),
        name="hgrn_ctx" if need_ctx else "hgrn",
    )(c_lat, c_lat, c_lat, c_ctx, c_ctx, c_ctx, lb.reshape(2, 1, GROUP_W), tmat, masks, bd)
    return outs if need_ctx else (outs[0], None)


def _outproj_kernel(x_ref, oa_ref, ob_ref, of_ref, obw_ref, gate_ref, od_ref, w_ref, onorm_ref, bd_ref,
                    ga_ref, lng_ref, lnb_ref, o_ref, *, alpha):
    oc = of_ref[0, 0] + obw_ref[0, 0]
    y = oc * lax.rsqrt(_head_mean_sq(oc, bd_ref[...]) + EPS) * onorm_ref[...]
    y = y * _silu(gate_ref[0])
    mix = jnp.concatenate([oa_ref[0], ob_ref[0], y.astype(BF16), od_ref[0]], axis=1)
    z = alpha * x_ref[0] + ga_ref[0] * _dot(mix, w_ref[...])
    o_ref[0] = _layer_norm(z, lng_ref[...], lnb_ref[...])


def _outproj(x, oa, ob, o_dirs, c_raw, od, w_out, onorm, bd256, gate_a, ln_g, ln_b, *, alpha, tm):
    b, s, d = x.shape
    assert s % tm == 0
    grp = pl.BlockSpec((1, tm, GROUP_W), lambda bi, i: (bi, i, 0))
    vec = pl.BlockSpec((1, 1, d), lambda bi, i: (bi, 0, 0))
    return pl.pallas_call(
        functools.partial(_outproj_kernel, alpha=alpha),
        grid=(b, s // tm),
        in_specs=[pl.BlockSpec((1, tm, d), lambda bi, i: (bi, i, 0)), grp, grp,
                  pl.BlockSpec((1, 1, tm, GROUP_W), lambda bi, i: (0, bi, i, 0)),
                  pl.BlockSpec((1, 1, tm, GROUP_W), lambda bi, i: (1, bi, i, 0)),
                  pl.BlockSpec((1, tm, GROUP_W), lambda bi, i: (bi, i, 4)),
                  grp, _const_spec(w_out.shape), _const_spec(onorm.shape), _const_spec(bd256.shape),
                  vec, _const_spec(ln_g.shape), _const_spec(ln_b.shape)],
        out_specs=pl.BlockSpec((1, tm, d), lambda bi, i: (bi, i, 0)),
        out_shape=jax.ShapeDtypeStruct((b, s, d), F32),
        compiler_params=_cparams(("parallel", "arbitrary")),
        name="outproj",
    )(x, oa, ob, o_dirs, o_dirs, c_raw, od, w_out, onorm, bd256, gate_a, ln_g, ln_b)


def _ffn_kernel(h_ref, hp_ref, hn_ref, sc_ref, sh_ref, gt_ref, wup_ref, cw_ref, cb_ref, wdn_ref,
                lng_ref, lnb_ref, o_ref, g_scr, *, alpha, d_ff, cwid, n_tiles):
    i = pl.program_id(1)
    tm = h_ref.shape[1]
    sc = 1.0 + sc_ref[0]
    sh = sh_ref[0]
    h = h_ref[0]
    halo = jnp.concatenate([hp_ref[0], hn_ref[0]], axis=0)
    lhs = jnp.concatenate([(h * sc + sh).astype(BF16), (halo * sc + sh).astype(BF16)], axis=0)
    has_prev = (i > 0).astype(F32)
    has_next = (i < n_tiles - 1).astype(F32)
    row = lax.broadcasted_iota(jnp.int32, (tm, cwid), 0)
    first, last = row == 0, row == tm - 1

    def conv_cols(c0):
        u = _dot(lhs, wup_ref[:, c0:c0 + cwid])
        um = u[:tm]
        u_prev = u[tm + 7:tm + 8] * has_prev
        u_next = u[tm + 8:tm + 9] * has_next
        dn = jnp.where(first, u_prev, pltpu.roll(um, 1, 0))
        up = jnp.where(last, u_next, pltpu.roll(um, tm - 1, 0))
        w = cw_ref[:, c0:c0 + cwid]
        return dn * w[0:1] + um * w[1:2] + up * w[2:3] + cb_ref[:, c0:c0 + cwid]

    for j in range(d_ff // cwid):
        a = conv_cols(j * cwid)
        v = conv_cols(d_ff + j * cwid)
        g_scr[:, j * cwid:(j + 1) * cwid] = (_silu(a) * v).astype(BF16)
    z = alpha * h + gt_ref[0] * _dot(g_scr[...], wdn_ref[...])
    o_ref[0] = _layer_norm(z, lng_ref[...], lnb_ref[...])


def _ffn(h, scale, shift, gate, w_up, conv_w, conv_b, w_down, ln_g, ln_b, *, alpha, tm, cwid):
    b, s, d = h.shape
    d_ff = w_down.shape[0]
    assert s % tm == 0 and tm % 16 == 0 and d_ff % cwid == 0
    n_tiles = s // tm
    r8 = tm // 8
    vec = pl.BlockSpec((1, 1, d), lambda bi, i: (bi, 0, 0))
    return pl.pallas_call(
        functools.partial(_ffn_kernel, alpha=alpha, d_ff=d_ff, cwid=cwid, n_tiles=n_tiles),
        grid=(b, n_tiles),
        in_specs=[pl.BlockSpec((1, tm, d), lambda bi, i: (bi, i, 0)),
                  pl.BlockSpec((1, 8, d), lambda bi, i: (bi, jnp.maximum(i * r8 - 1, 0), 0)),
                  pl.BlockSpec((1, 8, d), lambda bi, i: (bi, jnp.minimum((i + 1) * r8, s // 8 - 1), 0)),
                  vec, vec, vec,
                  _const_spec(w_up.shape), _const_spec(conv_w.shape), _const_spec(conv_b.shape),
                  _const_spec(w_down.shape), _const_spec(ln_g.shape), _const_spec(ln_b.shape)],
        out_specs=pl.BlockSpec((1, tm, d), lambda bi, i: (bi, i, 0)),
        out_shape=jax.ShapeDtypeStruct((b, s, d), F32),
        scratch_shapes=[pltpu.VMEM((tm, d_ff), BF16)],
        compiler_params=_cparams(("parallel", "arbitrary")),
        name="conv_ffn",
    )(h, h, h, scale, shift, gate, w_up, conv_w, conv_b, w_down, ln_g, ln_b)


def _rope_tables(n, with_rotation):
    lane = np.arange(V7X_LANES)
    if not with_rotation:
        one = jnp.ones((n, V7X_LANES), F32)
        zero = jnp.zeros((n, V7X_LANES), F32)
        return one, zero, zero
    t = jnp.arange(n)
    pos = jnp.stack([t // GRID_W, t % GRID_W], -1).astype(F32)
    n_freq = HEAD_DIM // 4
    inv = ROPE_BASE ** (-jnp.arange(n_freq, dtype=F32) / n_freq)
    ang = pos[:, :, None] * inv
    axis = (lane % HEAD_DIM) // (HEAD_DIM // 2)
    freq = lane % n_freq
    ang_l = ang[:, axis, freq]
    first = jnp.asarray((lane % (2 * n_freq)) < n_freq)
    cos, sin = jnp.cos(ang_l), jnp.sin(ang_l)
    return cos, jnp.where(first, -sin, 0.0), jnp.where(first, 0.0, sin)


def _block_diag_mean(width):
    lane = np.arange(width)
    same = (lane[:, None] // HEAD_DIM) == (lane[None, :] // HEAD_DIM)
    return jnp.asarray(same / HEAD_DIM, BF16)


def kernel(x, c, ctx, c_ctx, w_mod, b_mod, w_in, w_out, ln_g, ln_b, attn_sink, qk_norm, hgrn_lb_logits,
           hgrn_onorm, na_rpb, ffn_w_up, ffn_conv_w, ffn_conv_b, ffn_w_down):
    depth = w_mod.shape[0]
    b, s, d = x.shape
    t = ctx.shape[1]
    alpha = (2 * depth) ** 0.25
    s_all = s + t
    kt = 256
    assert s_all % kt == 0

    mods = _modulation(c, c_ctx, w_mod, b_mod)
    rope_lat = _rope_tables(s, True)
    rope_ctx = _rope_tables(t, False)
    bd128, bd256 = _block_diag_mean(KV_W), _block_diag_mean(GROUP_W)
    lane = np.arange(GROUP_W)
    head_bd = jnp.asarray((lane[:, None] // HEAD_DIM) == (lane[None, :] // HEAD_DIM), F32)
    hgrn_consts = _hgrn_constants() + (head_bd,)
    p_lb = jax.nn.softmax(hgrn_lb_logits.astype(F32), axis=1)
    lb_all = jnp.cumsum(p_lb, axis=1) - p_lb[:, :1]
    w_in_b, w_out_b = w_in.astype(BF16), w_out.astype(BF16)
    w_up_b, w_dn_b = ffn_w_up.astype(BF16), ffn_w_down.astype(BF16)

    h_x, h_c = x, ctx
    for l in range(depth):
        need_ctx = l < depth - 1
        mod = mods[l]
        sa, ca, ga, sf, cf, gf = (mod[:, i * d:(i + 1) * d] for i in range(6))
        lat = lambda a: a[:b, None, :]
        con = lambda a: jnp.broadcast_to(a[b][None, None, :], (b, 1, d))
        gq = jnp.tile(qk_norm[l, 0], 2)[None, :]
        gk = jnp.tile(qk_norm[l, 1], 2)[None, :]

        (qa, ka, va, qbt, kb, vbt, c_raw, qd, kd, vd) = _inproj(
            h_x, lat(sa), lat(ca), w_in_b[l], rope_lat, gq, gk, bd128, is_ctx=False, tm=512)
        (qa_c, ka_c, va_c, qb_c, kb_c, vbt_c, c_raw_c, qd_c, kd_c, vd_c, kbx_c, vbx_c) = _inproj(
            h_c, con(sa), con(ca), w_in_b[l], rope_ctx, gq, gk, bd128, is_ctx=True, tm=t)

        o_a = _window_attn(qa, ka, va, ka_c, va_c, attn_sink[l])

        k_all = jnp.concatenate([kb, kb_c], axis=1)
        vt_all = jnp.concatenate([vbt, vbt_c], axis=2).reshape(b, N_KV, HEAD_DIM, s_all)
        vt_all = jnp.concatenate([vt_all, jnp.ones((b, N_KV, ONES_ROWS, s_all), BF16)], axis=2)
        vt_tiles = vt_all.reshape(b, N_KV, HEAD_DIM + ONES_ROWS, s_all // kt, kt).transpose(0, 1, 3, 2, 4)
        o_b = _global_attn(qbt, k_all, vt_tiles, tq=128, kt=kt)

        o_dirs, oc_dirs = _hgrn(c_raw, c_raw_c, lb_all[:, l], hgrn_consts, need_ctx=need_ctx)

        o_d = _na_attn(qd, kd, vd, kd_c, vd_c, _na_bias_table(na_rpb[l]))

        onorm = hgrn_onorm[l][None, :]
        lng0, lnb0 = ln_g[l, 0][None, :], ln_b[l, 0][None, :]
        lng1, lnb1 = ln_g[l, 1][None, :], ln_b[l, 1][None, :]
        cw, cb = ffn_conv_w[l], ffn_conv_b[l][None, :]
        h_x = _outproj(h_x, o_a, o_b, o_dirs, c_raw, o_d, w_out_b[l], onorm, bd256, lat(ga), lng0, lnb0,
                       alpha=alpha, tm=512)
        h_x = _ffn(h_x, lat(cf), lat(sf), lat(gf), w_up_b[l], cw, cb, w_dn_b[l], lng1, lnb1,
                   alpha=alpha, tm=256, cwid=256)
        if need_ctx:
            oa_c, ob_c, od_c = _ctx_attn(attn_sink[l], qa_c, ka_c, va_c, qb_c, kbx_c, vbx_c, qd_c, kd_c, vd_c)
            h_c = _outproj(h_c, oa_c, ob_c, oc_dirs, c_raw_c, od_c, w_out_b[l], onorm, bd256, con(ga),
                           lng0, lnb0, alpha=alpha, tm=t)
            h_c = _ffn(h_c, con(cf), con(sf), con(gf), w_up_b[l], cw, cb, w_dn_b[l], lng1, lnb1,
                       alpha=alpha, tm=t, cwid=256)
    return h_x
```

```python
import functools

import numpy as np
import jax
import jax.numpy as jnp
from jax import lax
from jax.experimental import pallas as pl
from jax.experimental.pallas import tpu as pltpu

F32 = jnp.float32
BF16 = jnp.bfloat16

HEAD_DIM = 64
N_HEADS = 4
N_KV = 2
GROUP_W = N_HEADS * HEAD_DIM
KV_W = N_KV * HEAD_DIM
GRID_W = 64
WIN = 128
BLOCK = 128
NA_ROWS = 8
NA_COLS = 16
ROPE_BASE = 10000.0
EPS = 1e-6
HGRN_L = 64
HGRN_LEVELS = 6
ONES_ROWS = 16

V7X_LANES = 128
V7X_VMEM_LIMIT_BYTES = 56 * 1024 * 1024

COL_QA, COL_KA, COL_VA = 0, 256, 384
COL_QB, COL_KB, COL_VB = 512, 768, 896
COL_C = 1024
COL_QD, COL_KD, COL_VD = 2304, 2560, 2816
IN_WIDTH = 3072
C_WIDTH = 5 * GROUP_W

NEG_INF = float("-inf")
LOG2_E = 1.4426950408889634


def _cparams(sem):
    return pltpu.CompilerParams(dimension_semantics=sem, vmem_limit_bytes=V7X_VMEM_LIMIT_BYTES)


def _const_spec(shape):
    nd = len(shape)
    return pl.BlockSpec(shape, lambda *_: (0,) * nd, pipeline_mode=pl.Buffered(1))


def _silu(x):
    return x * jax.nn.sigmoid(x)


def _dot(a, b):
    return jnp.dot(a, b, preferred_element_type=F32)


def _dot_nt(a, b):
    return lax.dot_general(a, b, (((1,), (1,)), ((), ())), preferred_element_type=F32)


def _head_mask(shape, h):
    lane = lax.broadcasted_iota(jnp.int32, shape, len(shape) - 1)
    return (lane // HEAD_DIM) == h


def _stack_heads(q):
    zero = jnp.zeros_like(q)
    return jnp.concatenate([jnp.where(_head_mask(q.shape, h), q, zero) for h in range(N_HEADS)], axis=0)


def _unstack_heads(o, r):
    out = jnp.zeros((r, o.shape[1]), o.dtype)
    for h in range(N_HEADS):
        blk = o[h * r:(h + 1) * r]
        out = jnp.where(_head_mask(blk.shape, h), blk, out)
    return out


def _head_mean_sq(x, bd):
    sq = x * x
    hi = sq.astype(BF16)
    lo = (sq - hi.astype(F32)).astype(BF16)
    return _dot(hi, bd) + _dot(lo, bd)


def _layer_norm(z, g, b):
    mu = jnp.mean(z, axis=-1, keepdims=True)
    zc = z - mu
    var = jnp.mean(zc * zc, axis=-1, keepdims=True)
    return zc * lax.rsqrt(var + EPS) * g + b


def _mod_kernel(s_ref, w_ref, b_ref, o_ref):
    s = _silu(s_ref[...])
    o_ref[0] = jnp.dot(s, w_ref[0], preferred_element_type=F32,
                       precision=lax.Precision.HIGHEST) + b_ref[0]


def _modulation(c, c_ctx, w_mod, b_mod):
    depth, d, d6 = w_mod.shape
    b = c.shape[0]
    rows = 8
    assert b < rows
    s_in = jnp.zeros((rows, d), F32).at[:b].set(c).at[b].set(c_ctx)
    tn = d6 // 4
    return pl.pallas_call(
        _mod_kernel,
        grid=(depth, d6 // tn),
        in_specs=[pl.BlockSpec((rows, d), lambda l, j: (0, 0)),
                  pl.BlockSpec((1, d, tn), lambda l, j: (l, 0, j)),
                  pl.BlockSpec((1, 1, tn), lambda l, j: (l, 0, j))],
        out_specs=pl.BlockSpec((1, rows, tn), lambda l, j: (l, 0, j)),
        out_shape=jax.ShapeDtypeStruct((depth, rows, d6), F32),
        compiler_params=_cparams(("arbitrary", "arbitrary")),
        name="modulation",
    )(s_in, w_mod, b_mod.reshape(depth, 1, d6))


def _dup_heads(x):
    r = pltpu.roll(x, HEAD_DIM, 1)
    lo = lax.broadcasted_iota(jnp.int32, x.shape, 1) < HEAD_DIM
    return jnp.concatenate([jnp.where(lo, x, r), jnp.where(lo, r, x)], axis=1)


def _inproj_kernel(h_ref, sc_ref, sh_ref, w_ref, rc_ref, ra_ref, rb_ref, gq_ref, gk_ref, bd_ref,
                   *outs, is_ctx):
    if is_ctx:
        (qa_ref, ka_ref, va_ref, qb_ref, kb_ref, vbt_ref, c_ref, qd_ref, kd_ref, vd_ref,
         kbx_ref, vbx_ref) = outs
    else:
        qa_ref, ka_ref, va_ref, qbt_ref, kb_ref, vbt_ref, c_ref, qd_ref, kd_ref, vd_ref = outs
    hm = (h_ref[0] * (1.0 + sc_ref[0]) + sh_ref[0]).astype(BF16)
    rc, ra, rb = rc_ref[...], ra_ref[...], rb_ref[...]
    bd = bd_ref[...]
    scale = HEAD_DIM ** -0.5

    def proj(c0, width):
        return _dot(hm, w_ref[:, c0:c0 + width])

    def rope(x):
        return x * rc + pltpu.roll(x, V7X_LANES - 16, 1) * ra + pltpu.roll(x, 16, 1) * rb

    def rms(x, gain):
        return x * lax.rsqrt(_head_mean_sq(x, bd) + EPS) * gain

    xq = proj(COL_QA, GROUP_W)
    for j in range(2):
        sl = slice(j * V7X_LANES, (j + 1) * V7X_LANES)
        qa_ref[0, :, sl] = (rope(xq[:, sl]) * scale).astype(BF16)
    ka_ref[0] = _dup_heads(rope(proj(COL_KA, KV_W))).astype(BF16)
    va_ref[0] = _dup_heads(proj(COL_VA, KV_W)).astype(BF16)

    xq = proj(COL_QB, GROUP_W)
    gq, gk = gq_ref[...], gk_ref[...]
    scale_b = scale if is_ctx else scale * LOG2_E
    for j in range(2):
        sl = slice(j * V7X_LANES, (j + 1) * V7X_LANES)
        y = rope(rms(xq[:, sl], gq)) * scale_b
        if is_ctx:
            qb_ref[0, :, sl] = y.astype(BF16)
        else:
            qbt_ref[0, sl, :] = y.T.astype(BF16)
    kb = rope(rms(proj(COL_KB, KV_W), gk))
    kb_ref[0] = kb.astype(BF16)
    vb = proj(COL_VB, KV_W)
    vbt_ref[0] = vb.T.astype(BF16)
    if is_ctx:
        kbx_ref[0] = _dup_heads(kb).astype(BF16)
        vbx_ref[0] = _dup_heads(vb).astype(BF16)

    for j in range(C_WIDTH // GROUP_W):
        c_ref[0, :, j * GROUP_W:(j + 1) * GROUP_W] = proj(COL_C + j * GROUP_W, GROUP_W)

    qd_ref[0] = (proj(COL_QD, GROUP_W) * scale).astype(BF16)
    kd_ref[0] = proj(COL_KD, GROUP_W).astype(BF16)
    vd_ref[0] = proj(COL_VD, GROUP_W).astype(BF16)


def _inproj(h, shift, scale, w_in, rope_tabs, gq, gk, bd128, *, is_ctx, tm):
    b, s, d = h.shape
    assert s % tm == 0
    rc, ra, rb = rope_tabs
    row = lambda width: pl.BlockSpec((1, tm, width), lambda bi, i: (bi, i, 0))
    col = lambda height: pl.BlockSpec((1, height, tm), lambda bi, i: (bi, 0, i))
    sds = lambda shape, dt=BF16: jax.ShapeDtypeStruct(shape, dt)
    if is_ctx:
        out_specs = [row(GROUP_W), row(GROUP_W), row(GROUP_W), row(GROUP_W), row(KV_W), col(KV_W),
                     row(C_WIDTH), row(GROUP_W), row(GROUP_W), row(GROUP_W), row(GROUP_W), row(GROUP_W)]
        out_shape = [sds((b, s, GROUP_W)), sds((b, s, GROUP_W)), sds((b, s, GROUP_W)), sds((b, s, GROUP_W)),
                     sds((b, s, KV_W)), sds((b, KV_W, s)), sds((b, s, C_WIDTH), F32),
                     sds((b, s, GROUP_W)), sds((b, s, GROUP_W)), sds((b, s, GROUP_W)),
                     sds((b, s, GROUP_W)), sds((b, s, GROUP_W))]
    else:
        out_specs = [row(GROUP_W), row(GROUP_W), row(GROUP_W), col(GROUP_W), row(KV_W), col(KV_W),
                     row(C_WIDTH), row(GROUP_W), row(GROUP_W), row(GROUP_W)]
        out_shape = [sds((b, s, GROUP_W)), sds((b, s, GROUP_W)), sds((b, s, GROUP_W)), sds((b, GROUP_W, s)),
                     sds((b, s, KV_W)), sds((b, KV_W, s)), sds((b, s, C_WIDTH), F32),
                     sds((b, s, GROUP_W)), sds((b, s, GROUP_W)), sds((b, s, GROUP_W))]
    tab = pl.BlockSpec((tm, V7X_LANES), lambda bi, i: (i, 0))
    vec = pl.BlockSpec((1, 1, d), lambda bi, i: (bi, 0, 0))
    return pl.pallas_call(
        functools.partial(_inproj_kernel, is_ctx=is_ctx),
        grid=(b, s // tm),
        in_specs=[pl.BlockSpec((1, tm, d), lambda bi, i: (bi, i, 0)), vec, vec,
                  _const_spec(w_in.shape), tab, tab, tab,
                  _const_spec(gq.shape), _const_spec(gk.shape), _const_spec(bd128.shape)],
        out_specs=out_specs,
        out_shape=out_shape,
        compiler_params=_cparams(("parallel", "arbitrary")),
        name="inproj_ctx" if is_ctx else "inproj",
    )(h, scale, shift, w_in, rc, ra, rb, gq, gk, bd128)


def _attend(qs, parts, sink_col):
    scores = []
    m = sink_col
    for k, _, bias in parts:
        s = _dot_nt(qs, k)
        if bias is not None:
            s = s + bias
        scores.append(s)
        smax = jnp.max(s, axis=1, keepdims=True)
        m = smax if m is None else jnp.maximum(m, smax)
    den = None if sink_col is None else jnp.exp(sink_col - m)
    out = None
    for s, (_, v, _) in zip(scores, parts):
        p = jnp.exp(s - m)
        psum = jnp.sum(p, axis=1, keepdims=True)
        den = psum if den is None else den + psum
        pv = _dot(p.astype(BF16), v)
        out = pv if out is None else out + pv
    return out * (1.0 / den)


def _window_kernel(sink_ref, q_ref, k_ref, v_ref, kc_ref, vc_ref, o_ref, *, seq):
    n = pl.program_id(1)
    span = 3 * BLOCK
    start = pl.multiple_of(jnp.clip((n - 1) * BLOCK, 0, seq - span), BLOCK)
    kw = k_ref[0, pl.ds(start, span), :]
    vw = v_ref[0, pl.ds(start, span), :]
    qs = _stack_heads(q_ref[0])
    qpos = n * BLOCK + lax.broadcasted_iota(jnp.int32, (BLOCK, span), 0)
    kpos = start + lax.broadcasted_iota(jnp.int32, (BLOCK, span), 1)
    bias = jnp.where(jnp.abs(qpos - kpos) <= WIN, 0.0, NEG_INF).astype(F32)
    bias = jnp.concatenate([bias] * N_HEADS, axis=0)
    sink_col = jnp.concatenate([jnp.full((BLOCK, 1), sink_ref[0, h], F32) for h in range(N_HEADS)], axis=0)
    o = _attend(qs, [(kw, vw, bias), (kc_ref[0], vc_ref[0], None)], sink_col)
    o_ref[0] = _unstack_heads(o, BLOCK).astype(BF16)


def _window_attn(q, k, v, kc, vc, sink):
    b, s, w = q.shape
    t = kc.shape[1]
    assert s % BLOCK == 0 and s >= 3 * BLOCK
    full = lambda n: pl.BlockSpec((1, n, w), lambda bi, i: (bi, 0, 0))
    return pl.pallas_call(
        functools.partial(_window_kernel, seq=s),
        grid=(b, s // BLOCK),
        in_specs=[pl.BlockSpec(memory_space=pltpu.SMEM),
                  pl.BlockSpec((1, BLOCK, w), lambda bi, i: (bi, i, 0)),
                  full(s), full(s), full(t), full(t)],
        out_specs=pl.BlockSpec((1, BLOCK, w), lambda bi, i: (bi, i, 0)),
        out_shape=jax.ShapeDtypeStruct((b, s, w), BF16),
        compiler_params=_cparams(("parallel", "arbitrary")),
        name="window_attn",
    )(sink.reshape(1, N_HEADS).astype(F32), q, k, v, kc, vc)


def _na_kernel(q_ref, k_ref, v_ref, kc_ref, vc_ref, bias_ref, o_ref, *, rows):
    r = pl.program_id(1)
    r0 = jnp.clip(r - NA_ROWS // 2, 0, rows - NA_ROWS)
    span = NA_ROWS * GRID_W
    start = pl.multiple_of(r0 * GRID_W, GRID_W)
    kw = k_ref[0, pl.ds(start, span), :]
    vw = v_ref[0, pl.ds(start, span), :]
    qs = _stack_heads(q_ref[0])
    o = _attend(qs, [(kw, vw, bias_ref[0]), (kc_ref[0], vc_ref[0], None)], None)
    o_ref[0] = _unstack_heads(o, GRID_W).astype(BF16)


def _na_bias_table(rpb):
    h = rpb.shape[0]
    q = np.arange(GRID_W)
    kc = np.arange(GRID_W)
    coff = np.clip(kc[None, :] - q[:, None], -(NA_COLS - 1), NA_COLS - 1) + (NA_COLS - 1)
    c0 = np.clip(q - NA_COLS // 2, 0, GRID_W - NA_COLS)
    ok = (kc[None, :] >= c0[:, None]) & (kc[None, :] < c0[:, None] + NA_COLS)
    tab = jnp.where(ok[None, None], rpb.astype(F32)[:, :, coff], NEG_INF)
    cls = np.arange(NA_ROWS)[:, None] + np.arange(NA_ROWS)[None, :]
    tab = tab[:, cls]
    return tab.transpose(1, 0, 3, 2, 4).reshape(NA_ROWS, h * GRID_W, NA_ROWS * GRID_W)


def _na_attn(q, k, v, kc, vc, bias_tab):
    b, s, w = q.shape
    t = kc.shape[1]
    rows = s // GRID_W
    assert s % GRID_W == 0 and rows >= NA_ROWS
    full = lambda n: pl.BlockSpec((1, n, w), lambda bi, i: (bi, 0, 0))

    def bias_idx(bi, r):
        return (jnp.clip(r - NA_ROWS // 2, 0, rows - NA_ROWS) - r + (NA_ROWS - 1), 0, 0)

    return pl.pallas_call(
        functools.partial(_na_kernel, rows=rows),
        grid=(b, rows),
        in_specs=[pl.BlockSpec((1, GRID_W, w), lambda bi, r: (bi, r, 0)),
                  full(s), full(s), full(t), full(t),
                  pl.BlockSpec((1,) + bias_tab.shape[1:], bias_idx)],
        out_specs=pl.BlockSpec((1, GRID_W, w), lambda bi, r: (bi, r, 0)),
        out_shape=jax.ShapeDtypeStruct((b, s, w), BF16),
        compiler_params=_cparams(("parallel", "arbitrary")),
        name="na_attn",
    )(q, k, v, kc, vc, bias_tab)


def _ctx_attn_kernel(sink_ref, qa_ref, ka_ref, va_ref, qb_ref, kb_ref, vb_ref, qd_ref, kd_ref, vd_ref,
                     oa_ref, ob_ref, od_ref):
    t = qa_ref.shape[1]
    sink_col = jnp.concatenate([jnp.full((t, 1), sink_ref[0, h], F32) for h in range(N_HEADS)], axis=0)
    for q_ref, k_ref, v_ref, o_ref, sink in ((qa_ref, ka_ref, va_ref, oa_ref, sink_col),
                                             (qb_ref, kb_ref, vb_ref, ob_ref, None),
                                             (qd_ref, kd_ref, vd_ref, od_ref, None)):
        o = _attend(_stack_heads(q_ref[0]), [(k_ref[0], v_ref[0], None)], sink)
        o_ref[0] = _unstack_heads(o, t).astype(BF16)


def _ctx_attn(sink, qa, ka, va, qb, kb, vb, qd, kd, vd):
    b, t, w = qa.shape
    blk = pl.BlockSpec((1, t, w), lambda bi: (bi, 0, 0))
    out = jax.ShapeDtypeStruct((b, t, w), BF16)
    return pl.pallas_call(
        _ctx_attn_kernel,
        grid=(b,),
        in_specs=[pl.BlockSpec(memory_space=pltpu.SMEM)] + [blk] * 9,
        out_specs=[blk] * 3,
        out_shape=[out] * 3,
        compiler_params=_cparams(("arbitrary",)),
        name="ctx_attn",
    )(sink.reshape(1, N_HEADS).astype(F32), qa, ka, va, qb, kb, vb, qd, kd, vd)


def _global_kernel(q_ref, k_ref, vt_ref, o_ref, *, kt):
    kv = pl.program_id(1)
    qt = q_ref[0]
    tq = qt.shape[1]
    sel = (lax.broadcasted_iota(jnp.int32, qt.shape, 0) // HEAD_DIM) == kv
    zero = jnp.zeros_like(qt)

    def padded(g):
        qg = qt[g * HEAD_DIM:(g + 1) * HEAD_DIM, :]
        return jnp.where(sel, jnp.concatenate([qg, qg], axis=0), zero)

    rhs = jnp.concatenate([padded(0), padded(1)], axis=1)
    n_tiles = k_ref.shape[1] // kt
    scores, tile_max, run_max, probs = {}, {}, {}, {}
    acc = None
    for i in range(n_tiles + 2):
        if i < n_tiles:
            scores[i] = _dot(k_ref[0, i * kt:(i + 1) * kt, :], rhs)
            tile_max[i] = jnp.max(scores[i], axis=0, keepdims=True)
        c = i - 1
        if 0 <= c < n_tiles:
            run_max[c] = tile_max[c] if c == 0 else jnp.maximum(run_max[c - 1], tile_max[c])
            probs[c] = jnp.exp2(scores.pop(c) - run_max[c]).astype(BF16)
        c = i - 2
        if 0 <= c < n_tiles:
            pv = _dot(vt_ref[0, 0, :, c * kt:(c + 1) * kt], probs.pop(c))
            acc = pv if c == 0 else acc * jnp.exp2(run_max[c - 1] - run_max[c]) + pv
    o = acc[0:HEAD_DIM] * (1.0 / acc[HEAD_DIM:HEAD_DIM + 1])
    ot = jnp.concatenate([o[:, :tq], o[:, tq:]], axis=0)
    o_ref[0] = ot.T.astype(BF16)


def _global_attn(qt, k_all, vt_all, *, tq, kt):
    b, w, s = qt.shape
    s_all = k_all.shape[1]
    assert s % tq == 0 and s_all % kt == 0
    return pl.pallas_call(
        functools.partial(_global_kernel, kt=kt),
        grid=(b, N_KV, s // tq),
        in_specs=[pl.BlockSpec((1, 2 * HEAD_DIM, tq), lambda bi, kv, i: (bi, kv, i)),
                  pl.BlockSpec((1, s_all, KV_W), lambda bi, kv, i: (bi, 0, 0)),
                  pl.BlockSpec((1, 1, HEAD_DIM + ONES_ROWS, s_all), lambda bi, kv, i: (bi, kv, 0, 0))],
        out_specs=pl.BlockSpec((1, tq, 2 * HEAD_DIM), lambda bi, kv, i: (bi, i, kv)),
        out_shape=jax.ShapeDtypeStruct((b, s, w), BF16),
        compiler_params=_cparams(("parallel", "arbitrary", "arbitrary")),
        name="global_attn",
    )(qt, k_all, vt_all)


def _hgrn_constants():
    n = HGRN_L
    t_rows = (HGRN_LEVELS + 2) * n + ONES_ROWS
    tm = np.zeros((2, t_rows, n), np.float32)
    mk = np.zeros((2, HGRN_LEVELS + 1, n, n), np.float32)
    for d in range(2):
        pos = np.arange(n) if d == 0 else n - 1 - np.arange(n)
        p, q = pos[:, None], pos[None, :]
        for li in range(HGRN_LEVELS):
            h = n >> (li + 1)
            same_blk = (p // (2 * h)) == (q // (2 * h))
            up_p, up_q = (p % (2 * h)) >= h, (q % (2 * h)) >= h
            same_half = same_blk & (up_p == up_q)
            tm[d, li * n:(li + 1) * n] = np.where(up_p, same_half & (q <= p), same_half & (q > p))
            mk[d, li] = same_blk & up_p & ~up_q
        mk[d, HGRN_LEVELS] = np.eye(n)
        tm[d, HGRN_LEVELS * n:(HGRN_LEVELS + 1) * n] = q <= p
        tm[d, (HGRN_LEVELS + 1) * n:(HGRN_LEVELS + 2) * n] = q > p
        tm[d, (HGRN_LEVELS + 2) * n:] = 1.0
    mk = np.tile(mk, (1, 1, N_HEADS, 1))
    return jnp.asarray(tm, BF16), jnp.asarray(mk, F32)


def _hgrn_kernel(lq_ref, li_ref, lz_ref, cq_ref, ci_ref, cz_ref, lb_ref, t_ref, mk_ref, bd_ref, *rest,
                 n_ctx_chunks, need_ctx):
    if need_ctx:
        ol_ref, oc_ref, st_ref = rest
    else:
        ol_ref, st_ref = rest
    step = pl.program_id(2)
    is_ctx = step < n_ctx_chunks
    n = HGRN_L

    @pl.when(step == 0)
    def _():
        st_ref[...] = jnp.zeros_like(st_ref)

    q = jnp.where(is_ctx, cq_ref[0], lq_ref[0])
    v = jnp.where(is_ctx, ci_ref[0], li_ref[0])
    z = jnp.where(is_ctx, cz_ref[0], lz_ref[0])
    lb = lb_ref[0]
    f = lb + (1.0 - lb) * jax.nn.sigmoid(z)
    kk = 1.0 - f
    g = jnp.log(f)
    qq = _silu(q)

    g_hi = g.astype(BF16)
    r1 = g - g_hi.astype(F32)
    g_mid = r1.astype(BF16)
    g_lo = (r1 - g_mid.astype(F32)).astype(BF16)
    tmat = t_ref[0]
    e = _dot(tmat, g_hi) + _dot(tmat, g_mid) + _dot(tmat, g_lo)

    a = jnp.zeros((N_HEADS * n, n), F32)
    for li in range(HGRN_LEVELS + 1):
        if li < HGRN_LEVELS:
            sc = jnp.exp(e[li * n:(li + 1) * n])
            qs, ks = qq * sc, kk * sc
        else:
            qs, ks = qq, kk
        a = a + _dot_nt(_stack_heads(qs.astype(BF16)), ks.astype(BF16)) * mk_ref[0, li]
    b_incl = e[HGRN_LEVELS * n:(HGRN_LEVELS + 1) * n]
    b_rem = e[(HGRN_LEVELS + 1) * n:(HGRN_LEVELS + 2) * n]
    b_tot = e[(HGRN_LEVELS + 2) * n:(HGRN_LEVELS + 2) * n + 1]
    q_hat = (qq * jnp.exp(b_incl)).astype(BF16)
    k_hat = (kk * jnp.exp(b_rem)).astype(BF16)
    vb = v.astype(BF16)

    o = _unstack_heads(_dot(a.astype(BF16), vb), n)
    st = st_ref[...]
    o = o + _dot_nt(q_hat, st.astype(BF16))
    upd = _dot(v.T.astype(BF16), k_hat)
    st_ref[...] = st * jnp.exp(b_tot) + upd * bd_ref[...]

    @pl.when(jnp.logical_not(is_ctx))
    def _():
        ol_ref[0, 0] = o

    if need_ctx:
        @pl.when(is_ctx)
        def _():
            oc_ref[0, 0] = o


def _hgrn(c_lat, c_ctx, lb, consts, *, need_ctx):
    tmat, masks, bd = consts
    b, s, _ = c_lat.shape
    t = c_ctx.shape[1]
    n = HGRN_L
    assert s % n == 0 and t % n == 0
    n_lat, n_ctx = s // n, t // n

    def lat_chunk(d, st):
        i = jnp.clip(st - n_ctx, 0, n_lat - 1)
        return i + d * (n_lat - 1 - 2 * i)

    def ctx_chunk(d, st):
        i = jnp.clip(st, 0, n_ctx - 1)
        return i + d * (n_ctx - 1 - 2 * i)

    def lat_spec(colf):
        return pl.BlockSpec((1, n, GROUP_W), lambda bi, d, st: (bi, lat_chunk(d, st), colf(d)))

    def ctx_spec(colf):
        return pl.BlockSpec((1, n, GROUP_W), lambda bi, d, st: (bi, ctx_chunk(d, st), colf(d)))

    cols = (lambda d: 0, lambda d: 1, lambda d: 2 + d)
    out_specs = [pl.BlockSpec((1, 1, n, GROUP_W), lambda bi, d, st: (d, bi, lat_chunk(d, st), 0))]
    out_shape = [jax.ShapeDtypeStruct((2, b, s, GROUP_W), F32)]
    if need_ctx:
        out_specs.append(pl.BlockSpec((1, 1, n, GROUP_W), lambda bi, d, st: (d, bi, ctx_chunk(d, st), 0)))
        out_shape.append(jax.ShapeDtypeStruct((2, b, t, GROUP_W), F32))
    outs = pl.pallas_call(
        functools.partial(_hgrn_kernel, n_ctx_chunks=n_ctx, need_ctx=need_ctx),
        grid=(b, 2, n_ctx + n_lat),
        in_specs=[lat_spec(cols[0]), lat_spec(cols[1]), lat_spec(cols[2]),
                  ctx_spec(cols[0]), ctx_spec(cols[1]), ctx_spec(cols[2]),
                  pl.BlockSpec((1, 1, GROUP_W), lambda bi, d, st: (d, 0, 0)),
                  pl.BlockSpec((1,) + tmat.shape[1:], lambda bi, d, st: (d, 0, 0)),
                  pl.BlockSpec((1,) + masks.shape[1:], lambda bi, d, st: (d, 0, 0, 0)),
                  _const_spec(bd.shape)],
        out_specs=out_specs,
        out_shape=out_shape,
        scratch_shapes=[pltpu.VMEM((GROUP_W, GROUP_W), F32)],
        compiler_params=_cparams(("parallel", "arbitrary", "arbitrary")),
        name="hgrn_ctx" if need_ctx else "hgrn",
    )(c_lat, c_lat, c_lat, c_ctx, c_ctx, c_ctx, lb.reshape(2, 1, GROUP_W), tmat, masks, bd)
    return outs if need_ctx else (outs[0], None)


def _outproj_kernel(x_ref, oa_ref, ob_ref, of_ref, obw_ref, gate_ref, od_ref, w_ref, onorm_ref, bd_ref,
                    ga_ref, lng_ref, lnb_ref, o_ref, *, alpha):
    oc = of_ref[0, 0] + obw_ref[0, 0]
    y = oc * lax.rsqrt(_head_mean_sq(oc, bd_ref[...]) + EPS) * onorm_ref[...]
    y = y * _silu(gate_ref[0])
    mix = jnp.concatenate([oa_ref[0], ob_ref[0], y.astype(BF16), od_ref[0]], axis=1)
    z = alpha * x_ref[0] + ga_ref[0] * _dot(mix, w_ref[...])
    o_ref[0] = _layer_norm(z, lng_ref[...], lnb_ref[...])


def _outproj(x, oa, ob, o_dirs, c_raw, od, w_out, onorm, bd256, gate_a, ln_g, ln_b, *, alpha, tm):
    b, s, d = x.shape
    assert s % tm == 0
    grp = pl.BlockSpec((1, tm, GROUP_W), lambda bi, i: (bi, i, 0))
    vec = pl.BlockSpec((1, 1, d), lambda bi, i: (bi, 0, 0))
    return pl.pallas_call(
        functools.partial(_outproj_kernel, alpha=alpha),
        grid=(b, s // tm),
        in_specs=[pl.BlockSpec((1, tm, d), lambda bi, i: (bi, i, 0)), grp, grp,
                  pl.BlockSpec((1, 1, tm, GROUP_W), lambda bi, i: (0, bi, i, 0)),
                  pl.BlockSpec((1, 1, tm, GROUP_W), lambda bi, i: (1, bi, i, 0)),
                  pl.BlockSpec((1, tm, GROUP_W), lambda bi, i: (bi, i, 4)),
                  grp, _const_spec(w_out.shape), _const_spec(onorm.shape), _const_spec(bd256.shape),
                  vec, _const_spec(ln_g.shape), _const_spec(ln_b.shape)],
        out_specs=pl.BlockSpec((1, tm, d), lambda bi, i: (bi, i, 0)),
        out_shape=jax.ShapeDtypeStruct((b, s, d), F32),
        compiler_params=_cparams(("parallel", "arbitrary")),
        name="outproj",
    )(x, oa, ob, o_dirs, o_dirs, c_raw, od, w_out, onorm, bd256, gate_a, ln_g, ln_b)


def _ffn_kernel(h_ref, hp_ref, hn_ref, sc_ref, sh_ref, gt_ref, wup_ref, cw_ref, cb_ref, wdn_ref,
                lng_ref, lnb_ref, o_ref, g_scr, *, alpha, d_ff, cwid, n_tiles):
    i = pl.program_id(1)
    tm = h_ref.shape[1]
    sc = 1.0 + sc_ref[0]
    sh = sh_ref[0]
    h = h_ref[0]
    halo = jnp.concatenate([hp_ref[0], hn_ref[0]], axis=0)
    lhs = jnp.concatenate([(h * sc + sh).astype(BF16), (halo * sc + sh).astype(BF16)], axis=0)
    has_prev = (i > 0).astype(F32)
    has_next = (i < n_tiles - 1).astype(F32)
    row = lax.broadcasted_iota(jnp.int32, (tm, cwid), 0)
    first, last = row == 0, row == tm - 1

    def conv_cols(c0):
        u = _dot(lhs, wup_ref[:, c0:c0 + cwid])
        um = u[:tm]
        u_prev = u[tm + 7:tm + 8] * has_prev
        u_next = u[tm + 8:tm + 9] * has_next
        dn = jnp.where(first, u_prev, pltpu.roll(um, 1, 0))
        up = jnp.where(last, u_next, pltpu.roll(um, tm - 1, 0))
        w = cw_ref[:, c0:c0 + cwid]
        return dn * w[0:1] + um * w[1:2] + up * w[2:3] + cb_ref[:, c0:c0 + cwid]

    for j in range(d_ff // cwid):
        a = conv_cols(j * cwid)
        v = conv_cols(d_ff + j * cwid)
        g_scr[:, j * cwid:(j + 1) * cwid] = (_silu(a) * v).astype(BF16)
    z = alpha * h + gt_ref[0] * _dot(g_scr[...], wdn_ref[...])
    o_ref[0] = _layer_norm(z, lng_ref[...], lnb_ref[...])


def _ffn(h, scale, shift, gate, w_up, conv_w, conv_b, w_down, ln_g, ln_b, *, alpha, tm, cwid):
    b, s, d = h.shape
    d_ff = w_down.shape[0]
    assert s % tm == 0 and tm % 16 == 0 and d_ff % cwid == 0
    n_tiles = s // tm
    r8 = tm // 8
    vec = pl.BlockSpec((1, 1, d), lambda bi, i: (bi, 0, 0))
    return pl.pallas_call(
        functools.partial(_ffn_kernel, alpha=alpha, d_ff=d_ff, cwid=cwid, n_tiles=n_tiles),
        grid=(b, n_tiles),
        in_specs=[pl.BlockSpec((1, tm, d), lambda bi, i: (bi, i, 0)),
                  pl.BlockSpec((1, 8, d), lambda bi, i: (bi, jnp.maximum(i * r8 - 1, 0), 0)),
                  pl.BlockSpec((1, 8, d), lambda bi, i: (bi, jnp.minimum((i + 1) * r8, s // 8 - 1), 0)),
                  vec, vec, vec,
                  _const_spec(w_up.shape), _const_spec(conv_w.shape), _const_spec(conv_b.shape),
                  _const_spec(w_down.shape), _const_spec(ln_g.shape), _const_spec(ln_b.shape)],
        out_specs=pl.BlockSpec((1, tm, d), lambda bi, i: (bi, i, 0)),
        out_shape=jax.ShapeDtypeStruct((b, s, d), F32),
        scratch_shapes=[pltpu.VMEM((tm, d_ff), BF16)],
        compiler_params=_cparams(("parallel", "arbitrary")),
        name="conv_ffn",
    )(h, h, h, scale, shift, gate, w_up, conv_w, conv_b, w_down, ln_g, ln_b)


def _rope_tables(n, with_rotation):
    lane = np.arange(V7X_LANES)
    if not with_rotation:
        one = jnp.ones((n, V7X_LANES), F32)
        zero = jnp.zeros((n, V7X_LANES), F32)
        return one, zero, zero
    t = jnp.arange(n)
    pos = jnp.stack([t // GRID_W, t % GRID_W], -1).astype(F32)
    n_freq = HEAD_DIM // 4
    inv = ROPE_BASE ** (-jnp.arange(n_freq, dtype=F32) / n_freq)
    ang = pos[:, :, None] * inv
    axis = (lane % HEAD_DIM) // (HEAD_DIM // 2)
    freq = lane % n_freq
    ang_l = ang[:, axis, freq]
    first = jnp.asarray((lane % (2 * n_freq)) < n_freq)
    cos, sin = jnp.cos(ang_l), jnp.sin(ang_l)
    return cos, jnp.where(first, -sin, 0.0), jnp.where(first, 0.0, sin)


def _block_diag_mean(width):
    lane = np.arange(width)
    same = (lane[:, None] // HEAD_DIM) == (lane[None, :] // HEAD_DIM)
    return jnp.asarray(same / HEAD_DIM, BF16)


def kernel(x, c, ctx, c_ctx, w_mod, b_mod, w_in, w_out, ln_g, ln_b, attn_sink, qk_norm, hgrn_lb_logits,
           hgrn_onorm, na_rpb, ffn_w_up, ffn_conv_w, ffn_conv_b, ffn_w_down):
    depth = w_mod.shape[0]
    b, s, d = x.shape
    t = ctx.shape[1]
    alpha = (2 * depth) ** 0.25
    s_all = s + t
    kt = 256
    assert s_all % kt == 0

    mods = _modulation(c, c_ctx, w_mod, b_mod)
    rope_lat = _rope_tables(s, True)
    rope_ctx = _rope_tables(t, False)
    bd128, bd256 = _block_diag_mean(KV_W), _block_diag_mean(GROUP_W)
    lane = np.arange(GROUP_W)
    head_bd = jnp.asarray((lane[:, None] // HEAD_DIM) == (lane[None, :] // HEAD_DIM), F32)
    hgrn_consts = _hgrn_constants() + (head_bd,)
    p_lb = jax.nn.softmax(hgrn_lb_logits.astype(F32), axis=1)
    lb_all = jnp.cumsum(p_lb, axis=1) - p_lb[:, :1]
    w_in_b, w_out_b = w_in.astype(BF16), w_out.astype(BF16)
    w_up_b, w_dn_b = ffn_w_up.astype(BF16), ffn_w_down.astype(BF16)

    h_x, h_c = x, ctx
    for l in range(depth):
        need_ctx = l < depth - 1
        mod = mods[l]
        sa, ca, ga, sf, cf, gf = (mod[:, i * d:(i + 1) * d] for i in range(6))
        lat = lambda a: a[:b, None, :]
        con = lambda a: jnp.broadcast_to(a[b][None, None, :], (b, 1, d))
        gq = jnp.tile(qk_norm[l, 0], 2)[None, :]
        gk = jnp.tile(qk_norm[l, 1], 2)[None, :]

        (qa, ka, va, qbt, kb, vbt, c_raw, qd, kd, vd) = _inproj(
            h_x, lat(sa), lat(ca), w_in_b[l], rope_lat, gq, gk, bd128, is_ctx=False, tm=512)
        (qa_c, ka_c, va_c, qb_c, kb_c, vbt_c, c_raw_c, qd_c, kd_c, vd_c, kbx_c, vbx_c) = _inproj(
            h_c, con(sa), con(ca), w_in_b[l], rope_ctx, gq, gk, bd128, is_ctx=True, tm=t)

        o_a = _window_attn(qa, ka, va, ka_c, va_c, attn_sink[l])

        k_all = jnp.concatenate([kb, kb_c], axis=1)
        vt_all = jnp.concatenate([vbt, vbt_c], axis=2).reshape(b, N_KV, HEAD_DIM, s_all)
        vt_all = jnp.concatenate([vt_all, jnp.ones((b, N_KV, ONES_ROWS, s_all), BF16)], axis=2)
        o_b = _global_attn(qbt, k_all, vt_all, tq=256, kt=kt)

        o_dirs, oc_dirs = _hgrn(c_raw, c_raw_c, lb_all[:, l], hgrn_consts, need_ctx=need_ctx)

        o_d = _na_attn(qd, kd, vd, kd_c, vd_c, _na_bias_table(na_rpb[l]))

        onorm = hgrn_onorm[l][None, :]
        lng0, lnb0 = ln_g[l, 0][None, :], ln_b[l, 0][None, :]
        lng1, lnb1 = ln_g[l, 1][None, :], ln_b[l, 1][None, :]
        cw, cb = ffn_conv_w[l], ffn_conv_b[l][None, :]
        h_x = _outproj(h_x, o_a, o_b, o_dirs, c_raw, o_d, w_out_b[l], onorm, bd256, lat(ga), lng0, lnb0,
                       alpha=alpha, tm=512)
        h_x = _ffn(h_x, lat(cf), lat(sf), lat(gf), w_up_b[l], cw, cb, w_dn_b[l], lng1, lnb1,
                   alpha=alpha, tm=256, cwid=256)
        if need_ctx:
            oa_c, ob_c, od_c = _ctx_attn(attn_sink[l], qa_c, ka_c, va_c, qb_c, kbx_c, vbx_c, qd_c, kd_c, vd_c)
            h_c = _outproj(h_c, oa_c, ob_c, oc_dirs, c_raw_c, od_c, w_out_b[l], onorm, bd256, con(ga),
                           lng0, lnb0, alpha=alpha, tm=t)
            h_c = _ffn(h_c, con(cf), con(sf), con(gf), w_up_b[l], cw, cb, w_dn_b[l], lng1, lnb1,
                       alpha=alpha, tm=t, cwid=256)
    return h_x
```

```python
import functools

import numpy as np
import jax
import jax.numpy as jnp
from jax import lax
from jax.experimental import pallas as pl
from jax.experimental.pallas import tpu as pltpu

F32 = jnp.float32
BF16 = jnp.bfloat16

HEAD_DIM = 64
N_HEADS = 4
N_KV = 2
GROUP_W = N_HEADS * HEAD_DIM
KV_W = N_KV * HEAD_DIM
GRID_W = 64
WIN = 128
BLOCK = 128
NA_ROWS = 8
NA_COLS = 16
ROPE_BASE = 10000.0
EPS = 1e-6
HGRN_L = 64
HGRN_LEVELS = 6
HGRN_CHUNKS_PER_STEP = 4
WINDOW_BLOCKS_PER_STEP = 4
NA_ROWS_PER_STEP = 4
ONES_ROWS = 16

V7X_LANES = 128
V7X_VMEM_LIMIT_BYTES = 56 * 1024 * 1024

COL_QA, COL_KA, COL_VA = 0, 256, 384
COL_QB, COL_KB, COL_VB = 512, 768, 896
COL_C = 1024
COL_QD, COL_KD, COL_VD = 2304, 2560, 2816
IN_WIDTH = 3072
C_WIDTH = 5 * GROUP_W

NEG_INF = float("-inf")
LOG2_E = 1.4426950408889634


def _cparams(sem):
    return pltpu.CompilerParams(dimension_semantics=sem, vmem_limit_bytes=V7X_VMEM_LIMIT_BYTES)


def _const_spec(shape):
    nd = len(shape)
    return pl.BlockSpec(shape, lambda *_: (0,) * nd, pipeline_mode=pl.Buffered(1))


def _silu(x):
    return x * jax.nn.sigmoid(x)


def _dot(a, b):
    return jnp.dot(a, b, preferred_element_type=F32)


def _dot_nt(a, b):
    return lax.dot_general(a, b, (((1,), (1,)), ((), ())), preferred_element_type=F32)


def _head_mask(shape, h):
    lane = lax.broadcasted_iota(jnp.int32, shape, len(shape) - 1)
    return (lane // HEAD_DIM) == h


def _stack_heads(q):
    zero = jnp.zeros_like(q)
    return jnp.concatenate([jnp.where(_head_mask(q.shape, h), q, zero) for h in range(N_HEADS)], axis=0)


def _unstack_heads(o, r):
    out = jnp.zeros((r, o.shape[1]), o.dtype)
    for h in range(N_HEADS):
        blk = o[h * r:(h + 1) * r]
        out = jnp.where(_head_mask(blk.shape, h), blk, out)
    return out


def _head_mean_sq(x, bd):
    sq = x * x
    hi = sq.astype(BF16)
    lo = (sq - hi.astype(F32)).astype(BF16)
    return _dot(hi, bd) + _dot(lo, bd)


def _layer_norm(z, g, b):
    mu = jnp.mean(z, axis=-1, keepdims=True)
    zc = z - mu
    var = jnp.mean(zc * zc, axis=-1, keepdims=True)
    return zc * lax.rsqrt(var + EPS) * g + b


def _mod_kernel(s_ref, w_ref, b_ref, o_ref):
    s = _silu(s_ref[...])
    o_ref[0] = jnp.dot(s, w_ref[0], preferred_element_type=F32,
                       precision=lax.Precision.HIGHEST) + b_ref[0]


def _modulation(c, c_ctx, w_mod, b_mod):
    depth, d, d6 = w_mod.shape
    b = c.shape[0]
    rows = 8
    assert b < rows
    s_in = jnp.zeros((rows, d), F32).at[:b].set(c).at[b].set(c_ctx)
    tn = d6 // 4
    return pl.pallas_call(
        _mod_kernel,
        grid=(depth, d6 // tn),
        in_specs=[pl.BlockSpec((rows, d), lambda l, j: (0, 0)),
                  pl.BlockSpec((1, d, tn), lambda l, j: (l, 0, j)),
                  pl.BlockSpec((1, 1, tn), lambda l, j: (l, 0, j))],
        out_specs=pl.BlockSpec((1, rows, tn), lambda l, j: (l, 0, j)),
        out_shape=jax.ShapeDtypeStruct((depth, rows, d6), F32),
        compiler_params=_cparams(("arbitrary", "arbitrary")),
        name="modulation",
    )(s_in, w_mod, b_mod.reshape(depth, 1, d6))


def _dup_heads(x):
    r = pltpu.roll(x, HEAD_DIM, 1)
    lo = lax.broadcasted_iota(jnp.int32, x.shape, 1) < HEAD_DIM
    return jnp.concatenate([jnp.where(lo, x, r), jnp.where(lo, r, x)], axis=1)


def _inproj_kernel(h_ref, sc_ref, sh_ref, w_ref, rc_ref, ra_ref, rb_ref, gq_ref, gk_ref, bd_ref,
                   *outs, is_ctx):
    if is_ctx:
        (qa_ref, ka_ref, va_ref, qb_ref, kb_ref, vbt_ref, c_ref, qd_ref, kd_ref, vd_ref,
         kbx_ref, vbx_ref) = outs
    else:
        qa_ref, ka_ref, va_ref, qbt_ref, kb_ref, vbt_ref, c_ref, qd_ref, kd_ref, vd_ref = outs
    hm = (h_ref[0] * (1.0 + sc_ref[0]) + sh_ref[0]).astype(BF16)
    rc, ra, rb = rc_ref[...], ra_ref[...], rb_ref[...]
    bd = bd_ref[...]
    scale = HEAD_DIM ** -0.5

    def proj(c0, width):
        return _dot(hm, w_ref[:, c0:c0 + width])

    def rope(x):
        return x * rc + pltpu.roll(x, V7X_LANES - 16, 1) * ra + pltpu.roll(x, 16, 1) * rb

    def rms(x, gain):
        return x * lax.rsqrt(_head_mean_sq(x, bd) + EPS) * gain

    xq = proj(COL_QA, GROUP_W)
    for j in range(2):
        sl = slice(j * V7X_LANES, (j + 1) * V7X_LANES)
        qa_ref[0, :, sl] = (rope(xq[:, sl]) * scale).astype(BF16)
    ka_ref[0] = _dup_heads(rope(proj(COL_KA, KV_W))).astype(BF16)
    va_ref[0] = _dup_heads(proj(COL_VA, KV_W)).astype(BF16)

    xq = proj(COL_QB, GROUP_W)
    gq, gk = gq_ref[...], gk_ref[...]
    scale_b = scale if is_ctx else scale * LOG2_E
    for j in range(2):
        sl = slice(j * V7X_LANES, (j + 1) * V7X_LANES)
        y = rope(rms(xq[:, sl], gq)) * scale_b
        if is_ctx:
            qb_ref[0, :, sl] = y.astype(BF16)
        else:
            qbt_ref[0, sl, :] = y.T.astype(BF16)
    kb = rope(rms(proj(COL_KB, KV_W), gk))
    kb_ref[0] = kb.astype(BF16)
    vb = proj(COL_VB, KV_W)
    vbt_ref[0] = vb.T.astype(BF16)
    if is_ctx:
        kbx_ref[0] = _dup_heads(kb).astype(BF16)
        vbx_ref[0] = _dup_heads(vb).astype(BF16)

    for j in range(C_WIDTH // GROUP_W):
        c_ref[0, :, j * GROUP_W:(j + 1) * GROUP_W] = proj(COL_C + j * GROUP_W, GROUP_W)

    qd_ref[0] = (proj(COL_QD, GROUP_W) * scale).astype(BF16)
    kd_ref[0] = proj(COL_KD, GROUP_W).astype(BF16)
    vd_ref[0] = proj(COL_VD, GROUP_W).astype(BF16)


def _inproj(h, shift, scale, w_in, rope_tabs, gq, gk, bd128, *, is_ctx, tm):
    b, s, d = h.shape
    assert s % tm == 0
    rc, ra, rb = rope_tabs
    row = lambda width: pl.BlockSpec((1, tm, width), lambda bi, i: (bi, i, 0))
    col = lambda height: pl.BlockSpec((1, height, tm), lambda bi, i: (bi, 0, i))
    sds = lambda shape, dt=BF16: jax.ShapeDtypeStruct(shape, dt)
    if is_ctx:
        out_specs = [row(GROUP_W), row(GROUP_W), row(GROUP_W), row(GROUP_W), row(KV_W), col(KV_W),
                     row(C_WIDTH), row(GROUP_W), row(GROUP_W), row(GROUP_W), row(GROUP_W), row(GROUP_W)]
        out_shape = [sds((b, s, GROUP_W)), sds((b, s, GROUP_W)), sds((b, s, GROUP_W)), sds((b, s, GROUP_W)),
                     sds((b, s, KV_W)), sds((b, KV_W, s)), sds((b, s, C_WIDTH), F32),
                     sds((b, s, GROUP_W)), sds((b, s, GROUP_W)), sds((b, s, GROUP_W)),
                     sds((b, s, GROUP_W)), sds((b, s, GROUP_W))]
    else:
        out_specs = [row(GROUP_W), row(GROUP_W), row(GROUP_W), col(GROUP_W), row(KV_W), col(KV_W),
                     row(C_WIDTH), row(GROUP_W), row(GROUP_W), row(GROUP_W)]
        out_shape = [sds((b, s, GROUP_W)), sds((b, s, GROUP_W)), sds((b, s, GROUP_W)), sds((b, GROUP_W, s)),
                     sds((b, s, KV_W)), sds((b, KV_W, s)), sds((b, s, C_WIDTH), F32),
                     sds((b, s, GROUP_W)), sds((b, s, GROUP_W)), sds((b, s, GROUP_W))]
    tab = pl.BlockSpec((tm, V7X_LANES), lambda bi, i: (i, 0))
    vec = pl.BlockSpec((1, 1, d), lambda bi, i: (bi, 0, 0))
    return pl.pallas_call(
        functools.partial(_inproj_kernel, is_ctx=is_ctx),
        grid=(b, s // tm),
        in_specs=[pl.BlockSpec((1, tm, d), lambda bi, i: (bi, i, 0)), vec, vec,
                  _const_spec(w_in.shape), tab, tab, tab,
                  _const_spec(gq.shape), _const_spec(gk.shape), _const_spec(bd128.shape)],
        out_specs=out_specs,
        out_shape=out_shape,
        compiler_params=_cparams(("parallel", "arbitrary")),
        name="inproj_ctx" if is_ctx else "inproj",
    )(h, scale, shift, w_in, rc, ra, rb, gq, gk, bd128)


def _attend_many(items):
    scores, maxes = [], []
    for qs, parts, sink_col in items:
        ss, m = [], sink_col
        for k, _, bias in parts:
            s = _dot_nt(qs, k)
            if bias is not None:
                s = s + bias
            ss.append(s)
            smax = jnp.max(s, axis=1, keepdims=True)
            m = smax if m is None else jnp.maximum(m, smax)
        scores.append(ss)
        maxes.append(m)
    probs, dens = [], []
    for (qs, parts, sink_col), ss, m in zip(items, scores, maxes):
        den = None if sink_col is None else jnp.exp(sink_col - m)
        ps = []
        for s in ss:
            p = jnp.exp(s - m)
            psum = jnp.sum(p, axis=1, keepdims=True)
            den = psum if den is None else den + psum
            ps.append(p.astype(BF16))
        probs.append(ps)
        dens.append(den)
    outs = []
    for (qs, parts, _), ps, den in zip(items, probs, dens):
        out = None
        for p, (_, v, _) in zip(ps, parts):
            pv = _dot(p, v)
            out = pv if out is None else out + pv
        outs.append(out * (1.0 / den))
    return outs


def _window_kernel(sink_ref, q_ref, k_ref, v_ref, kc_ref, vc_ref, o_ref, *, seq, n_blk):
    step = pl.program_id(1)
    span = 3 * BLOCK
    sink_col = jnp.concatenate([jnp.full((BLOCK, 1), sink_ref[0, h], F32) for h in range(N_HEADS)], axis=0)
    ctx_part = (kc_ref[0], vc_ref[0], None)
    items = []
    for j in range(n_blk):
        n = step * n_blk + j
        start = pl.multiple_of(jnp.clip((n - 1) * BLOCK, 0, seq - span), BLOCK)
        kw = k_ref[0, pl.ds(start, span), :]
        vw = v_ref[0, pl.ds(start, span), :]
        qs = _stack_heads(q_ref[0, j * BLOCK:(j + 1) * BLOCK, :])
        qpos = n * BLOCK + lax.broadcasted_iota(jnp.int32, (BLOCK, span), 0)
        kpos = start + lax.broadcasted_iota(jnp.int32, (BLOCK, span), 1)
        bias = jnp.where(jnp.abs(qpos - kpos) <= WIN, 0.0, NEG_INF).astype(F32)
        items.append((qs, [(kw, vw, jnp.concatenate([bias] * N_HEADS, axis=0)), ctx_part], sink_col))
    for j, o in enumerate(_attend_many(items)):
        o_ref[0, j * BLOCK:(j + 1) * BLOCK, :] = _unstack_heads(o, BLOCK).astype(BF16)


def _window_attn(q, k, v, kc, vc, sink, *, n_blk):
    b, s, w = q.shape
    t = kc.shape[1]
    rows = n_blk * BLOCK
    assert s % rows == 0 and s >= 3 * BLOCK
    full = lambda n: pl.BlockSpec((1, n, w), lambda bi, i: (bi, 0, 0))
    return pl.pallas_call(
        functools.partial(_window_kernel, seq=s, n_blk=n_blk),
        grid=(b, s // rows),
        in_specs=[pl.BlockSpec(memory_space=pltpu.SMEM),
                  pl.BlockSpec((1, rows, w), lambda bi, i: (bi, i, 0)),
                  full(s), full(s), full(t), full(t)],
        out_specs=pl.BlockSpec((1, rows, w), lambda bi, i: (bi, i, 0)),
        out_shape=jax.ShapeDtypeStruct((b, s, w), BF16),
        compiler_params=_cparams(("parallel", "arbitrary")),
        name="window_attn",
    )(sink.reshape(1, N_HEADS).astype(F32), q, k, v, kc, vc)


def _na_kernel(q_ref, k_ref, v_ref, kc_ref, vc_ref, bias_ref, o_ref, *, rows, n_rows):
    step = pl.program_id(1)
    span = NA_ROWS * GRID_W
    ctx_part = (kc_ref[0], vc_ref[0], None)
    items = []
    for j in range(n_rows):
        r = step * n_rows + j
        r0 = jnp.clip(r - NA_ROWS // 2, 0, rows - NA_ROWS)
        start = pl.multiple_of(r0 * GRID_W, GRID_W)
        kw = k_ref[0, pl.ds(start, span), :]
        vw = v_ref[0, pl.ds(start, span), :]
        qs = _stack_heads(q_ref[0, j * GRID_W:(j + 1) * GRID_W, :])
        bias = bias_ref[r0 - r + (NA_ROWS - 1)]
        items.append((qs, [(kw, vw, bias), ctx_part], None))
    for j, o in enumerate(_attend_many(items)):
        o_ref[0, j * GRID_W:(j + 1) * GRID_W, :] = _unstack_heads(o, GRID_W).astype(BF16)


def _na_bias_table(rpb):
    h = rpb.shape[0]
    q = np.arange(GRID_W)
    kc = np.arange(GRID_W)
    coff = np.clip(kc[None, :] - q[:, None], -(NA_COLS - 1), NA_COLS - 1) + (NA_COLS - 1)
    c0 = np.clip(q - NA_COLS // 2, 0, GRID_W - NA_COLS)
    ok = (kc[None, :] >= c0[:, None]) & (kc[None, :] < c0[:, None] + NA_COLS)
    tab = jnp.where(ok[None, None], rpb.astype(F32)[:, :, coff], NEG_INF)
    cls = np.arange(NA_ROWS)[:, None] + np.arange(NA_ROWS)[None, :]
    tab = tab[:, cls]
    return tab.transpose(1, 0, 3, 2, 4).reshape(NA_ROWS, h * GRID_W, NA_ROWS * GRID_W)


def _na_attn(q, k, v, kc, vc, bias_tab, *, n_rows):
    b, s, w = q.shape
    t = kc.shape[1]
    rows = s // GRID_W
    assert s % GRID_W == 0 and rows >= NA_ROWS and rows % n_rows == 0
    full = lambda n: pl.BlockSpec((1, n, w), lambda bi, i: (bi, 0, 0))
    blk = n_rows * GRID_W
    return pl.pallas_call(
        functools.partial(_na_kernel, rows=rows, n_rows=n_rows),
        grid=(b, rows // n_rows),
        in_specs=[pl.BlockSpec((1, blk, w), lambda bi, r: (bi, r, 0)),
                  full(s), full(s), full(t), full(t), _const_spec(bias_tab.shape)],
        out_specs=pl.BlockSpec((1, blk, w), lambda bi, r: (bi, r, 0)),
        out_shape=jax.ShapeDtypeStruct((b, s, w), BF16),
        compiler_params=_cparams(("parallel", "arbitrary")),
        name="na_attn",
    )(q, k, v, kc, vc, bias_tab)


def _ctx_attn_kernel(sink_ref, qa_ref, ka_ref, va_ref, qb_ref, kb_ref, vb_ref, qd_ref, kd_ref, vd_ref,
                     oa_ref, ob_ref, od_ref):
    t = qa_ref.shape[1]
    sink_col = jnp.concatenate([jnp.full((t, 1), sink_ref[0, h], F32) for h in range(N_HEADS)], axis=0)
    groups = ((qa_ref, ka_ref, va_ref, sink_col), (qb_ref, kb_ref, vb_ref, None), (qd_ref, kd_ref, vd_ref, None))
    outs = _attend_many([(_stack_heads(q_ref[0]), [(k_ref[0], v_ref[0], None)], sink)
                         for q_ref, k_ref, v_ref, sink in groups])
    for o_ref, o in zip((oa_ref, ob_ref, od_ref), outs):
        o_ref[0] = _unstack_heads(o, t).astype(BF16)


def _ctx_attn(sink, qa, ka, va, qb, kb, vb, qd, kd, vd):
    b, t, w = qa.shape
    blk = pl.BlockSpec((1, t, w), lambda bi: (bi, 0, 0))
    out = jax.ShapeDtypeStruct((b, t, w), BF16)
    return pl.pallas_call(
        _ctx_attn_kernel,
        grid=(b,),
        in_specs=[pl.BlockSpec(memory_space=pltpu.SMEM)] + [blk] * 9,
        out_specs=[blk] * 3,
        out_shape=[out] * 3,
        compiler_params=_cparams(("arbitrary",)),
        name="ctx_attn",
    )(sink.reshape(1, N_HEADS).astype(F32), qa, ka, va, qb, kb, vb, qd, kd, vd)


def _global_kernel(q_ref, k_ref, vt_ref, kc_ref, vtc_ref, o_ref, *, kt):
    kv = pl.program_id(1)
    qt = q_ref[0]
    tq = qt.shape[1]
    sel = (lax.broadcasted_iota(jnp.int32, qt.shape, 0) // HEAD_DIM) == kv
    zero = jnp.zeros_like(qt)

    def padded(g):
        qg = qt[g * HEAD_DIM:(g + 1) * HEAD_DIM, :]
        return jnp.where(sel, jnp.concatenate([qg, qg], axis=0), zero)

    rhs = jnp.concatenate([padded(0), padded(1)], axis=1)
    n_lat = k_ref.shape[1] // kt
    n_tiles = n_lat + kc_ref.shape[1] // kt
    ones = jnp.ones((ONES_ROWS, kt), BF16)

    def k_tile(i):
        return k_ref[0, i * kt:(i + 1) * kt, :] if i < n_lat else kc_ref[0, (i - n_lat) * kt:(i - n_lat + 1) * kt, :]

    def vt_tile(i):
        vt = (vt_ref[0, :, i * kt:(i + 1) * kt] if i < n_lat
              else vtc_ref[0, :, (i - n_lat) * kt:(i - n_lat + 1) * kt])
        return jnp.concatenate([vt, ones], axis=0)

    scores, tile_max, run_max, probs = {}, {}, {}, {}
    acc = None
    for i in range(n_tiles + 2):
        if i < n_tiles:
            scores[i] = _dot(k_tile(i), rhs)
            tile_max[i] = jnp.max(scores[i], axis=0, keepdims=True)
        c = i - 1
        if 0 <= c < n_tiles:
            run_max[c] = tile_max[c] if c == 0 else jnp.maximum(run_max[c - 1], tile_max[c])
            probs[c] = jnp.exp2(scores.pop(c) - run_max[c]).astype(BF16)
        c = i - 2
        if 0 <= c < n_tiles:
            pv = _dot(vt_tile(c), probs.pop(c))
            acc = pv if c == 0 else acc * jnp.exp2(run_max[c - 1] - run_max[c]) + pv
    o = acc[0:HEAD_DIM] * (1.0 / acc[HEAD_DIM:HEAD_DIM + 1])
    ot = jnp.concatenate([o[:, :tq], o[:, tq:]], axis=0)
    o_ref[0] = ot.T.astype(BF16)


def _global_attn(qt, k, vt, kc, vtc, *, tq, kt):
    b, w, s = qt.shape
    t = kc.shape[1]
    assert s % tq == 0 and s % kt == 0 and t % kt == 0
    return pl.pallas_call(
        functools.partial(_global_kernel, kt=kt),
        grid=(b, N_KV, s // tq),
        in_specs=[pl.BlockSpec((1, 2 * HEAD_DIM, tq), lambda bi, kv, i: (bi, kv, i)),
                  pl.BlockSpec((1, s, KV_W), lambda bi, kv, i: (bi, 0, 0)),
                  pl.BlockSpec((1, HEAD_DIM, s), lambda bi, kv, i: (bi, kv, 0)),
                  pl.BlockSpec((1, t, KV_W), lambda bi, kv, i: (bi, 0, 0)),
                  pl.BlockSpec((1, HEAD_DIM, t), lambda bi, kv, i: (bi, kv, 0))],
        out_specs=pl.BlockSpec((1, tq, 2 * HEAD_DIM), lambda bi, kv, i: (bi, i, kv)),
        out_shape=jax.ShapeDtypeStruct((b, s, w), BF16),
        compiler_params=_cparams(("parallel", "arbitrary", "arbitrary")),
        name="global_attn",
    )(qt, k, vt, kc, vtc)


def _hgrn_constants():
    n = HGRN_L
    tm = np.zeros((2, 2 * n, n), np.float32)
    mk = np.zeros((2, HGRN_LEVELS + 1, n, n), np.float32)
    for d in range(2):
        pos = np.arange(n) if d == 0 else n - 1 - np.arange(n)
        p, q = pos[:, None], pos[None, :]
        for li in range(HGRN_LEVELS):
            h = n >> (li + 1)
            same_blk = (p // (2 * h)) == (q // (2 * h))
            up_p, up_q = (p % (2 * h)) >= h, (q % (2 * h)) >= h
            mk[d, li] = same_blk & up_p & ~up_q
            if h == 2:
                same_half = same_blk & (up_p == up_q)
                tm[d, 0:n] = np.where(up_p, same_half & (q <= p), same_half & (q > p))
        mk[d, HGRN_LEVELS] = np.eye(n)
        tm[d, n:2 * n] = q <= p
    mk = np.tile(mk, (1, 1, N_HEADS, 1))
    return jnp.asarray(tm, BF16), jnp.asarray(mk, F32)


def _hgrn_seq_kernel(qf_ref, vf_ref, zf_ref, qb_ref, vb_ref, zb_ref, lb_ref, t_ref, mk_ref, bd_ref, *rest,
                     g_chunks, has_init, emit_state):
    rest = list(rest)
    init_ref = rest.pop(0) if has_init else None
    of_ref, ob_ref = rest.pop(0), rest.pop(0)
    so_ref = rest.pop(0) if emit_state else None
    st_ref = rest.pop(0)
    step = pl.program_id(1)
    n = HGRN_L

    @pl.when(step == 0)
    def _():
        st_ref[...] = init_ref[0] if has_init else jnp.zeros_like(st_ref)

    ins = ((qf_ref, vf_ref, zf_ref), (qb_ref, vb_ref, zb_ref))
    outs = (of_ref, ob_ref)
    items = [(d, (g if d == 0 else g_chunks - 1 - g) * n) for g in range(g_chunks) for d in (0, 1)]

    work = []
    for d, r0 in items:
        q = ins[d][0][0, r0:r0 + n, :]
        v = ins[d][1][0, r0:r0 + n, :]
        z = ins[d][2][0, r0:r0 + n, :]
        lb = lb_ref[d]
        f = lb + (1.0 - lb) * jax.nn.sigmoid(z)
        g = jnp.log(f)
        g_hi = g.astype(BF16)
        g_lo = (g - g_hi.astype(F32)).astype(BF16)
        e = _dot(t_ref[d], g_hi) + _dot(t_ref[d], g_lo)
        work.append(dict(qq=_silu(q), kk=1.0 - f, f=f, v=v, e2=e[0:n], b=e[n:2 * n],
                         a=jnp.zeros((N_HEADS * n, n), F32)))

    row = lax.broadcasted_iota(jnp.int32, (n, GROUP_W), 0)

    def level_scale(w, d, h):
        if h == 2:
            return jnp.exp(w["e2"])
        b = w["b"]
        blocks = []
        for j in range(n // (2 * h)):
            m = 2 * h * j + (h - 1 if d == 0 else h)
            blocks.append(jnp.broadcast_to(b[m:m + 1, :], (2 * h, GROUP_W)))
        bm = blocks[0] if len(blocks) == 1 else jnp.concatenate(blocks, axis=0)
        later = ((row // h) % 2) == (1 if d == 0 else 0)
        return jnp.exp(jnp.where(later, b - bm, bm - b))

    for li in range(HGRN_LEVELS + 1):
        h = n >> (li + 1)
        for (d, _), w in zip(items, work):
            if h >= 2:
                sc = level_scale(w, d, h)
                qs, ks = w["qq"] * sc, w["kk"] * sc
            elif h == 1:
                qs, ks = w["qq"] * w["f"], w["kk"]
            else:
                qs, ks = w["qq"], w["kk"]
            w["a"] = w["a"] + _dot_nt(_stack_heads(qs.astype(BF16)), ks.astype(BF16)) * mk_ref[d, li]

    for (d, _), w in zip(items, work):
        b_incl = w["b"]
        b_tot = b_incl[n - 1:n] if d == 0 else b_incl[0:1]
        vb = w["v"].astype(BF16)
        w["q_hat"] = (w["qq"] * jnp.exp(b_incl)).astype(BF16)
        k_hat = (w["kk"] * jnp.exp(b_tot - b_incl)).astype(BF16)
        w["decay"] = jnp.exp(b_tot)
        w["o"] = _unstack_heads(_dot(w["a"].astype(BF16), vb), n)
        w["upd"] = _dot(w["v"].T.astype(BF16), k_hat) * bd_ref[...]

    st = [st_ref[0], st_ref[1]]
    for (d, r0), w in zip(items, work):
        outs[d][0, r0:r0 + n, :] = w["o"] + _dot_nt(w["q_hat"], st[d].astype(BF16))
        st[d] = st[d] * w["decay"] + w["upd"]
    st_ref[0] = st[0]
    st_ref[1] = st[1]
    if emit_state:
        so_ref[0, 0] = st[0]
        so_ref[0, 1] = st[1]


def _hgrn_seq(c_raw, lb, consts, init_state, *, g_chunks, emit_state):
    tmat, masks, bd = consts
    b, s, _ = c_raw.shape
    rows = g_chunks * HGRN_L
    assert s % rows == 0
    n_steps = s // rows
    has_init = init_state is not None
    fwd = lambda col: pl.BlockSpec((1, rows, GROUP_W), lambda bi, st: (bi, st, col))
    bwd = lambda col: pl.BlockSpec((1, rows, GROUP_W), lambda bi, st: (bi, n_steps - 1 - st, col))
    state_spec = pl.BlockSpec((1, 2, GROUP_W, GROUP_W), lambda bi, st: (bi, 0, 0, 0))
    in_specs = [fwd(0), fwd(1), fwd(2), bwd(0), bwd(1), bwd(3),
                _const_spec((2, 1, GROUP_W)), _const_spec(tmat.shape), _const_spec(masks.shape),
                _const_spec(bd.shape)]
    args = [c_raw] * 6 + [lb.reshape(2, 1, GROUP_W), tmat, masks, bd]
    if has_init:
        in_specs.append(state_spec)
        args.append(init_state)
    out_specs = [fwd(0), bwd(0)]
    out_shape = [jax.ShapeDtypeStruct((b, s, GROUP_W), F32)] * 2
    if emit_state:
        out_specs.append(state_spec)
        out_shape.append(jax.ShapeDtypeStruct((b, 2, GROUP_W, GROUP_W), F32))
    return pl.pallas_call(
        functools.partial(_hgrn_seq_kernel, g_chunks=g_chunks, has_init=has_init, emit_state=emit_state),
        grid=(b, n_steps),
        in_specs=in_specs,
        out_specs=out_specs,
        out_shape=out_shape,
        scratch_shapes=[pltpu.VMEM((2, GROUP_W, GROUP_W), F32)],
        compiler_params=_cparams(("parallel", "arbitrary")),
        name="hgrn_ctx" if emit_state else "hgrn",
    )(*args)


def _outproj_kernel(x_ref, oa_ref, ob_ref, of_ref, obw_ref, gate_ref, od_ref, w_ref, onorm_ref, bd_ref,
                    ga_ref, lng_ref, lnb_ref, o_ref, *, alpha):
    oc = of_ref[0] + obw_ref[0]
    y = oc * lax.rsqrt(_head_mean_sq(oc, bd_ref[...]) + EPS) * onorm_ref[...]
    y = y * _silu(gate_ref[0])
    mix = jnp.concatenate([oa_ref[0], ob_ref[0], y.astype(BF16), od_ref[0]], axis=1)
    z = alpha * x_ref[0] + ga_ref[0] * _dot(mix, w_ref[...])
    o_ref[0] = _layer_norm(z, lng_ref[...], lnb_ref[...])


def _outproj(x, oa, ob, oc_f, oc_b, c_raw, od, w_out, onorm, bd256, gate_a, ln_g, ln_b, *, alpha, tm):
    b, s, d = x.shape
    assert s % tm == 0
    grp = pl.BlockSpec((1, tm, GROUP_W), lambda bi, i: (bi, i, 0))
    vec = pl.BlockSpec((1, 1, d), lambda bi, i: (bi, 0, 0))
    return pl.pallas_call(
        functools.partial(_outproj_kernel, alpha=alpha),
        grid=(b, s // tm),
        in_specs=[pl.BlockSpec((1, tm, d), lambda bi, i: (bi, i, 0)), grp, grp, grp, grp,
                  pl.BlockSpec((1, tm, GROUP_W), lambda bi, i: (bi, i, 4)),
                  grp, _const_spec(w_out.shape), _const_spec(onorm.shape), _const_spec(bd256.shape),
                  vec, _const_spec(ln_g.shape), _const_spec(ln_b.shape)],
        out_specs=pl.BlockSpec((1, tm, d), lambda bi, i: (bi, i, 0)),
        out_shape=jax.ShapeDtypeStruct((b, s, d), F32),
        compiler_params=_cparams(("parallel", "arbitrary")),
        name="outproj",
    )(x, oa, ob, oc_f, oc_b, c_raw, od, w_out, onorm, bd256, gate_a, ln_g, ln_b)


def _ffn_kernel(h_ref, hp_ref, hn_ref, sc_ref, sh_ref, gt_ref, wup_ref, cw_ref, cb_ref, wdn_ref,
                lng_ref, lnb_ref, o_ref, g_scr, *, alpha, d_ff, cwid, n_tiles):
    i = pl.program_id(1)
    tm = h_ref.shape[1]
    sc = 1.0 + sc_ref[0]
    sh = sh_ref[0]
    h = h_ref[0]
    halo = jnp.concatenate([hp_ref[0], hn_ref[0]], axis=0)
    lhs = jnp.concatenate([(h * sc + sh).astype(BF16), (halo * sc + sh).astype(BF16)], axis=0)
    has_prev = (i > 0).astype(F32)
    has_next = (i < n_tiles - 1).astype(F32)
    row = lax.broadcasted_iota(jnp.int32, (tm, cwid), 0)
    first, last = row == 0, row == tm - 1

    def conv_cols(c0):
        u = _dot(lhs, wup_ref[:, c0:c0 + cwid])
        um = u[:tm]
        u_prev = u[tm + 7:tm + 8] * has_prev
        u_next = u[tm + 8:tm + 9] * has_next
        dn = jnp.where(first, u_prev, pltpu.roll(um, 1, 0))
        up = jnp.where(last, u_next, pltpu.roll(um, tm - 1, 0))
        w = cw_ref[:, c0:c0 + cwid]
        return dn * w[0:1] + um * w[1:2] + up * w[2:3] + cb_ref[:, c0:c0 + cwid]

    for j in range(d_ff // cwid):
        a = conv_cols(j * cwid)
        v = conv_cols(d_ff + j * cwid)
        g_scr[:, j * cwid:(j + 1) * cwid] = (_silu(a) * v).astype(BF16)
    z = alpha * h + gt_ref[0] * _dot(g_scr[...], wdn_ref[...])
    o_ref[0] = _layer_norm(z, lng_ref[...], lnb_ref[...])


def _ffn(h, scale, shift, gate, w_up, conv_w, conv_b, w_down, ln_g, ln_b, *, alpha, tm, cwid):
    b, s, d = h.shape
    d_ff = w_down.shape[0]
    assert s % tm == 0 and tm % 16 == 0 and d_ff % cwid == 0
    n_tiles = s // tm
    r8 = tm // 8
    vec = pl.BlockSpec((1, 1, d), lambda bi, i: (bi, 0, 0))
    return pl.pallas_call(
        functools.partial(_ffn_kernel, alpha=alpha, d_ff=d_ff, cwid=cwid, n_tiles=n_tiles),
        grid=(b, n_tiles),
        in_specs=[pl.BlockSpec((1, tm, d), lambda bi, i: (bi, i, 0)),
                  pl.BlockSpec((1, 8, d), lambda bi, i: (bi, jnp.maximum(i * r8 - 1, 0), 0)),
                  pl.BlockSpec((1, 8, d), lambda bi, i: (bi, jnp.minimum((i + 1) * r8, s // 8 - 1), 0)),
                  vec, vec, vec,
                  _const_spec(w_up.shape), _const_spec(conv_w.shape), _const_spec(conv_b.shape),
                  _const_spec(w_down.shape), _const_spec(ln_g.shape), _const_spec(ln_b.shape)],
        out_specs=pl.BlockSpec((1, tm, d), lambda bi, i: (bi, i, 0)),
        out_shape=jax.ShapeDtypeStruct((b, s, d), F32),
        scratch_shapes=[pltpu.VMEM((tm, d_ff), BF16)],
        compiler_params=_cparams(("parallel", "arbitrary")),
        name="conv_ffn",
    )(h, h, h, scale, shift, gate, w_up, conv_w, conv_b, w_down, ln_g, ln_b)


def _rope_tables(n, with_rotation):
    lane = np.arange(V7X_LANES)
    if not with_rotation:
        one = jnp.ones((n, V7X_LANES), F32)
        zero = jnp.zeros((n, V7X_LANES), F32)
        return one, zero, zero
    t = np.arange(n)
    pos = np.stack([t // GRID_W, t % GRID_W], -1).astype(np.float64)
    n_freq = HEAD_DIM // 4
    inv = ROPE_BASE ** (-np.arange(n_freq, dtype=np.float64) / n_freq)
    ang = pos[:, :, None] * inv
    axis = (lane % HEAD_DIM) // (HEAD_DIM // 2)
    freq = lane % n_freq
    ang_l = ang[:, axis, freq]
    first = (lane % (2 * n_freq)) < n_freq
    cos, sin = np.cos(ang_l), np.sin(ang_l)
    tabs = (cos, np.where(first, -sin, 0.0), np.where(first, 0.0, sin))
    return tuple(jnp.asarray(a, F32) for a in tabs)


def _block_diag_mean(width):
    lane = np.arange(width)
    same = (lane[:, None] // HEAD_DIM) == (lane[None, :] // HEAD_DIM)
    return jnp.asarray(same / HEAD_DIM, BF16)


def kernel(x, c, ctx, c_ctx, w_mod, b_mod, w_in, w_out, ln_g, ln_b, attn_sink, qk_norm, hgrn_lb_logits,
           hgrn_onorm, na_rpb, ffn_w_up, ffn_conv_w, ffn_conv_b, ffn_w_down):
    depth = w_mod.shape[0]
    b, s, d = x.shape
    t = ctx.shape[1]
    alpha = (2 * depth) ** 0.25

    mods = _modulation(c, c_ctx, w_mod, b_mod)
    rope_lat = _rope_tables(s, True)
    rope_ctx = _rope_tables(t, False)
    bd128, bd256 = _block_diag_mean(KV_W), _block_diag_mean(GROUP_W)
    lane = np.arange(GROUP_W)
    head_bd = jnp.asarray((lane[:, None] // HEAD_DIM) == (lane[None, :] // HEAD_DIM), F32)
    hgrn_consts = _hgrn_constants() + (head_bd,)
    p_lb = jax.nn.softmax(hgrn_lb_logits.astype(F32), axis=1)
    lb_all = jnp.cumsum(p_lb, axis=1) - p_lb[:, :1]
    w_in_b, w_out_b = w_in.astype(BF16), w_out.astype(BF16)
    w_up_b, w_dn_b = ffn_w_up.astype(BF16), ffn_w_down.astype(BF16)

    h_x, h_c = x, ctx
    for l in range(depth):
        need_ctx = l < depth - 1
        mod = mods[l]
        sa, ca, ga, sf, cf, gf = (mod[:, i * d:(i + 1) * d] for i in range(6))
        lat = lambda a: a[:b, None, :]
        con = lambda a: jnp.broadcast_to(a[b][None, None, :], (b, 1, d))
        gq = jnp.tile(qk_norm[l, 0], 2)[None, :]
        gk = jnp.tile(qk_norm[l, 1], 2)[None, :]

        (qa, ka, va, qbt, kb, vbt, c_raw, qd, kd, vd) = _inproj(
            h_x, lat(sa), lat(ca), w_in_b[l], rope_lat, gq, gk, bd128, is_ctx=False, tm=512)
        (qa_c, ka_c, va_c, qb_c, kb_c, vbt_c, c_raw_c, qd_c, kd_c, vd_c, kbx_c, vbx_c) = _inproj(
            h_c, con(sa), con(ca), w_in_b[l], rope_ctx, gq, gk, bd128, is_ctx=True, tm=t)

        o_a = _window_attn(qa, ka, va, ka_c, va_c, attn_sink[l], n_blk=WINDOW_BLOCKS_PER_STEP)

        o_b = _global_attn(qbt, kb, vbt, kb_c, vbt_c, tq=256, kt=256)

        ocf_c, ocb_c, ctx_state = _hgrn_seq(c_raw_c, lb_all[:, l], hgrn_consts, None,
                                            g_chunks=t // HGRN_L, emit_state=True)
        oc_f, oc_b = _hgrn_seq(c_raw, lb_all[:, l], hgrn_consts, ctx_state,
                               g_chunks=HGRN_CHUNKS_PER_STEP, emit_state=False)

        o_d = _na_attn(qd, kd, vd, kd_c, vd_c, _na_bias_table(na_rpb[l]), n_rows=NA_ROWS_PER_STEP)

        onorm = hgrn_onorm[l][None, :]
        lng0, lnb0 = ln_g[l, 0][None, :], ln_b[l, 0][None, :]
        lng1, lnb1 = ln_g[l, 1][None, :], ln_b[l, 1][None, :]
        cw, cb = ffn_conv_w[l], ffn_conv_b[l][None, :]
        h_x = _outproj(h_x, o_a, o_b, oc_f, oc_b, c_raw, o_d, w_out_b[l], onorm, bd256, lat(ga), lng0, lnb0,
                       alpha=alpha, tm=512)
        h_x = _ffn(h_x, lat(cf), lat(sf), lat(gf), w_up_b[l], cw, cb, w_dn_b[l], lng1, lnb1,
                   alpha=alpha, tm=256, cwid=256)
        if need_ctx:
            oa_c, ob_c, od_c = _ctx_attn(attn_sink[l], qa_c, ka_c, va_c, qb_c, kbx_c, vbx_c, qd_c, kd_c, vd_c)
            h_c = _outproj(h_c, oa_c, ob_c, ocf_c, ocb_c, c_raw_c, od_c, w_out_b[l], onorm, bd256, con(ga),
                           lng0, lnb0, alpha=alpha, tm=t)
            h_c = _ffn(h_c, con(cf), con(sf), con(gf), w_up_b[l], cw, cb, w_dn_b[l], lng1, lnb1,
                       alpha=alpha, tm=t, cwid=256)
    return h_x
```

```python
import functools

import numpy as np
import jax
import jax.numpy as jnp
from jax import lax
from jax.experimental import pallas as pl
from jax.experimental.pallas import tpu as pltpu

F32 = jnp.float32
BF16 = jnp.bfloat16

HEAD_DIM = 64
N_HEADS = 4
N_KV = 2
GROUP_W = N_HEADS * HEAD_DIM
KV_W = N_KV * HEAD_DIM
GRID_W = 64
WIN = 128
BLOCK = 128
NA_ROWS = 8
NA_COLS = 16
ROPE_BASE = 10000.0
EPS = 1e-6
HGRN_L = 64
HGRN_LEVELS = 6
HGRN_CHUNKS_PER_STEP = 4
WINDOW_BLOCKS_PER_STEP = 4
NA_ROWS_PER_STEP = 4
GLOBAL_BLOCKS_PER_STEP = 2
ONES_ROWS = 16

V7X_LANES = 128
V7X_VMEM_LIMIT_BYTES = 56 * 1024 * 1024

COL_QA, COL_KA, COL_VA = 0, 256, 384
COL_QB, COL_KB, COL_VB = 512, 768, 896
COL_C = 1024
COL_QD, COL_KD, COL_VD = 2304, 2560, 2816
IN_WIDTH = 3072
C_WIDTH = 5 * GROUP_W

NEG_INF = float("-inf")
LOG2_E = 1.4426950408889634


def _cparams(sem):
    return pltpu.CompilerParams(dimension_semantics=sem, vmem_limit_bytes=V7X_VMEM_LIMIT_BYTES)


def _const_spec(shape):
    nd = len(shape)
    return pl.BlockSpec(shape, lambda *_: (0,) * nd, pipeline_mode=pl.Buffered(1))


def _layer_spec(shape, layer):
    nd = len(shape)
    return pl.BlockSpec((1,) + tuple(shape[1:]), lambda *_: (layer,) + (0,) * (nd - 1),
                        pipeline_mode=pl.Buffered(1))


def _silu(x):
    return x * jax.nn.sigmoid(x)


def _dot(a, b):
    return jnp.dot(a, b, preferred_element_type=F32)


def _dot_nt(a, b):
    return lax.dot_general(a, b, (((1,), (1,)), ((), ())), preferred_element_type=F32)


def _head_mask(shape, h):
    lane = lax.broadcasted_iota(jnp.int32, shape, len(shape) - 1)
    return (lane // HEAD_DIM) == h


def _stack_heads(q):
    zero = jnp.zeros_like(q)
    return jnp.concatenate([jnp.where(_head_mask(q.shape, h), q, zero) for h in range(N_HEADS)], axis=0)


def _unstack_heads(o, r):
    out = jnp.zeros((r, o.shape[1]), o.dtype)
    for h in range(N_HEADS):
        blk = o[h * r:(h + 1) * r]
        out = jnp.where(_head_mask(blk.shape, h), blk, out)
    return out


def _head_mean_sq(x, bd):
    sq = x * x
    hi = sq.astype(BF16)
    lo = (sq - hi.astype(F32)).astype(BF16)
    return _dot(hi, bd) + _dot(lo, bd)


def _layer_norm(z, g, b):
    mu = jnp.mean(z, axis=-1, keepdims=True)
    zc = z - mu
    var = jnp.mean(zc * zc, axis=-1, keepdims=True)
    return zc * lax.rsqrt(var + EPS) * g + b


def _mod_kernel(s_ref, w_ref, b_ref, o_ref):
    s = _silu(s_ref[...])
    o_ref[0] = jnp.dot(s, w_ref[0], preferred_element_type=F32,
                       precision=lax.Precision.HIGHEST) + b_ref[0]


def _modulation(c, c_ctx, w_mod, b_mod):
    depth, d, d6 = w_mod.shape
    b = c.shape[0]
    rows = 8
    assert b < rows
    s_in = jnp.zeros((rows, d), F32).at[:b].set(c).at[b].set(c_ctx)
    tn = d6 // 4
    return pl.pallas_call(
        _mod_kernel,
        grid=(depth, d6 // tn),
        in_specs=[pl.BlockSpec((rows, d), lambda l, j: (0, 0)),
                  pl.BlockSpec((1, d, tn), lambda l, j: (l, 0, j)),
                  pl.BlockSpec((1, 1, tn), lambda l, j: (l, 0, j))],
        out_specs=pl.BlockSpec((1, rows, tn), lambda l, j: (l, 0, j)),
        out_shape=jax.ShapeDtypeStruct((depth, rows, d6), F32),
        compiler_params=_cparams(("arbitrary", "arbitrary")),
        name="modulation",
    )(s_in, w_mod, b_mod.reshape(depth, 1, d6))


def _dup_heads(x):
    r = pltpu.roll(x, HEAD_DIM, 1)
    lo = lax.broadcasted_iota(jnp.int32, x.shape, 1) < HEAD_DIM
    return jnp.concatenate([jnp.where(lo, x, r), jnp.where(lo, r, x)], axis=1)


def _inproj_kernel(h_ref, sc_ref, sh_ref, w_ref, rc_ref, ra_ref, rb_ref, gq_ref, gk_ref, bd_ref,
                   *outs, is_ctx):
    if is_ctx:
        (qa_ref, ka_ref, va_ref, qb_ref, kb_ref, vbt_ref, c_ref, qd_ref, kd_ref, vd_ref,
         kbx_ref, vbx_ref) = outs
    else:
        qa_ref, ka_ref, va_ref, qbt_ref, kb_ref, vbt_ref, c_ref, qd_ref, kd_ref, vd_ref = outs
    hm = (h_ref[0] * (1.0 + sc_ref[0]) + sh_ref[0]).astype(BF16)
    rc, ra, rb = rc_ref[...], ra_ref[...], rb_ref[...]
    bd = bd_ref[...]
    scale = HEAD_DIM ** -0.5

    def proj(c0, width):
        return _dot(hm, w_ref[0, :, c0:c0 + width])

    def rope(x):
        return x * rc + pltpu.roll(x, V7X_LANES - 16, 1) * ra + pltpu.roll(x, 16, 1) * rb

    def rms(x, gain):
        return x * lax.rsqrt(_head_mean_sq(x, bd) + EPS) * gain

    xq = proj(COL_QA, GROUP_W)
    for j in range(2):
        sl = slice(j * V7X_LANES, (j + 1) * V7X_LANES)
        qa_ref[0, :, sl] = (rope(xq[:, sl]) * scale).astype(BF16)
    ka_ref[0] = _dup_heads(rope(proj(COL_KA, KV_W))).astype(BF16)
    va_ref[0] = _dup_heads(proj(COL_VA, KV_W)).astype(BF16)

    xq = proj(COL_QB, GROUP_W)
    gq, gk = gq_ref[...], gk_ref[...]
    scale_b = scale if is_ctx else scale * LOG2_E
    for j in range(2):
        sl = slice(j * V7X_LANES, (j + 1) * V7X_LANES)
        y = rope(rms(xq[:, sl], gq)) * scale_b
        if is_ctx:
            qb_ref[0, :, sl] = y.astype(BF16)
        else:
            qbt_ref[0, sl, :] = y.T.astype(BF16)
    kb = rope(rms(proj(COL_KB, KV_W), gk))
    kb_ref[0] = kb.astype(BF16)
    vb = proj(COL_VB, KV_W)
    vbt_ref[0] = vb.T.astype(BF16)
    if is_ctx:
        kbx_ref[0] = _dup_heads(kb).astype(BF16)
        vbx_ref[0] = _dup_heads(vb).astype(BF16)

    for j in range(C_WIDTH // GROUP_W):
        c_ref[0, :, j * GROUP_W:(j + 1) * GROUP_W] = proj(COL_C + j * GROUP_W, GROUP_W)

    qd_ref[0] = (proj(COL_QD, GROUP_W) * scale).astype(BF16)
    kd_ref[0] = proj(COL_KD, GROUP_W).astype(BF16)
    vd_ref[0] = proj(COL_VD, GROUP_W).astype(BF16)


def _inproj(h, shift, scale, w_in, layer, rope_tabs, gq, gk, bd128, *, is_ctx, tm):
    b, s, d = h.shape
    assert s % tm == 0
    rc, ra, rb = rope_tabs
    row = lambda width: pl.BlockSpec((1, tm, width), lambda bi, i: (bi, i, 0))
    col = lambda height: pl.BlockSpec((1, height, tm), lambda bi, i: (bi, 0, i))
    sds = lambda shape, dt=BF16: jax.ShapeDtypeStruct(shape, dt)
    if is_ctx:
        out_specs = [row(GROUP_W), row(GROUP_W), row(GROUP_W), row(GROUP_W), row(KV_W), col(KV_W),
                     row(C_WIDTH), row(GROUP_W), row(GROUP_W), row(GROUP_W), row(GROUP_W), row(GROUP_W)]
        out_shape = [sds((b, s, GROUP_W)), sds((b, s, GROUP_W)), sds((b, s, GROUP_W)), sds((b, s, GROUP_W)),
                     sds((b, s, KV_W)), sds((b, KV_W, s)), sds((b, s, C_WIDTH), F32),
                     sds((b, s, GROUP_W)), sds((b, s, GROUP_W)), sds((b, s, GROUP_W)),
                     sds((b, s, GROUP_W)), sds((b, s, GROUP_W))]
    else:
        out_specs = [row(GROUP_W), row(GROUP_W), row(GROUP_W), col(GROUP_W), row(KV_W), col(KV_W),
                     row(C_WIDTH), row(GROUP_W), row(GROUP_W), row(GROUP_W)]
        out_shape = [sds((b, s, GROUP_W)), sds((b, s, GROUP_W)), sds((b, s, GROUP_W)), sds((b, GROUP_W, s)),
                     sds((b, s, KV_W)), sds((b, KV_W, s)), sds((b, s, C_WIDTH), F32),
                     sds((b, s, GROUP_W)), sds((b, s, GROUP_W)), sds((b, s, GROUP_W))]
    tab = pl.BlockSpec((tm, V7X_LANES), lambda bi, i: (i, 0))
    vec = pl.BlockSpec((1, 1, d), lambda bi, i: (bi, 0, 0))
    return pl.pallas_call(
        functools.partial(_inproj_kernel, is_ctx=is_ctx),
        grid=(b, s // tm),
        in_specs=[pl.BlockSpec((1, tm, d), lambda bi, i: (bi, i, 0)), vec, vec,
                  _layer_spec(w_in.shape, layer), tab, tab, tab,
                  _const_spec(gq.shape), _const_spec(gk.shape), _const_spec(bd128.shape)],
        out_specs=out_specs,
        out_shape=out_shape,
        compiler_params=_cparams(("parallel", "arbitrary")),
        name="inproj_ctx" if is_ctx else "inproj",
    )(h, scale, shift, w_in, rc, ra, rb, gq, gk, bd128)


def _attend_many(items):
    scores, maxes = [], []
    for qs, parts, sink_col in items:
        ss, m = [], sink_col
        for k, _, bias in parts:
            s = _dot_nt(qs, k)
            if bias is not None:
                s = s + bias
            ss.append(s)
            smax = jnp.max(s, axis=1, keepdims=True)
            m = smax if m is None else jnp.maximum(m, smax)
        scores.append(ss)
        maxes.append(m)
    probs, dens = [], []
    for (qs, parts, sink_col), ss, m in zip(items, scores, maxes):
        den = None if sink_col is None else jnp.exp(sink_col - m)
        ps = []
        for s in ss:
            p = jnp.exp(s - m)
            psum = jnp.sum(p, axis=1, keepdims=True)
            den = psum if den is None else den + psum
            ps.append(p.astype(BF16))
        probs.append(ps)
        dens.append(den)
    outs = []
    for (qs, parts, _), ps, den in zip(items, probs, dens):
        out = None
        for p, (_, v, _) in zip(ps, parts):
            pv = _dot(p, v)
            out = pv if out is None else out + pv
        outs.append(out * (1.0 / den))
    return outs


def _window_kernel(sink_ref, q_ref, k_ref, v_ref, kc_ref, vc_ref, o_ref, *, seq, n_blk):
    step = pl.program_id(1)
    span = 3 * BLOCK
    sink_col = jnp.concatenate([jnp.full((BLOCK, 1), sink_ref[0, h], F32) for h in range(N_HEADS)], axis=0)
    ctx_part = (kc_ref[0], vc_ref[0], None)
    items = []
    for j in range(n_blk):
        n = step * n_blk + j
        start = pl.multiple_of(jnp.clip((n - 1) * BLOCK, 0, seq - span), BLOCK)
        kw = k_ref[0, pl.ds(start, span), :]
        vw = v_ref[0, pl.ds(start, span), :]
        qs = _stack_heads(q_ref[0, j * BLOCK:(j + 1) * BLOCK, :])
        qpos = n * BLOCK + lax.broadcasted_iota(jnp.int32, (BLOCK, span), 0)
        kpos = start + lax.broadcasted_iota(jnp.int32, (BLOCK, span), 1)
        bias = jnp.where(jnp.abs(qpos - kpos) <= WIN, 0.0, NEG_INF).astype(F32)
        items.append((qs, [(kw, vw, jnp.concatenate([bias] * N_HEADS, axis=0)), ctx_part], sink_col))
    for j, o in enumerate(_attend_many(items)):
        o_ref[0, j * BLOCK:(j + 1) * BLOCK, :] = _unstack_heads(o, BLOCK).astype(BF16)


def _window_attn(q, k, v, kc, vc, sink, *, n_blk):
    b, s, w = q.shape
    t = kc.shape[1]
    rows = n_blk * BLOCK
    assert s % rows == 0 and s >= 3 * BLOCK
    full = lambda n: pl.BlockSpec((1, n, w), lambda bi, i: (bi, 0, 0))
    return pl.pallas_call(
        functools.partial(_window_kernel, seq=s, n_blk=n_blk),
        grid=(b, s // rows),
        in_specs=[pl.BlockSpec(memory_space=pltpu.SMEM),
                  pl.BlockSpec((1, rows, w), lambda bi, i: (bi, i, 0)),
                  full(s), full(s), full(t), full(t)],
        out_specs=pl.BlockSpec((1, rows, w), lambda bi, i: (bi, i, 0)),
        out_shape=jax.ShapeDtypeStruct((b, s, w), BF16),
        compiler_params=_cparams(("parallel", "arbitrary")),
        name="window_attn",
    )(sink.reshape(1, N_HEADS).astype(F32), q, k, v, kc, vc)


def _na_kernel(q_ref, k_ref, v_ref, kc_ref, vc_ref, bias_ref, o_ref, *, rows, n_rows):
    step = pl.program_id(1)
    span = NA_ROWS * GRID_W
    ctx_part = (kc_ref[0], vc_ref[0], None)
    items = []
    for j in range(n_rows):
        r = step * n_rows + j
        r0 = jnp.clip(r - NA_ROWS // 2, 0, rows - NA_ROWS)
        start = pl.multiple_of(r0 * GRID_W, GRID_W)
        kw = k_ref[0, pl.ds(start, span), :]
        vw = v_ref[0, pl.ds(start, span), :]
        qs = _stack_heads(q_ref[0, j * GRID_W:(j + 1) * GRID_W, :])
        off0 = r0 - r + (NA_ROWS - 1)
        bias = jnp.concatenate([bias_ref[off0 + 2 * i] for i in range(NA_ROWS // 2)], axis=1)
        items.append((qs, [(kw, vw, bias), ctx_part], None))
    for j, o in enumerate(_attend_many(items)):
        o_ref[0, j * GRID_W:(j + 1) * GRID_W, :] = _unstack_heads(o, GRID_W).astype(BF16)


def _na_bias_table(rpb):
    n_h, n_ro, n_co = rpb.shape
    q = np.arange(GRID_W)
    kc = np.arange(GRID_W)
    coff = np.clip(kc[None, :] - q[:, None], -(NA_COLS - 1), NA_COLS - 1) + (NA_COLS - 1)
    c0 = np.clip(q - NA_COLS // 2, 0, GRID_W - NA_COLS)
    ok = (kc[None, :] >= c0[:, None]) & (kc[None, :] < c0[:, None] + NA_COLS)
    ro = np.arange(n_ro - 1)[:, None, None, None, None] + np.arange(2)[None, None, None, :, None]
    hh = np.arange(n_h)[None, :, None, None, None]
    flat = (hh * n_ro + ro) * n_co + coff[None, None, :, None, :]
    flat = flat.reshape(n_ro - 1, n_h * GRID_W, 2 * GRID_W)
    ok2 = np.broadcast_to(ok[None, None, :, None, :], (n_ro - 1, n_h, GRID_W, 2, GRID_W)).reshape(flat.shape)
    return jnp.where(ok2, jnp.take(rpb.astype(F32).reshape(-1), flat), NEG_INF)


def _na_attn(q, k, v, kc, vc, bias_tab, *, n_rows):
    b, s, w = q.shape
    t = kc.shape[1]
    rows = s // GRID_W
    assert s % GRID_W == 0 and rows >= NA_ROWS and rows % n_rows == 0
    full = lambda n: pl.BlockSpec((1, n, w), lambda bi, i: (bi, 0, 0))
    blk = n_rows * GRID_W
    return pl.pallas_call(
        functools.partial(_na_kernel, rows=rows, n_rows=n_rows),
        grid=(b, rows // n_rows),
        in_specs=[pl.BlockSpec((1, blk, w), lambda bi, r: (bi, r, 0)),
                  full(s), full(s), full(t), full(t), _const_spec(bias_tab.shape)],
        out_specs=pl.BlockSpec((1, blk, w), lambda bi, r: (bi, r, 0)),
        out_shape=jax.ShapeDtypeStruct((b, s, w), BF16),
        compiler_params=_cparams(("parallel", "arbitrary")),
        name="na_attn",
    )(q, k, v, kc, vc, bias_tab)


def _ctx_attn_kernel(sink_ref, qa_ref, ka_ref, va_ref, qb_ref, kb_ref, vb_ref, qd_ref, kd_ref, vd_ref,
                     oa_ref, ob_ref, od_ref):
    t = qa_ref.shape[1]
    sink_col = jnp.concatenate([jnp.full((t, 1), sink_ref[0, h], F32) for h in range(N_HEADS)], axis=0)
    groups = ((qa_ref, ka_ref, va_ref, sink_col), (qb_ref, kb_ref, vb_ref, None), (qd_ref, kd_ref, vd_ref, None))
    outs = _attend_many([(_stack_heads(q_ref[0]), [(k_ref[0], v_ref[0], None)], sink)
                         for q_ref, k_ref, v_ref, sink in groups])
    for o_ref, o in zip((oa_ref, ob_ref, od_ref), outs):
        o_ref[0] = _unstack_heads(o, t).astype(BF16)


def _ctx_attn(sink, qa, ka, va, qb, kb, vb, qd, kd, vd):
    b, t, w = qa.shape
    blk = pl.BlockSpec((1, t, w), lambda bi: (bi, 0, 0))
    out = jax.ShapeDtypeStruct((b, t, w), BF16)
    return pl.pallas_call(
        _ctx_attn_kernel,
        grid=(b,),
        in_specs=[pl.BlockSpec(memory_space=pltpu.SMEM)] + [blk] * 9,
        out_specs=[blk] * 3,
        out_shape=[out] * 3,
        compiler_params=_cparams(("arbitrary",)),
        name="ctx_attn",
    )(sink.reshape(1, N_HEADS).astype(F32), qa, ka, va, qb, kb, vb, qd, kd, vd)


def _global_kernel(q_ref, k_ref, vt_ref, kc_ref, vtc_ref, o_ref, *, tq, kt):
    kv = pl.program_id(1)
    n_blk = q_ref.shape[2] // tq
    sel = (lax.broadcasted_iota(jnp.int32, (2 * HEAD_DIM, tq), 0) // HEAD_DIM) == kv

    def block_rhs(blk):
        qt = q_ref[0, :, blk * tq:(blk + 1) * tq]
        zero = jnp.zeros_like(qt)

        def padded(g):
            qg = qt[g * HEAD_DIM:(g + 1) * HEAD_DIM, :]
            return jnp.where(sel, jnp.concatenate([qg, qg], axis=0), zero)

        return jnp.concatenate([padded(0), padded(1)], axis=1)

    rhs = [block_rhs(blk) for blk in range(n_blk)]
    n_lat = k_ref.shape[1] // kt
    n_tiles = n_lat + kc_ref.shape[1] // kt
    ones = jnp.ones((ONES_ROWS, kt), BF16)

    def k_tile(i):
        return k_ref[0, i * kt:(i + 1) * kt, :] if i < n_lat else kc_ref[0, (i - n_lat) * kt:(i - n_lat + 1) * kt, :]

    def vt_tile(i):
        vt = (vt_ref[0, :, i * kt:(i + 1) * kt] if i < n_lat
              else vtc_ref[0, :, (i - n_lat) * kt:(i - n_lat + 1) * kt])
        return jnp.concatenate([vt, ones], axis=0)

    scores, tile_max, run_max, probs = {}, {}, {}, {}
    acc = [None] * n_blk
    n_items = n_blk * n_tiles
    for i in range(n_items + 2):
        if i < n_items:
            blk, c = divmod(i, n_tiles)
            scores[i] = _dot(k_tile(c), rhs[blk])
            tile_max[i] = jnp.max(scores[i], axis=0, keepdims=True)
        j = i - 1
        if 0 <= j < n_items:
            first = j % n_tiles == 0
            run_max[j] = tile_max[j] if first else jnp.maximum(run_max[j - 1], tile_max[j])
            probs[j] = jnp.exp2(scores.pop(j) - run_max[j]).astype(BF16)
        j = i - 2
        if 0 <= j < n_items:
            blk, c = divmod(j, n_tiles)
            pv = _dot(vt_tile(c), probs.pop(j))
            acc[blk] = pv if c == 0 else acc[blk] * jnp.exp2(run_max[j - 1] - run_max[j]) + pv
            if c == n_tiles - 1:
                o = acc[blk][0:HEAD_DIM] * (1.0 / acc[blk][HEAD_DIM:HEAD_DIM + 1])
                ot = jnp.concatenate([o[:, :tq], o[:, tq:]], axis=0)
                o_ref[0, blk * tq:(blk + 1) * tq, :] = ot.T.astype(BF16)


def _global_attn(qt, k, vt, kc, vtc, *, tq, kt, n_blk):
    b, w, s = qt.shape
    t = kc.shape[1]
    rows = n_blk * tq
    assert s % rows == 0 and s % kt == 0 and t % kt == 0
    return pl.pallas_call(
        functools.partial(_global_kernel, tq=tq, kt=kt),
        grid=(b, N_KV, s // rows),
        in_specs=[pl.BlockSpec((1, 2 * HEAD_DIM, rows), lambda bi, kv, i: (bi, kv, i)),
                  pl.BlockSpec((1, s, KV_W), lambda bi, kv, i: (bi, 0, 0)),
                  pl.BlockSpec((1, HEAD_DIM, s), lambda bi, kv, i: (bi, kv, 0)),
                  pl.BlockSpec((1, t, KV_W), lambda bi, kv, i: (bi, 0, 0)),
                  pl.BlockSpec((1, HEAD_DIM, t), lambda bi, kv, i: (bi, kv, 0))],
        out_specs=pl.BlockSpec((1, rows, 2 * HEAD_DIM), lambda bi, kv, i: (bi, i, kv)),
        out_shape=jax.ShapeDtypeStruct((b, s, w), BF16),
        compiler_params=_cparams(("parallel", "arbitrary", "arbitrary")),
        name="global_attn",
    )(qt, k, vt, kc, vtc)


def _hgrn_constants():
    n = HGRN_L
    tm = np.zeros((2, 2 * n, n), np.float32)
    mk = np.zeros((2, HGRN_LEVELS + 1, n, n), np.float32)
    for d in range(2):
        pos = np.arange(n) if d == 0 else n - 1 - np.arange(n)
        p, q = pos[:, None], pos[None, :]
        for li in range(HGRN_LEVELS):
            h = n >> (li + 1)
            same_blk = (p // (2 * h)) == (q // (2 * h))
            up_p, up_q = (p % (2 * h)) >= h, (q % (2 * h)) >= h
            mk[d, li] = same_blk & up_p & ~up_q
            if h == 2:
                same_half = same_blk & (up_p == up_q)
                tm[d, 0:n] = np.where(up_p, same_half & (q <= p), same_half & (q > p))
        mk[d, HGRN_LEVELS] = np.eye(n)
        tm[d, n:2 * n] = q <= p
    mk = np.tile(mk, (1, 1, N_HEADS, 1))
    return jnp.asarray(tm, BF16), jnp.asarray(mk, F32)


def _hgrn_seq_kernel(qf_ref, vf_ref, zf_ref, qb_ref, vb_ref, zb_ref, lb_ref, t_ref, mk_ref, bd_ref, *rest,
                     g_chunks, has_init, emit_state):
    rest = list(rest)
    init_ref = rest.pop(0) if has_init else None
    of_ref, ob_ref = rest.pop(0), rest.pop(0)
    so_ref = rest.pop(0) if emit_state else None
    st_ref = rest.pop(0)
    step = pl.program_id(1)
    n = HGRN_L

    @pl.when(step == 0)
    def _():
        st_ref[...] = init_ref[0] if has_init else jnp.zeros_like(st_ref)

    ins = ((qf_ref, vf_ref, zf_ref), (qb_ref, vb_ref, zb_ref))
    outs = (of_ref, ob_ref)
    items = [(d, (g if d == 0 else g_chunks - 1 - g) * n) for g in range(g_chunks) for d in (0, 1)]

    work = []
    for d, r0 in items:
        q = ins[d][0][0, r0:r0 + n, :]
        v = ins[d][1][0, r0:r0 + n, :]
        z = ins[d][2][0, r0:r0 + n, :]
        lb = lb_ref[d]
        f = lb + (1.0 - lb) * jax.nn.sigmoid(z)
        g = jnp.log(f)
        g_hi = g.astype(BF16)
        g_lo = (g - g_hi.astype(F32)).astype(BF16)
        e = _dot(t_ref[d], g_hi) + _dot(t_ref[d], g_lo)
        work.append(dict(qq=_silu(q), kk=1.0 - f, f=f, v=v, e2=e[0:n], b=e[n:2 * n],
                         a=jnp.zeros((N_HEADS * n, n), F32)))

    row = lax.broadcasted_iota(jnp.int32, (n, GROUP_W), 0)

    def level_scale(w, d, h):
        if h == 2:
            return jnp.exp(w["e2"])
        b = w["b"]
        blocks = []
        for j in range(n // (2 * h)):
            m = 2 * h * j + (h - 1 if d == 0 else h)
            blocks.append(jnp.broadcast_to(b[m:m + 1, :], (2 * h, GROUP_W)))
        bm = blocks[0] if len(blocks) == 1 else jnp.concatenate(blocks, axis=0)
        later = ((row // h) % 2) == (1 if d == 0 else 0)
        return jnp.exp(jnp.where(later, b - bm, bm - b))

    for li in range(HGRN_LEVELS + 1):
        h = n >> (li + 1)
        for (d, _), w in zip(items, work):
            if h >= 2:
                sc = level_scale(w, d, h)
                qs, ks = w["qq"] * sc, w["kk"] * sc
            elif h == 1:
                qs, ks = w["qq"] * w["f"], w["kk"]
            else:
                qs, ks = w["qq"], w["kk"]
            w["a"] = w["a"] + _dot_nt(_stack_heads(qs.astype(BF16)), ks.astype(BF16)) * mk_ref[d, li]

    for (d, _), w in zip(items, work):
        b_incl = w["b"]
        b_tot = b_incl[n - 1:n] if d == 0 else b_incl[0:1]
        vb = w["v"].astype(BF16)
        w["q_hat"] = (w["qq"] * jnp.exp(b_incl)).astype(BF16)
        k_hat = (w["kk"] * jnp.exp(b_tot - b_incl)).astype(BF16)
        w["decay"] = jnp.exp(b_tot)
        w["o"] = _unstack_heads(_dot(w["a"].astype(BF16), vb), n)
        w["upd"] = _dot(w["v"].T.astype(BF16), k_hat) * bd_ref[...]

    st = [st_ref[0], st_ref[1]]
    for (d, r0), w in zip(items, work):
        outs[d][0, r0:r0 + n, :] = w["o"] + _dot_nt(w["q_hat"], st[d].astype(BF16))
        st[d] = st[d] * w["decay"] + w["upd"]
    st_ref[0] = st[0]
    st_ref[1] = st[1]
    if emit_state:
        so_ref[0, 0] = st[0]
        so_ref[0, 1] = st[1]


def _hgrn_seq(c_raw, lb, consts, init_state, *, g_chunks, emit_state):
    tmat, masks, bd = consts
    b, s, _ = c_raw.shape
    rows = g_chunks * HGRN_L
    assert s % rows == 0
    n_steps = s // rows
    has_init = init_state is not None
    fwd = lambda col: pl.BlockSpec((1, rows, GROUP_W), lambda bi, st: (bi, st, col))
    bwd = lambda col: pl.BlockSpec((1, rows, GROUP_W), lambda bi, st: (bi, n_steps - 1 - st, col))
    state_spec = pl.BlockSpec((1, 2, GROUP_W, GROUP_W), lambda bi, st: (bi, 0, 0, 0))
    in_specs = [fwd(0), fwd(1), fwd(2), bwd(0), bwd(1), bwd(3),
                _const_spec((2, 1, GROUP_W)), _const_spec(tmat.shape), _const_spec(masks.shape),
                _const_spec(bd.shape)]
    args = [c_raw] * 6 + [lb.reshape(2, 1, GROUP_W), tmat, masks, bd]
    if has_init:
        in_specs.append(state_spec)
        args.append(init_state)
    out_specs = [fwd(0), bwd(0)]
    out_shape = [jax.ShapeDtypeStruct((b, s, GROUP_W), F32)] * 2
    if emit_state:
        out_specs.append(state_spec)
        out_shape.append(jax.ShapeDtypeStruct((b, 2, GROUP_W, GROUP_W), F32))
    return pl.pallas_call(
        functools.partial(_hgrn_seq_kernel, g_chunks=g_chunks, has_init=has_init, emit_state=emit_state),
        grid=(b, n_steps),
        in_specs=in_specs,
        out_specs=out_specs,
        out_shape=out_shape,
        scratch_shapes=[pltpu.VMEM((2, GROUP_W, GROUP_W), F32)],
        compiler_params=_cparams(("parallel", "arbitrary")),
        name="hgrn_ctx" if emit_state else "hgrn",
    )(*args)


def _outproj_kernel(x_ref, oa_ref, ob_ref, of_ref, obw_ref, gate_ref, od_ref, w_ref, onorm_ref, bd_ref,
                    ga_ref, lng_ref, lnb_ref, o_ref, *, alpha):
    oc = of_ref[0] + obw_ref[0]
    y = oc * lax.rsqrt(_head_mean_sq(oc, bd_ref[...]) + EPS) * onorm_ref[...]
    y = y * _silu(gate_ref[0])
    mix = jnp.concatenate([oa_ref[0], ob_ref[0], y.astype(BF16), od_ref[0]], axis=1)
    z = alpha * x_ref[0] + ga_ref[0] * _dot(mix, w_ref[0])
    o_ref[0] = _layer_norm(z, lng_ref[...], lnb_ref[...])


def _outproj(x, oa, ob, oc_f, oc_b, c_raw, od, w_out, layer, onorm, bd256, gate_a, ln_g, ln_b, *, alpha, tm):
    b, s, d = x.shape
    assert s % tm == 0
    grp = pl.BlockSpec((1, tm, GROUP_W), lambda bi, i: (bi, i, 0))
    vec = pl.BlockSpec((1, 1, d), lambda bi, i: (bi, 0, 0))
    return pl.pallas_call(
        functools.partial(_outproj_kernel, alpha=alpha),
        grid=(b, s // tm),
        in_specs=[pl.BlockSpec((1, tm, d), lambda bi, i: (bi, i, 0)), grp, grp, grp, grp,
                  pl.BlockSpec((1, tm, GROUP_W), lambda bi, i: (bi, i, 4)),
                  grp, _layer_spec(w_out.shape, layer), _const_spec(onorm.shape), _const_spec(bd256.shape),
                  vec, _const_spec(ln_g.shape), _const_spec(ln_b.shape)],
        out_specs=pl.BlockSpec((1, tm, d), lambda bi, i: (bi, i, 0)),
        out_shape=jax.ShapeDtypeStruct((b, s, d), F32),
        compiler_params=_cparams(("parallel", "arbitrary")),
        name="outproj",
    )(x, oa, ob, oc_f, oc_b, c_raw, od, w_out, onorm, bd256, gate_a, ln_g, ln_b)


def _ffn_kernel(h_ref, hp_ref, hn_ref, sc_ref, sh_ref, gt_ref, wup_ref, cw_ref, cb_ref, wdn_ref,
                lng_ref, lnb_ref, o_ref, g_scr, *, alpha, d_ff, cwid, n_tiles):
    i = pl.program_id(1)
    tm = h_ref.shape[1]
    sc = 1.0 + sc_ref[0]
    sh = sh_ref[0]
    h = h_ref[0]
    halo = jnp.concatenate([hp_ref[0], hn_ref[0]], axis=0)
    lhs = jnp.concatenate([(h * sc + sh).astype(BF16), (halo * sc + sh).astype(BF16)], axis=0)
    has_prev = (i > 0).astype(F32)
    has_next = (i < n_tiles - 1).astype(F32)
    row = lax.broadcasted_iota(jnp.int32, (tm, cwid), 0)
    first, last = row == 0, row == tm - 1

    def conv_cols(c0):
        u = _dot(lhs, wup_ref[0, :, c0:c0 + cwid])
        um = u[:tm]
        u_prev = u[tm + 7:tm + 8] * has_prev
        u_next = u[tm + 8:tm + 9] * has_next
        dn = jnp.where(first, u_prev, pltpu.roll(um, 1, 0))
        up = jnp.where(last, u_next, pltpu.roll(um, tm - 1, 0))
        w = cw_ref[:, c0:c0 + cwid]
        return dn * w[0:1] + um * w[1:2] + up * w[2:3] + cb_ref[:, c0:c0 + cwid]

    for j in range(d_ff // cwid):
        a = conv_cols(j * cwid)
        v = conv_cols(d_ff + j * cwid)
        g_scr[:, j * cwid:(j + 1) * cwid] = (_silu(a) * v).astype(BF16)
    z = alpha * h + gt_ref[0] * _dot(g_scr[...], wdn_ref[0])
    o_ref[0] = _layer_norm(z, lng_ref[...], lnb_ref[...])


def _ffn(h, scale, shift, gate, w_up, conv_w, conv_b, w_down, layer, ln_g, ln_b, *, alpha, tm, cwid):
    b, s, d = h.shape
    d_ff = w_down.shape[1]
    assert s % tm == 0 and tm % 16 == 0 and d_ff % cwid == 0
    n_tiles = s // tm
    r8 = tm // 8
    vec = pl.BlockSpec((1, 1, d), lambda bi, i: (bi, 0, 0))
    return pl.pallas_call(
        functools.partial(_ffn_kernel, alpha=alpha, d_ff=d_ff, cwid=cwid, n_tiles=n_tiles),
        grid=(b, n_tiles),
        in_specs=[pl.BlockSpec((1, tm, d), lambda bi, i: (bi, i, 0)),
                  pl.BlockSpec((1, 8, d), lambda bi, i: (bi, jnp.maximum(i * r8 - 1, 0), 0)),
                  pl.BlockSpec((1, 8, d), lambda bi, i: (bi, jnp.minimum((i + 1) * r8, s // 8 - 1), 0)),
                  vec, vec, vec,
                  _layer_spec(w_up.shape, layer), _const_spec(conv_w.shape), _const_spec(conv_b.shape),
                  _layer_spec(w_down.shape, layer), _const_spec(ln_g.shape), _const_spec(ln_b.shape)],
        out_specs=pl.BlockSpec((1, tm, d), lambda bi, i: (bi, i, 0)),
        out_shape=jax.ShapeDtypeStruct((b, s, d), F32),
        scratch_shapes=[pltpu.VMEM((tm, d_ff), BF16)],
        compiler_params=_cparams(("parallel", "arbitrary")),
        name="conv_ffn",
    )(h, h, h, scale, shift, gate, w_up, conv_w, conv_b, w_down, ln_g, ln_b)


def _rope_tables(n, with_rotation):
    lane = np.arange(V7X_LANES)
    if not with_rotation:
        one = jnp.ones((n, V7X_LANES), F32)
        zero = jnp.zeros((n, V7X_LANES), F32)
        return one, zero, zero
    t = np.arange(n)
    pos = np.stack([t // GRID_W, t % GRID_W], -1).astype(np.float64)
    n_freq = HEAD_DIM // 4
    inv = ROPE_BASE ** (-np.arange(n_freq, dtype=np.float64) / n_freq)
    ang = pos[:, :, None] * inv
    axis = (lane % HEAD_DIM) // (HEAD_DIM // 2)
    freq = lane % n_freq
    ang_l = ang[:, axis, freq]
    first = (lane % (2 * n_freq)) < n_freq
    cos, sin = np.cos(ang_l), np.sin(ang_l)
    tabs = (cos, np.where(first, -sin, 0.0), np.where(first, 0.0, sin))
    return tuple(jnp.asarray(a, F32) for a in tabs)


def _block_diag_mean(width):
    lane = np.arange(width)
    same = (lane[:, None] // HEAD_DIM) == (lane[None, :] // HEAD_DIM)
    return jnp.asarray(same / HEAD_DIM, BF16)


def kernel(x, c, ctx, c_ctx, w_mod, b_mod, w_in, w_out, ln_g, ln_b, attn_sink, qk_norm, hgrn_lb_logits,
           hgrn_onorm, na_rpb, ffn_w_up, ffn_conv_w, ffn_conv_b, ffn_w_down):
    depth = w_mod.shape[0]
    b, s, d = x.shape
    t = ctx.shape[1]
    alpha = (2 * depth) ** 0.25

    mods = _modulation(c, c_ctx, w_mod, b_mod)
    rope_lat = _rope_tables(s, True)
    rope_ctx = _rope_tables(t, False)
    bd128, bd256 = _block_diag_mean(KV_W), _block_diag_mean(GROUP_W)
    lane = np.arange(GROUP_W)
    head_bd = jnp.asarray((lane[:, None] // HEAD_DIM) == (lane[None, :] // HEAD_DIM), F32)
    hgrn_consts = _hgrn_constants() + (head_bd,)
    p_lb = jax.nn.softmax(hgrn_lb_logits.astype(F32), axis=1)
    lb_all = jnp.cumsum(p_lb, axis=1) - p_lb[:, :1]
    w_in_b, w_out_b = w_in.astype(BF16), w_out.astype(BF16)
    w_up_b, w_dn_b = ffn_w_up.astype(BF16), ffn_w_down.astype(BF16)

    h_x, h_c = x, ctx
    for l in range(depth):
        need_ctx = l < depth - 1
        mod = mods[l]
        sa, ca, ga, sf, cf, gf = (mod[:, i * d:(i + 1) * d] for i in range(6))
        lat = lambda a: a[:b, None, :]
        con = lambda a: jnp.broadcast_to(a[b][None, None, :], (b, 1, d))
        gq = jnp.tile(qk_norm[l, 0], 2)[None, :]
        gk = jnp.tile(qk_norm[l, 1], 2)[None, :]

        (qa, ka, va, qbt, kb, vbt, c_raw, qd, kd, vd) = _inproj(
            h_x, lat(sa), lat(ca), w_in_b, l, rope_lat, gq, gk, bd128, is_ctx=False, tm=512)
        (qa_c, ka_c, va_c, qb_c, kb_c, vbt_c, c_raw_c, qd_c, kd_c, vd_c, kbx_c, vbx_c) = _inproj(
            h_c, con(sa), con(ca), w_in_b, l, rope_ctx, gq, gk, bd128, is_ctx=True, tm=t)

        o_a = _window_attn(qa, ka, va, ka_c, va_c, attn_sink[l], n_blk=WINDOW_BLOCKS_PER_STEP)

        o_b = _global_attn(qbt, kb, vbt, kb_c, vbt_c, tq=256, kt=256, n_blk=GLOBAL_BLOCKS_PER_STEP)

        ocf_c, ocb_c, ctx_state = _hgrn_seq(c_raw_c, lb_all[:, l], hgrn_consts, None,
                                            g_chunks=t // HGRN_L, emit_state=True)
        oc_f, oc_b = _hgrn_seq(c_raw, lb_all[:, l], hgrn_consts, ctx_state,
                               g_chunks=HGRN_CHUNKS_PER_STEP, emit_state=False)

        o_d = _na_attn(qd, kd, vd, kd_c, vd_c, _na_bias_table(na_rpb[l]), n_rows=NA_ROWS_PER_STEP)

        onorm = hgrn_onorm[l][None, :]
        lng0, lnb0 = ln_g[l, 0][None, :], ln_b[l, 0][None, :]
        lng1, lnb1 = ln_g[l, 1][None, :], ln_b[l, 1][None, :]
        cw, cb = ffn_conv_w[l], ffn_conv_b[l][None, :]
        h_x = _outproj(h_x, o_a, o_b, oc_f, oc_b, c_raw, o_d, w_out_b, l, onorm, bd256, lat(ga), lng0, lnb0,
                       alpha=alpha, tm=1024)
        h_x = _ffn(h_x, lat(cf), lat(sf), lat(gf), w_up_b, cw, cb, w_dn_b, l, lng1, lnb1,
                   alpha=alpha, tm=512, cwid=256)
        if need_ctx:
            oa_c, ob_c, od_c = _ctx_attn(attn_sink[l], qa_c, ka_c, va_c, qb_c, kbx_c, vbx_c, qd_c, kd_c, vd_c)
            h_c = _outproj(h_c, oa_c, ob_c, ocf_c, ocb_c, c_raw_c, od_c, w_out_b, l, onorm, bd256, con(ga),
                           lng0, lnb0, alpha=alpha, tm=t)
            h_c = _ffn(h_c, con(cf), con(sf), con(gf), w_up_b, cw, cb, w_dn_b, l, lng1, lnb1,
                       alpha=alpha, tm=t, cwid=256)
    return h_x
```

```python
import functools

import numpy as np
import jax
import jax.numpy as jnp
from jax import lax
from jax.experimental import pallas as pl
from jax.experimental.pallas import tpu as pltpu

F32 = jnp.float32
BF16 = jnp.bfloat16

HEAD_DIM = 64
N_HEADS = 4
N_KV = 2
GROUP_W = N_HEADS * HEAD_DIM
KV_W = N_KV * HEAD_DIM
GRID_W = 64
WIN = 128
BLOCK = 128
NA_ROWS = 8
NA_COLS = 16
ROPE_BASE = 10000.0
EPS = 1e-6
HGRN_L = 64
HGRN_LEVELS = 6
HGRN_CHUNKS_PER_STEP = 4
WINDOW_PAIRS_PER_STEP = 2
NA_ROWS_PER_STEP = 4
GLOBAL_BLOCKS_PER_STEP = 2
ONES_ROWS = 16

V7X_LANES = 128
V7X_VMEM_LIMIT_BYTES = 56 * 1024 * 1024

COL_QA, COL_KA, COL_VA = 0, 256, 384
COL_QB, COL_KB, COL_VB = 512, 768, 896
COL_C = 1024
COL_QD, COL_KD, COL_VD = 2304, 2560, 2816
IN_WIDTH = 3072
C_WIDTH = 5 * GROUP_W

NEG_INF = float("-inf")
LOG2_E = 1.4426950408889634


def _cparams(sem):
    return pltpu.CompilerParams(dimension_semantics=sem, vmem_limit_bytes=V7X_VMEM_LIMIT_BYTES)


def _const_spec(shape):
    nd = len(shape)
    return pl.BlockSpec(shape, lambda *_: (0,) * nd, pipeline_mode=pl.Buffered(1))


def _layer_spec(shape, layer):
    nd = len(shape)
    return pl.BlockSpec((1,) + tuple(shape[1:]), lambda *_: (layer,) + (0,) * (nd - 1),
                        pipeline_mode=pl.Buffered(1))


def _silu(x):
    return x * jax.nn.sigmoid(x)


def _dot(a, b):
    return jnp.dot(a, b, preferred_element_type=F32)


def _dot_nt(a, b):
    return lax.dot_general(a, b, (((1,), (1,)), ((), ())), preferred_element_type=F32)


def _head_mask(shape, h):
    lane = lax.broadcasted_iota(jnp.int32, shape, len(shape) - 1)
    return (lane // HEAD_DIM) == h


def _stack_heads(q):
    zero = jnp.zeros_like(q)
    return jnp.concatenate([jnp.where(_head_mask(q.shape, h), q, zero) for h in range(N_HEADS)], axis=0)


def _unstack_heads(o, r):
    out = jnp.zeros((r, o.shape[1]), o.dtype)
    for h in range(N_HEADS):
        blk = o[h * r:(h + 1) * r]
        out = jnp.where(_head_mask(blk.shape, h), blk, out)
    return out


def _head_mean_sq(x, bd):
    sq = x * x
    hi = sq.astype(BF16)
    lo = (sq - hi.astype(F32)).astype(BF16)
    return _dot(hi, bd) + _dot(lo, bd)


def _layer_norm(z, g, b):
    mu = jnp.mean(z, axis=-1, keepdims=True)
    zc = z - mu
    var = jnp.mean(zc * zc, axis=-1, keepdims=True)
    return zc * lax.rsqrt(var + EPS) * g + b


def _mod_kernel(s_ref, w_ref, b_ref, o_ref):
    s = _silu(s_ref[...])
    o_ref[0] = jnp.dot(s, w_ref[0], preferred_element_type=F32,
                       precision=lax.Precision.HIGHEST) + b_ref[0]


def _modulation(c, c_ctx, w_mod, b_mod):
    depth, d, d6 = w_mod.shape
    b = c.shape[0]
    rows = 8
    assert b < rows
    s_in = jnp.zeros((rows, d), F32).at[:b].set(c).at[b].set(c_ctx)
    tn = d6 // 4
    return pl.pallas_call(
        _mod_kernel,
        grid=(depth, d6 // tn),
        in_specs=[pl.BlockSpec((rows, d), lambda l, j: (0, 0)),
                  pl.BlockSpec((1, d, tn), lambda l, j: (l, 0, j)),
                  pl.BlockSpec((1, 1, tn), lambda l, j: (l, 0, j))],
        out_specs=pl.BlockSpec((1, rows, tn), lambda l, j: (l, 0, j)),
        out_shape=jax.ShapeDtypeStruct((depth, rows, d6), F32),
        compiler_params=_cparams(("arbitrary", "arbitrary")),
        name="modulation",
    )(s_in, w_mod, b_mod.reshape(depth, 1, d6))


def _dup_heads(x):
    r = pltpu.roll(x, HEAD_DIM, 1)
    lo = lax.broadcasted_iota(jnp.int32, x.shape, 1) < HEAD_DIM
    return jnp.concatenate([jnp.where(lo, x, r), jnp.where(lo, r, x)], axis=1)


def _inproj_outputs(is_ctx):
    gqa = lambda g: [("q" + g, "row" if is_ctx else "col", GROUP_W, BF16), ("k" + g, "row", KV_W, BF16),
                     ("v" + g + "t", "col", KV_W, BF16)]
    dup = lambda g: [("k" + g + "x", "row", GROUP_W, BF16), ("v" + g + "x", "row", GROUP_W, BF16)]
    outs = gqa("a") + gqa("b")
    if is_ctx:
        outs += dup("a") + dup("b")
    return outs + [("c", "row", C_WIDTH, F32), ("qd", "row", GROUP_W, BF16), ("kd", "row", GROUP_W, BF16),
                   ("vd", "row", GROUP_W, BF16)]


def _inproj_kernel(h_ref, sc_ref, sh_ref, w_ref, rc_ref, ra_ref, rb_ref, gq_ref, gk_ref, bd_ref,
                   *outs, is_ctx):
    o = {name: ref for (name, _, _, _), ref in zip(_inproj_outputs(is_ctx), outs)}
    hm = (h_ref[0] * (1.0 + sc_ref[0]) + sh_ref[0]).astype(BF16)
    rc, ra, rb = rc_ref[...], ra_ref[...], rb_ref[...]
    bd = bd_ref[...]
    q_scale = HEAD_DIM ** -0.5 * (1.0 if is_ctx else LOG2_E)

    def proj(c0, width):
        return _dot(hm, w_ref[0, :, c0:c0 + width])

    def rope(x):
        return x * rc + pltpu.roll(x, V7X_LANES - 16, 1) * ra + pltpu.roll(x, 16, 1) * rb

    def rms(x, gain):
        return x * lax.rsqrt(_head_mean_sq(x, bd) + EPS) * gain

    def gqa_group(g, c_q, c_k, c_v, q_fn, k_fn):
        xq = proj(c_q, GROUP_W)
        for j in range(2):
            sl = slice(j * V7X_LANES, (j + 1) * V7X_LANES)
            y = rope(q_fn(xq[:, sl])) * q_scale
            if is_ctx:
                o["q" + g][0, :, sl] = y.astype(BF16)
            else:
                o["q" + g][0, sl, :] = y.T.astype(BF16)
        k = rope(k_fn(proj(c_k, KV_W)))
        v = proj(c_v, KV_W)
        o["k" + g][0] = k.astype(BF16)
        o["v" + g + "t"][0] = v.T.astype(BF16)
        if is_ctx:
            o["k" + g + "x"][0] = _dup_heads(k).astype(BF16)
            o["v" + g + "x"][0] = _dup_heads(v).astype(BF16)

    gq, gk = gq_ref[...], gk_ref[...]
    gqa_group("a", COL_QA, COL_KA, COL_VA, lambda x: x, lambda x: x)
    gqa_group("b", COL_QB, COL_KB, COL_VB, lambda x: rms(x, gq), lambda x: rms(x, gk))

    for j in range(C_WIDTH // GROUP_W):
        o["c"][0, :, j * GROUP_W:(j + 1) * GROUP_W] = proj(COL_C + j * GROUP_W, GROUP_W)

    o["qd"][0] = (proj(COL_QD, GROUP_W) * HEAD_DIM ** -0.5).astype(BF16)
    o["kd"][0] = proj(COL_KD, GROUP_W).astype(BF16)
    o["vd"][0] = proj(COL_VD, GROUP_W).astype(BF16)


def _inproj(h, shift, scale, w_in, layer, rope_tabs, gq, gk, bd128, *, is_ctx, tm):
    b, s, d = h.shape
    assert s % tm == 0
    rc, ra, rb = rope_tabs
    outs = _inproj_outputs(is_ctx)
    out_specs = [pl.BlockSpec((1, tm, w), lambda bi, i: (bi, i, 0)) if lay == "row"
                 else pl.BlockSpec((1, w, tm), lambda bi, i: (bi, 0, i)) for _, lay, w, _ in outs]
    out_shape = [jax.ShapeDtypeStruct((b, s, w) if lay == "row" else (b, w, s), dt) for _, lay, w, dt in outs]
    tab = pl.BlockSpec((tm, V7X_LANES), lambda bi, i: (i, 0))
    vec = pl.BlockSpec((1, 1, d), lambda bi, i: (bi, 0, 0))
    res = pl.pallas_call(
        functools.partial(_inproj_kernel, is_ctx=is_ctx),
        grid=(b, s // tm),
        in_specs=[pl.BlockSpec((1, tm, d), lambda bi, i: (bi, i, 0)), vec, vec,
                  _layer_spec(w_in.shape, layer), tab, tab, tab,
                  _const_spec(gq.shape), _const_spec(gk.shape), _const_spec(bd128.shape)],
        out_specs=out_specs,
        out_shape=out_shape,
        compiler_params=_cparams(("parallel", "arbitrary")),
        name="inproj_ctx" if is_ctx else "inproj",
    )(h, scale, shift, w_in, rc, ra, rb, gq, gk, bd128)
    return {name: r for (name, _, _, _), r in zip(outs, res)}


def _attend_many(items):
    scores, maxes = [], []
    for qs, parts, sink_col in items:
        ss, m = [], sink_col
        for k, _, bias in parts:
            s = _dot_nt(qs, k)
            if bias is not None:
                s = s + bias
            ss.append(s)
            smax = jnp.max(s, axis=1, keepdims=True)
            m = smax if m is None else jnp.maximum(m, smax)
        scores.append(ss)
        maxes.append(m)
    probs, dens = [], []
    for (qs, parts, sink_col), ss, m in zip(items, scores, maxes):
        den = None if sink_col is None else jnp.exp(sink_col - m)
        ps = []
        for s in ss:
            p = jnp.exp(s - m)
            psum = jnp.sum(p, axis=1, keepdims=True)
            den = psum if den is None else den + psum
            ps.append(p.astype(BF16))
        probs.append(ps)
        dens.append(den)
    outs = []
    for (qs, parts, _), ps, den in zip(items, probs, dens):
        out = None
        for p, (_, v, _) in zip(ps, parts):
            pv = _dot(p, v)
            out = pv if out is None else out + pv
        outs.append(out * (1.0 / den))
    return outs


def _window_kernel(sink_ref, q_ref, *refs, n_pairs, n_blocks):
    n_kblk = 2 * n_pairs + 2
    k_refs, vt_refs = refs[:n_kblk], refs[n_kblk:2 * n_kblk]
    kc_ref, vtc_ref, o_ref = refs[2 * n_kblk:]
    step = pl.program_id(1)
    tq = 2 * BLOCK
    first_blk = step * 2 * n_pairs - 1
    ones_blk = jnp.ones((ONES_ROWS, BLOCK), BF16)
    ones_ctx = jnp.ones((ONES_ROWS, kc_ref.shape[1]), BF16)
    ki = lax.broadcasted_iota(jnp.int32, (BLOCK, tq), 0)
    qi = lax.broadcasted_iota(jnp.int32, (BLOCK, tq), 1)
    lane = lax.broadcasted_iota(jnp.int32, (1, 2 * tq), 1)
    zeros_q = jnp.zeros((HEAD_DIM, tq), BF16)

    def band_bias(jj, j):
        ok = jnp.abs((1 - jj) * BLOCK + qi - ki) <= WIN
        if j == 0:
            ok = ok & (first_blk >= 0)
        elif j == n_kblk - 1:
            ok = ok & (first_blk + j < n_blocks)
        bias = jnp.where(ok, 0.0, NEG_INF).astype(F32)
        return jnp.concatenate([bias, bias], axis=1)

    items = [(p, kv) for p in range(n_pairs) for kv in range(N_KV)]
    biases = {(p, jj): band_bias(jj, 2 * p + jj) for p in range(n_pairs) for jj in range(4)}

    scores, maxes = {}, {}
    for p, kv in items:
        qt = q_ref[0, 2 * HEAD_DIM * kv:2 * HEAD_DIM * (kv + 1), p * tq:(p + 1) * tq]

        def padded(g):
            qg = qt[g * HEAD_DIM:(g + 1) * HEAD_DIM, :]
            return jnp.concatenate([qg, zeros_q] if kv == 0 else [zeros_q, qg], axis=0)

        rhs = jnp.concatenate([padded(0), padded(1)], axis=1)
        ss = [_dot(k_refs[2 * p + jj][0], rhs) + biases[(p, jj)] for jj in range(4)]
        ss.append(_dot(kc_ref[0], rhs))
        sink2 = jnp.where(lane < tq, sink_ref[0, 2 * kv], sink_ref[0, 2 * kv + 1]) * LOG2_E
        m = sink2
        for s in ss:
            m = jnp.maximum(m, jnp.max(s, axis=0, keepdims=True))
        scores[(p, kv)], maxes[(p, kv)] = ss, (m, sink2)
    probs = {}
    for it in items:
        m = maxes[it][0]
        probs[it] = [jnp.exp2(s - m).astype(BF16) for s in scores.pop(it)]
    for p, kv in items:
        ps = probs.pop((p, kv))
        acc = None
        for jj in range(4):
            vt = vt_refs[2 * p + jj][0, kv * HEAD_DIM:(kv + 1) * HEAD_DIM, :]
            pv = _dot(jnp.concatenate([vt, ones_blk], axis=0), ps[jj])
            acc = pv if acc is None else acc + pv
        vtc = vtc_ref[0, kv * HEAD_DIM:(kv + 1) * HEAD_DIM, :]
        acc = acc + _dot(jnp.concatenate([vtc, ones_ctx], axis=0), ps[4])
        m, sink2 = maxes[(p, kv)]
        den = acc[HEAD_DIM:HEAD_DIM + 1] + jnp.exp2(sink2 - m)
        o = acc[0:HEAD_DIM] * (1.0 / den)
        ot = jnp.concatenate([o[:, :tq], o[:, tq:]], axis=0)
        o_ref[0, p * tq:(p + 1) * tq, 2 * HEAD_DIM * kv:2 * HEAD_DIM * (kv + 1)] = ot.T.astype(BF16)


def _window_attn(qt, k, vt, kc, vtc, sink, *, n_pairs):
    b, w, s = qt.shape
    t = kc.shape[1]
    rows = n_pairs * 2 * BLOCK
    n_blocks = s // BLOCK
    n_kblk = 2 * n_pairs + 2
    assert s % rows == 0

    def blk_idx(i, j):
        return jnp.clip(i * 2 * n_pairs - 1 + j, 0, n_blocks - 1)

    k_specs = [pl.BlockSpec((1, BLOCK, KV_W), functools.partial(lambda bi, i, j: (bi, blk_idx(i, j), 0), j=j))
               for j in range(n_kblk)]
    vt_specs = [pl.BlockSpec((1, KV_W, BLOCK), functools.partial(lambda bi, i, j: (bi, 0, blk_idx(i, j)), j=j))
                for j in range(n_kblk)]
    return pl.pallas_call(
        functools.partial(_window_kernel, n_pairs=n_pairs, n_blocks=n_blocks),
        grid=(b, s // rows),
        in_specs=[pl.BlockSpec(memory_space=pltpu.SMEM),
                  pl.BlockSpec((1, w, rows), lambda bi, i: (bi, 0, i))] + k_specs + vt_specs +
                 [pl.BlockSpec((1, t, KV_W), lambda bi, i: (bi, 0, 0)),
                  pl.BlockSpec((1, KV_W, t), lambda bi, i: (bi, 0, 0))],
        out_specs=pl.BlockSpec((1, rows, w), lambda bi, i: (bi, i, 0)),
        out_shape=jax.ShapeDtypeStruct((b, s, w), BF16),
        compiler_params=_cparams(("parallel", "arbitrary")),
        name="window_attn",
    )(sink.reshape(1, N_HEADS).astype(F32), qt, *([k] * n_kblk), *([vt] * n_kblk), kc, vtc)


def _na_kernel(q_ref, k_ref, v_ref, kc_ref, vc_ref, bias_ref, o_ref, *, rows, n_rows):
    step = pl.program_id(1)
    span = NA_ROWS * GRID_W
    ctx_part = (kc_ref[0], vc_ref[0], None)
    items = []
    for j in range(n_rows):
        r = step * n_rows + j
        r0 = jnp.clip(r - NA_ROWS // 2, 0, rows - NA_ROWS)
        start = pl.multiple_of(r0 * GRID_W, GRID_W)
        kw = k_ref[0, pl.ds(start, span), :]
        vw = v_ref[0, pl.ds(start, span), :]
        qs = _stack_heads(q_ref[0, j * GRID_W:(j + 1) * GRID_W, :])
        off0 = r0 - r + (NA_ROWS - 1)
        bias = jnp.concatenate([bias_ref[off0 + 2 * i] for i in range(NA_ROWS // 2)], axis=1)
        items.append((qs, [(kw, vw, bias), ctx_part], None))
    for j, o in enumerate(_attend_many(items)):
        o_ref[0, j * GRID_W:(j + 1) * GRID_W, :] = _unstack_heads(o, GRID_W).astype(BF16)


def _na_bias_table(rpb):
    n_h, n_ro, n_co = rpb.shape
    q = np.arange(GRID_W)
    kc = np.arange(GRID_W)
    coff = np.clip(kc[None, :] - q[:, None], -(NA_COLS - 1), NA_COLS - 1) + (NA_COLS - 1)
    c0 = np.clip(q - NA_COLS // 2, 0, GRID_W - NA_COLS)
    ok = (kc[None, :] >= c0[:, None]) & (kc[None, :] < c0[:, None] + NA_COLS)
    onehot = np.zeros((n_co, GRID_W, GRID_W), np.float32)
    onehot[coff, q[:, None], kc[None, :]] = 1.0
    tab = jnp.einsum("hrc,cqk->hrqk", rpb.astype(F32), jnp.asarray(onehot), precision=lax.Precision.HIGHEST)
    tab = jnp.where(ok[None, None], tab, NEG_INF)
    pairs = jnp.concatenate([tab[:, :-1], tab[:, 1:]], axis=-1)
    return pairs.transpose(1, 0, 2, 3).reshape(n_ro - 1, n_h * GRID_W, 2 * GRID_W)


def _na_attn(q, k, v, kc, vc, bias_tab, *, n_rows):
    b, s, w = q.shape
    t = kc.shape[1]
    rows = s // GRID_W
    assert s % GRID_W == 0 and rows >= NA_ROWS and rows % n_rows == 0
    full = lambda n: pl.BlockSpec((1, n, w), lambda bi, i: (bi, 0, 0))
    blk = n_rows * GRID_W
    return pl.pallas_call(
        functools.partial(_na_kernel, rows=rows, n_rows=n_rows),
        grid=(b, rows // n_rows),
        in_specs=[pl.BlockSpec((1, blk, w), lambda bi, r: (bi, r, 0)),
                  full(s), full(s), full(t), full(t), _const_spec(bias_tab.shape)],
        out_specs=pl.BlockSpec((1, blk, w), lambda bi, r: (bi, r, 0)),
        out_shape=jax.ShapeDtypeStruct((b, s, w), BF16),
        compiler_params=_cparams(("parallel", "arbitrary")),
        name="na_attn",
    )(q, k, v, kc, vc, bias_tab)


def _ctx_attn_kernel(sink_ref, qa_ref, ka_ref, va_ref, qb_ref, kb_ref, vb_ref, qd_ref, kd_ref, vd_ref,
                     oa_ref, ob_ref, od_ref):
    t = qa_ref.shape[1]
    sink_col = jnp.concatenate([jnp.full((t, 1), sink_ref[0, h], F32) for h in range(N_HEADS)], axis=0)
    groups = ((qa_ref, ka_ref, va_ref, sink_col), (qb_ref, kb_ref, vb_ref, None), (qd_ref, kd_ref, vd_ref, None))
    outs = _attend_many([(_stack_heads(q_ref[0]), [(k_ref[0], v_ref[0], None)], sink)
                         for q_ref, k_ref, v_ref, sink in groups])
    for o_ref, o in zip((oa_ref, ob_ref, od_ref), outs):
        o_ref[0] = _unstack_heads(o, t).astype(BF16)


def _ctx_attn(sink, qa, ka, va, qb, kb, vb, qd, kd, vd):
    b, t, w = qa.shape
    blk = pl.BlockSpec((1, t, w), lambda bi: (bi, 0, 0))
    out = jax.ShapeDtypeStruct((b, t, w), BF16)
    return pl.pallas_call(
        _ctx_attn_kernel,
        grid=(b,),
        in_specs=[pl.BlockSpec(memory_space=pltpu.SMEM)] + [blk] * 9,
        out_specs=[blk] * 3,
        out_shape=[out] * 3,
        compiler_params=_cparams(("arbitrary",)),
        name="ctx_attn",
    )(sink.reshape(1, N_HEADS).astype(F32), qa, ka, va, qb, kb, vb, qd, kd, vd)


def _global_kernel(q_ref, k_ref, vt_ref, kc_ref, vtc_ref, o_ref, *, tq, kt):
    kv = pl.program_id(1)
    n_blk = q_ref.shape[2] // tq
    sel = (lax.broadcasted_iota(jnp.int32, (2 * HEAD_DIM, tq), 0) // HEAD_DIM) == kv

    def block_rhs(blk):
        qt = q_ref[0, :, blk * tq:(blk + 1) * tq]
        zero = jnp.zeros_like(qt)

        def padded(g):
            qg = qt[g * HEAD_DIM:(g + 1) * HEAD_DIM, :]
            return jnp.where(sel, jnp.concatenate([qg, qg], axis=0), zero)

        return jnp.concatenate([padded(0), padded(1)], axis=1)

    rhs = [block_rhs(blk) for blk in range(n_blk)]
    n_lat = k_ref.shape[1] // kt
    n_tiles = n_lat + kc_ref.shape[1] // kt
    ones = jnp.ones((ONES_ROWS, kt), BF16)

    def k_tile(i):
        return k_ref[0, i * kt:(i + 1) * kt, :] if i < n_lat else kc_ref[0, (i - n_lat) * kt:(i - n_lat + 1) * kt, :]

    def vt_tile(i):
        vt = (vt_ref[0, :, i * kt:(i + 1) * kt] if i < n_lat
              else vtc_ref[0, :, (i - n_lat) * kt:(i - n_lat + 1) * kt])
        return jnp.concatenate([vt, ones], axis=0)

    scores, tile_max, run_max, probs = {}, {}, {}, {}
    acc = [None] * n_blk
    n_items = n_blk * n_tiles
    for i in range(n_items + 2):
        if i < n_items:
            blk, c = divmod(i, n_tiles)
            scores[i] = _dot(k_tile(c), rhs[blk])
            tile_max[i] = jnp.max(scores[i], axis=0, keepdims=True)
        j = i - 1
        if 0 <= j < n_items:
            first = j % n_tiles == 0
            run_max[j] = tile_max[j] if first else jnp.maximum(run_max[j - 1], tile_max[j])
            probs[j] = jnp.exp2(scores.pop(j) - run_max[j]).astype(BF16)
        j = i - 2
        if 0 <= j < n_items:
            blk, c = divmod(j, n_tiles)
            pv = _dot(vt_tile(c), probs.pop(j))
            acc[blk] = pv if c == 0 else acc[blk] * jnp.exp2(run_max[j - 1] - run_max[j]) + pv
            if c == n_tiles - 1:
                o = acc[blk][0:HEAD_DIM] * (1.0 / acc[blk][HEAD_DIM:HEAD_DIM + 1])
                ot = jnp.concatenate([o[:, :tq], o[:, tq:]], axis=0)
                o_ref[0, blk * tq:(blk + 1) * tq, :] = ot.T.astype(BF16)


def _global_attn(qt, k, vt, kc, vtc, *, tq, kt, n_blk):
    b, w, s = qt.shape
    t = kc.shape[1]
    rows = n_blk * tq
    assert s % rows == 0 and s % kt == 0 and t % kt == 0
    return pl.pallas_call(
        functools.partial(_global_kernel, tq=tq, kt=kt),
        grid=(b, N_KV, s // rows),
        in_specs=[pl.BlockSpec((1, 2 * HEAD_DIM, rows), lambda bi, kv, i: (bi, kv, i)),
                  pl.BlockSpec((1, s, KV_W), lambda bi, kv, i: (bi, 0, 0)),
                  pl.BlockSpec((1, HEAD_DIM, s), lambda bi, kv, i: (bi, kv, 0)),
                  pl.BlockSpec((1, t, KV_W), lambda bi, kv, i: (bi, 0, 0)),
                  pl.BlockSpec((1, HEAD_DIM, t), lambda bi, kv, i: (bi, kv, 0))],
        out_specs=pl.BlockSpec((1, rows, 2 * HEAD_DIM), lambda bi, kv, i: (bi, i, kv)),
        out_shape=jax.ShapeDtypeStruct((b, s, w), BF16),
        compiler_params=_cparams(("parallel", "arbitrary", "arbitrary")),
        name="global_attn",
    )(qt, k, vt, kc, vtc)


def _hgrn_constants():
    n = HGRN_L
    tm = np.zeros((2, 2 * n, n), np.float32)
    mk = np.zeros((2, HGRN_LEVELS + 1, n, n), np.float32)
    for d in range(2):
        pos = np.arange(n) if d == 0 else n - 1 - np.arange(n)
        p, q = pos[:, None], pos[None, :]
        for li in range(HGRN_LEVELS):
            h = n >> (li + 1)
            same_blk = (p // (2 * h)) == (q // (2 * h))
            up_p, up_q = (p % (2 * h)) >= h, (q % (2 * h)) >= h
            mk[d, li] = same_blk & up_p & ~up_q
            if h == 2:
                same_half = same_blk & (up_p == up_q)
                tm[d, 0:n] = np.where(up_p, same_half & (q <= p), same_half & (q > p))
        mk[d, HGRN_LEVELS] = np.eye(n)
        tm[d, n:2 * n] = q <= p
    mk = np.tile(mk, (1, 1, N_HEADS, 1))
    return jnp.asarray(tm, BF16), jnp.asarray(mk, F32)


def _hgrn_seq_kernel(qf_ref, vf_ref, zf_ref, qb_ref, vb_ref, zb_ref, lb_ref, t_ref, mk_ref, bd_ref, *rest,
                     g_chunks, has_init, emit_state):
    rest = list(rest)
    init_ref = rest.pop(0) if has_init else None
    of_ref, ob_ref = rest.pop(0), rest.pop(0)
    so_ref = rest.pop(0) if emit_state else None
    st_ref = rest.pop(0)
    step = pl.program_id(1)
    n = HGRN_L

    @pl.when(step == 0)
    def _():
        st_ref[...] = init_ref[0] if has_init else jnp.zeros_like(st_ref)

    ins = ((qf_ref, vf_ref, zf_ref), (qb_ref, vb_ref, zb_ref))
    outs = (of_ref, ob_ref)
    items = [(d, (g if d == 0 else g_chunks - 1 - g) * n) for g in range(g_chunks) for d in (0, 1)]

    work = []
    for d, r0 in items:
        q = ins[d][0][0, r0:r0 + n, :]
        v = ins[d][1][0, r0:r0 + n, :]
        z = ins[d][2][0, r0:r0 + n, :]
        lb = lb_ref[d]
        f = lb + (1.0 - lb) * jax.nn.sigmoid(z)
        g = jnp.log(f)
        g_hi = g.astype(BF16)
        g_lo = (g - g_hi.astype(F32)).astype(BF16)
        e = _dot(t_ref[d], g_hi) + _dot(t_ref[d], g_lo)
        work.append(dict(qq=_silu(q), kk=1.0 - f, f=f, v=v, e2=e[0:n], b=e[n:2 * n],
                         a=jnp.zeros((N_HEADS * n, n), F32)))

    row = lax.broadcasted_iota(jnp.int32, (n, GROUP_W), 0)

    def level_scale(w, d, h):
        if h == 2:
            return jnp.exp(w["e2"])
        b = w["b"]
        blocks = []
        for j in range(n // (2 * h)):
            m = 2 * h * j + (h - 1 if d == 0 else h)
            blocks.append(jnp.broadcast_to(b[m:m + 1, :], (2 * h, GROUP_W)))
        bm = blocks[0] if len(blocks) == 1 else jnp.concatenate(blocks, axis=0)
        later = ((row // h) % 2) == (1 if d == 0 else 0)
        return jnp.exp(jnp.where(later, b - bm, bm - b))

    for li in range(HGRN_LEVELS + 1):
        h = n >> (li + 1)
        for (d, _), w in zip(items, work):
            if h >= 2:
                sc = level_scale(w, d, h)
                qs, ks = w["qq"] * sc, w["kk"] * sc
            elif h == 1:
                qs, ks = w["qq"] * w["f"], w["kk"]
            else:
                qs, ks = w["qq"], w["kk"]
            w["a"] = w["a"] + _dot_nt(_stack_heads(qs.astype(BF16)), ks.astype(BF16)) * mk_ref[d, li]

    for (d, _), w in zip(items, work):
        b_incl = w["b"]
        b_tot = b_incl[n - 1:n] if d == 0 else b_incl[0:1]
        vb = w["v"].astype(BF16)
        w["q_hat"] = (w["qq"] * jnp.exp(b_incl)).astype(BF16)
        k_hat = (w["kk"] * jnp.exp(b_tot - b_incl)).astype(BF16)
        w["decay"] = jnp.exp(b_tot)
        w["o"] = _unstack_heads(_dot(w["a"].astype(BF16), vb), n)
        w["upd"] = _dot(w["v"].T.astype(BF16), k_hat) * bd_ref[...]

    st = [st_ref[0], st_ref[1]]
    for (d, r0), w in zip(items, work):
        outs[d][0, r0:r0 + n, :] = w["o"] + _dot_nt(w["q_hat"], st[d].astype(BF16))
        st[d] = st[d] * w["decay"] + w["upd"]
    st_ref[0] = st[0]
    st_ref[1] = st[1]
    if emit_state:
        so_ref[0, 0] = st[0]
        so_ref[0, 1] = st[1]


def _hgrn_seq(c_raw, lb, consts, init_state, *, g_chunks, emit_state):
    tmat, masks, bd = consts
    b, s, _ = c_raw.shape
    rows = g_chunks * HGRN_L
    assert s % rows == 0
    n_steps = s // rows
    has_init = init_state is not None
    fwd = lambda col: pl.BlockSpec((1, rows, GROUP_W), lambda bi, st: (bi, st, col))
    bwd = lambda col: pl.BlockSpec((1, rows, GROUP_W), lambda bi, st: (bi, n_steps - 1 - st, col))
    state_spec = pl.BlockSpec((1, 2, GROUP_W, GROUP_W), lambda bi, st: (bi, 0, 0, 0))
    in_specs = [fwd(0), fwd(1), fwd(2), bwd(0), bwd(1), bwd(3),
                _const_spec((2, 1, GROUP_W)), _const_spec(tmat.shape), _const_spec(masks.shape),
                _const_spec(bd.shape)]
    args = [c_raw] * 6 + [lb.reshape(2, 1, GROUP_W), tmat, masks, bd]
    if has_init:
        in_specs.append(state_spec)
        args.append(init_state)
    out_specs = [fwd(0), bwd(0)]
    out_shape = [jax.ShapeDtypeStruct((b, s, GROUP_W), F32)] * 2
    if emit_state:
        out_specs.append(state_spec)
        out_shape.append(jax.ShapeDtypeStruct((b, 2, GROUP_W, GROUP_W), F32))
    return pl.pallas_call(
        functools.partial(_hgrn_seq_kernel, g_chunks=g_chunks, has_init=has_init, emit_state=emit_state),
        grid=(b, n_steps),
        in_specs=in_specs,
        out_specs=out_specs,
        out_shape=out_shape,
        scratch_shapes=[pltpu.VMEM((2, GROUP_W, GROUP_W), F32)],
        compiler_params=_cparams(("parallel", "arbitrary")),
        name="hgrn_ctx" if emit_state else "hgrn",
    )(*args)


def _outproj_kernel(x_ref, oa_ref, ob_ref, of_ref, obw_ref, gate_ref, od_ref, w_ref, onorm_ref, bd_ref,
                    ga_ref, lng_ref, lnb_ref, o_ref, *, alpha):
    oc = of_ref[0] + obw_ref[0]
    y = oc * lax.rsqrt(_head_mean_sq(oc, bd_ref[...]) + EPS) * onorm_ref[...]
    y = y * _silu(gate_ref[0])
    mix = jnp.concatenate([oa_ref[0], ob_ref[0], y.astype(BF16), od_ref[0]], axis=1)
    z = alpha * x_ref[0] + ga_ref[0] * _dot(mix, w_ref[0])
    o_ref[0] = _layer_norm(z, lng_ref[...], lnb_ref[...])


def _outproj(x, oa, ob, oc_f, oc_b, c_raw, od, w_out, layer, onorm, bd256, gate_a, ln_g, ln_b, *, alpha, tm):
    b, s, d = x.shape
    assert s % tm == 0
    grp = pl.BlockSpec((1, tm, GROUP_W), lambda bi, i: (bi, i, 0))
    vec = pl.BlockSpec((1, 1, d), lambda bi, i: (bi, 0, 0))
    return pl.pallas_call(
        functools.partial(_outproj_kernel, alpha=alpha),
        grid=(b, s // tm),
        in_specs=[pl.BlockSpec((1, tm, d), lambda bi, i: (bi, i, 0)), grp, grp, grp, grp,
                  pl.BlockSpec((1, tm, GROUP_W), lambda bi, i: (bi, i, 4)),
                  grp, _layer_spec(w_out.shape, layer), _const_spec(onorm.shape), _const_spec(bd256.shape),
                  vec, _const_spec(ln_g.shape), _const_spec(ln_b.shape)],
        out_specs=pl.BlockSpec((1, tm, d), lambda bi, i: (bi, i, 0)),
        out_shape=jax.ShapeDtypeStruct((b, s, d), F32),
        compiler_params=_cparams(("parallel", "arbitrary")),
        name="outproj",
    )(x, oa, ob, oc_f, oc_b, c_raw, od, w_out, onorm, bd256, gate_a, ln_g, ln_b)


def _ffn_kernel(h_ref, hp_ref, hn_ref, sc_ref, sh_ref, gt_ref, wup_ref, cw_ref, cb_ref, wdn_ref,
                lng_ref, lnb_ref, o_ref, g_scr, *, alpha, d_ff, cwid, n_tiles):
    i = pl.program_id(1)
    tm = h_ref.shape[1]
    sc = 1.0 + sc_ref[0]
    sh = sh_ref[0]
    h = h_ref[0]
    halo = jnp.concatenate([hp_ref[0], hn_ref[0]], axis=0)
    lhs = jnp.concatenate([(h * sc + sh).astype(BF16), (halo * sc + sh).astype(BF16)], axis=0)
    has_prev = (i > 0).astype(F32)
    has_next = (i < n_tiles - 1).astype(F32)
    row = lax.broadcasted_iota(jnp.int32, (tm, cwid), 0)
    first, last = row == 0, row == tm - 1

    def conv_cols(c0):
        u = _dot(lhs, wup_ref[0, :, c0:c0 + cwid])
        um = u[:tm]
        u_prev = u[tm + 7:tm + 8] * has_prev
        u_next = u[tm + 8:tm + 9] * has_next
        dn = jnp.where(first, u_prev, pltpu.roll(um, 1, 0))
        up = jnp.where(last, u_next, pltpu.roll(um, tm - 1, 0))
        w = cw_ref[:, c0:c0 + cwid]
        return dn * w[0:1] + um * w[1:2] + up * w[2:3] + cb_ref[:, c0:c0 + cwid]

    for j in range(d_ff // cwid):
        a = conv_cols(j * cwid)
        v = conv_cols(d_ff + j * cwid)
        g_scr[:, j * cwid:(j + 1) * cwid] = (_silu(a) * v).astype(BF16)
    z = alpha * h + gt_ref[0] * _dot(g_scr[...], wdn_ref[0])
    o_ref[0] = _layer_norm(z, lng_ref[...], lnb_ref[...])


def _ffn(h, scale, shift, gate, w_up, conv_w, conv_b, w_down, layer, ln_g, ln_b, *, alpha, tm, cwid):
    b, s, d = h.shape
    d_ff = w_down.shape[1]
    assert s % tm == 0 and tm % 16 == 0 and d_ff % cwid == 0
    n_tiles = s // tm
    r8 = tm // 8
    vec = pl.BlockSpec((1, 1, d), lambda bi, i: (bi, 0, 0))
    return pl.pallas_call(
        functools.partial(_ffn_kernel, alpha=alpha, d_ff=d_ff, cwid=cwid, n_tiles=n_tiles),
        grid=(b, n_tiles),
        in_specs=[pl.BlockSpec((1, tm, d), lambda bi, i: (bi, i, 0)),
                  pl.BlockSpec((1, 8, d), lambda bi, i: (bi, jnp.maximum(i * r8 - 1, 0), 0)),
                  pl.BlockSpec((1, 8, d), lambda bi, i: (bi, jnp.minimum((i + 1) * r8, s // 8 - 1), 0)),
                  vec, vec, vec,
                  _layer_spec(w_up.shape, layer), _const_spec(conv_w.shape), _const_spec(conv_b.shape),
                  _layer_spec(w_down.shape, layer), _const_spec(ln_g.shape), _const_spec(ln_b.shape)],
        out_specs=pl.BlockSpec((1, tm, d), lambda bi, i: (bi, i, 0)),
        out_shape=jax.ShapeDtypeStruct((b, s, d), F32),
        scratch_shapes=[pltpu.VMEM((tm, d_ff), BF16)],
        compiler_params=_cparams(("parallel", "arbitrary")),
        name="conv_ffn",
    )(h, h, h, scale, shift, gate, w_up, conv_w, conv_b, w_down, ln_g, ln_b)


def _rope_tables(n, with_rotation):
    lane = np.arange(V7X_LANES)
    if not with_rotation:
        one = jnp.ones((n, V7X_LANES), F32)
        zero = jnp.zeros((n, V7X_LANES), F32)
        return one, zero, zero
    t = np.arange(n)
    pos = np.stack([t // GRID_W, t % GRID_W], -1).astype(np.float64)
    n_freq = HEAD_DIM // 4
    inv = ROPE_BASE ** (-np.arange(n_freq, dtype=np.float64) / n_freq)
    ang = pos[:, :, None] * inv
    axis = (lane % HEAD_DIM) // (HEAD_DIM // 2)
    freq = lane % n_freq
    ang_l = ang[:, axis, freq]
    first = (lane % (2 * n_freq)) < n_freq
    cos, sin = np.cos(ang_l), np.sin(ang_l)
    tabs = (cos, np.where(first, -sin, 0.0), np.where(first, 0.0, sin))
    return tuple(jnp.asarray(a, F32) for a in tabs)


def _block_diag_mean(width):
    lane = np.arange(width)
    same = (lane[:, None] // HEAD_DIM) == (lane[None, :] // HEAD_DIM)
    return jnp.asarray(same / HEAD_DIM, BF16)


def kernel(x, c, ctx, c_ctx, w_mod, b_mod, w_in, w_out, ln_g, ln_b, attn_sink, qk_norm, hgrn_lb_logits,
           hgrn_onorm, na_rpb, ffn_w_up, ffn_conv_w, ffn_conv_b, ffn_w_down):
    depth = w_mod.shape[0]
    b, s, d = x.shape
    t = ctx.shape[1]
    alpha = (2 * depth) ** 0.25

    mods = _modulation(c, c_ctx, w_mod, b_mod)
    rope_lat = _rope_tables(s, True)
    rope_ctx = _rope_tables(t, False)
    bd128, bd256 = _block_diag_mean(KV_W), _block_diag_mean(GROUP_W)
    lane = np.arange(GROUP_W)
    head_bd = jnp.asarray((lane[:, None] // HEAD_DIM) == (lane[None, :] // HEAD_DIM), F32)
    hgrn_consts = _hgrn_constants() + (head_bd,)
    p_lb = jax.nn.softmax(hgrn_lb_logits.astype(F32), axis=1)
    lb_all = jnp.cumsum(p_lb, axis=1) - p_lb[:, :1]
    w_in_b, w_out_b = w_in.astype(BF16), w_out.astype(BF16)
    w_up_b, w_dn_b = ffn_w_up.astype(BF16), ffn_w_down.astype(BF16)

    h_x, h_c = x, ctx
    for l in range(depth):
        need_ctx = l < depth - 1
        mod = mods[l]
        sa, ca, ga, sf, cf, gf = (mod[:, i * d:(i + 1) * d] for i in range(6))
        lat = lambda a: a[:b, None, :]
        con = lambda a: jnp.broadcast_to(a[b][None, None, :], (b, 1, d))
        gq = jnp.tile(qk_norm[l, 0], 2)[None, :]
        gk = jnp.tile(qk_norm[l, 1], 2)[None, :]

        px = _inproj(h_x, lat(sa), lat(ca), w_in_b, l, rope_lat, gq, gk, bd128, is_ctx=False, tm=512)
        pc = _inproj(h_c, con(sa), con(ca), w_in_b, l, rope_ctx, gq, gk, bd128, is_ctx=True, tm=t)
        c_raw, c_raw_c = px["c"], pc["c"]

        o_a = _window_attn(px["qa"], px["ka"], px["vat"], pc["ka"], pc["vat"], attn_sink[l],
                           n_pairs=WINDOW_PAIRS_PER_STEP)

        o_b = _global_attn(px["qb"], px["kb"], px["vbt"], pc["kb"], pc["vbt"], tq=256, kt=256,
                           n_blk=GLOBAL_BLOCKS_PER_STEP)

        ocf_c, ocb_c, ctx_state = _hgrn_seq(c_raw_c, lb_all[:, l], hgrn_consts, None,
                                            g_chunks=t // HGRN_L, emit_state=True)
        oc_f, oc_b = _hgrn_seq(c_raw, lb_all[:, l], hgrn_consts, ctx_state,
                               g_chunks=HGRN_CHUNKS_PER_STEP, emit_state=False)

        o_d = _na_attn(px["qd"], px["kd"], px["vd"], pc["kd"], pc["vd"], _na_bias_table(na_rpb[l]),
                       n_rows=NA_ROWS_PER_STEP)

        onorm = hgrn_onorm[l][None, :]
        lng0, lnb0 = ln_g[l, 0][None, :], ln_b[l, 0][None, :]
        lng1, lnb1 = ln_g[l, 1][None, :], ln_b[l, 1][None, :]
        cw, cb = ffn_conv_w[l], ffn_conv_b[l][None, :]
        h_x = _outproj(h_x, o_a, o_b, oc_f, oc_b, c_raw, o_d, w_out_b, l, onorm, bd256, lat(ga), lng0, lnb0,
                       alpha=alpha, tm=1024)
        h_x = _ffn(h_x, lat(cf), lat(sf), lat(gf), w_up_b, cw, cb, w_dn_b, l, lng1, lnb1,
                   alpha=alpha, tm=512, cwid=256)
        if need_ctx:
            oa_c, ob_c, od_c = _ctx_attn(attn_sink[l], pc["qa"], pc["kax"], pc["vax"], pc["qb"], pc["kbx"],
                                         pc["vbx"], pc["qd"], pc["kd"], pc["vd"])
            h_c = _outproj(h_c, oa_c, ob_c, ocf_c, ocb_c, c_raw_c, od_c, w_out_b, l, onorm, bd256, con(ga),
                           lng0, lnb0, alpha=alpha, tm=t)
            h_c = _ffn(h_c, con(cf), con(sf), con(gf), w_up_b, cw, cb, w_dn_b, l, lng1, lnb1,
                       alpha=alpha, tm=t, cwid=256)
    return h_x
```

```python
import functools

import numpy as np
import jax
import jax.numpy as jnp
from jax import lax
from jax.experimental import pallas as pl
from jax.experimental.pallas import tpu as pltpu

F32 = jnp.float32
BF16 = jnp.bfloat16

HEAD_DIM = 64
N_HEADS = 4
N_KV = 2
GROUP_W = N_HEADS * HEAD_DIM
KV_W = N_KV * HEAD_DIM
GRID_W = 64
WIN = 128
BLOCK = 128
NA_ROWS = 8
NA_COLS = 16
ROPE_BASE = 10000.0
EPS = 1e-6
HGRN_L = 64
HGRN_LEVELS = 6
HGRN_CHUNKS_PER_STEP = 4
WINDOW_PAIRS_PER_STEP = 4
NA_ROWS_PER_STEP = 4
GLOBAL_BLOCKS_PER_STEP = 2
ONES_ROWS = 16

V7X_LANES = 128
V7X_VMEM_LIMIT_BYTES = 56 * 1024 * 1024

COL_QA, COL_KA, COL_VA = 0, 256, 384
COL_QB, COL_KB, COL_VB = 512, 768, 896
COL_C = 1024
COL_QD, COL_KD, COL_VD = 2304, 2560, 2816
IN_WIDTH = 3072
C_WIDTH = 5 * GROUP_W

NEG_INF = float("-inf")
LOG2_E = 1.4426950408889634


def _cparams(sem):
    return pltpu.CompilerParams(dimension_semantics=sem, vmem_limit_bytes=V7X_VMEM_LIMIT_BYTES)


def _const_spec(shape):
    nd = len(shape)
    return pl.BlockSpec(shape, lambda *_: (0,) * nd, pipeline_mode=pl.Buffered(1))


def _layer_spec(shape, layer):
    nd = len(shape)
    return pl.BlockSpec((1,) + tuple(shape[1:]), lambda *_: (layer,) + (0,) * (nd - 1),
                        pipeline_mode=pl.Buffered(1))


def _silu(x):
    return x * jax.nn.sigmoid(x)


def _dot(a, b):
    return jnp.dot(a, b, preferred_element_type=F32)


def _dot_nt(a, b):
    return lax.dot_general(a, b, (((1,), (1,)), ((), ())), preferred_element_type=F32)


def _head_mask(shape, h):
    lane = lax.broadcasted_iota(jnp.int32, shape, len(shape) - 1)
    return (lane // HEAD_DIM) == h


def _stack_heads(q):
    zero = jnp.zeros_like(q)
    return jnp.concatenate([jnp.where(_head_mask(q.shape, h), q, zero) for h in range(N_HEADS)], axis=0)


def _unstack_heads(o, r):
    out = jnp.zeros((r, o.shape[1]), o.dtype)
    for h in range(N_HEADS):
        blk = o[h * r:(h + 1) * r]
        out = jnp.where(_head_mask(blk.shape, h), blk, out)
    return out


def _head_mean_sq(x, bd):
    sq = x * x
    hi = sq.astype(BF16)
    lo = (sq - hi.astype(F32)).astype(BF16)
    return _dot(hi, bd) + _dot(lo, bd)


def _layer_norm(z, g, b):
    mu = jnp.mean(z, axis=-1, keepdims=True)
    zc = z - mu
    var = jnp.mean(zc * zc, axis=-1, keepdims=True)
    return zc * lax.rsqrt(var + EPS) * g + b


def _mod_kernel(s_ref, w_ref, b_ref, o_ref):
    s = _silu(s_ref[...])
    o_ref[0] = jnp.dot(s, w_ref[0], preferred_element_type=F32,
                       precision=lax.Precision.HIGHEST) + b_ref[0]


def _modulation(c, c_ctx, w_mod, b_mod):
    depth, d, d6 = w_mod.shape
    b = c.shape[0]
    rows = 8
    assert b < rows
    s_in = jnp.zeros((rows, d), F32).at[:b].set(c).at[b].set(c_ctx)
    tn = d6 // 4
    return pl.pallas_call(
        _mod_kernel,
        grid=(depth, d6 // tn),
        in_specs=[pl.BlockSpec((rows, d), lambda l, j: (0, 0)),
                  pl.BlockSpec((1, d, tn), lambda l, j: (l, 0, j)),
                  pl.BlockSpec((1, 1, tn), lambda l, j: (l, 0, j))],
        out_specs=pl.BlockSpec((1, rows, tn), lambda l, j: (l, 0, j)),
        out_shape=jax.ShapeDtypeStruct((depth, rows, d6), F32),
        compiler_params=_cparams(("arbitrary", "arbitrary")),
        name="modulation",
    )(s_in, w_mod, b_mod.reshape(depth, 1, d6))


def _dup_heads(x):
    r = pltpu.roll(x, HEAD_DIM, 1)
    lo = lax.broadcasted_iota(jnp.int32, x.shape, 1) < HEAD_DIM
    return jnp.concatenate([jnp.where(lo, x, r), jnp.where(lo, r, x)], axis=1)


def _inproj_outputs(is_ctx):
    gqa = lambda g: [("q" + g, "row" if is_ctx else "col", GROUP_W, BF16), ("k" + g, "row", KV_W, BF16),
                     ("v" + g + "t", "col", KV_W, BF16)]
    dup = lambda g: [("k" + g + "x", "row", GROUP_W, BF16), ("v" + g + "x", "row", GROUP_W, BF16)]
    outs = gqa("a") + gqa("b")
    if is_ctx:
        outs += dup("a") + dup("b")
    return outs + [("c", "row", C_WIDTH, F32), ("qd", "row", GROUP_W, BF16), ("kd", "row", GROUP_W, BF16),
                   ("vd", "row", GROUP_W, BF16)]


def _inproj_kernel(h_ref, sc_ref, sh_ref, w_ref, rc_ref, ra_ref, rb_ref, gq_ref, gk_ref, bd_ref,
                   *outs, is_ctx):
    o = {name: ref for (name, _, _, _), ref in zip(_inproj_outputs(is_ctx), outs)}
    hm = (h_ref[0] * (1.0 + sc_ref[0]) + sh_ref[0]).astype(BF16)
    rc, ra, rb = rc_ref[...], ra_ref[...], rb_ref[...]
    bd = bd_ref[...]
    q_scale = HEAD_DIM ** -0.5 * (1.0 if is_ctx else LOG2_E)

    def proj(c0, width):
        return _dot(hm, w_ref[0, :, c0:c0 + width])

    def rope(x):
        return x * rc + pltpu.roll(x, V7X_LANES - 16, 1) * ra + pltpu.roll(x, 16, 1) * rb

    def rms(x, gain):
        return x * lax.rsqrt(_head_mean_sq(x, bd) + EPS) * gain

    def gqa_group(g, c_q, c_k, c_v, q_fn, k_fn):
        xq = proj(c_q, GROUP_W)
        for j in range(2):
            sl = slice(j * V7X_LANES, (j + 1) * V7X_LANES)
            y = rope(q_fn(xq[:, sl])) * q_scale
            if is_ctx:
                o["q" + g][0, :, sl] = y.astype(BF16)
            else:
                o["q" + g][0, sl, :] = y.T.astype(BF16)
        k = rope(k_fn(proj(c_k, KV_W)))
        v = proj(c_v, KV_W)
        o["k" + g][0] = k.astype(BF16)
        o["v" + g + "t"][0] = v.T.astype(BF16)
        if is_ctx:
            o["k" + g + "x"][0] = _dup_heads(k).astype(BF16)
            o["v" + g + "x"][0] = _dup_heads(v).astype(BF16)

    gq, gk = gq_ref[...], gk_ref[...]
    gqa_group("a", COL_QA, COL_KA, COL_VA, lambda x: x, lambda x: x)
    gqa_group("b", COL_QB, COL_KB, COL_VB, lambda x: rms(x, gq), lambda x: rms(x, gk))

    for j in range(C_WIDTH // GROUP_W):
        o["c"][0, :, j * GROUP_W:(j + 1) * GROUP_W] = proj(COL_C + j * GROUP_W, GROUP_W)

    o["qd"][0] = (proj(COL_QD, GROUP_W) * HEAD_DIM ** -0.5).astype(BF16)
    o["kd"][0] = proj(COL_KD, GROUP_W).astype(BF16)
    o["vd"][0] = proj(COL_VD, GROUP_W).astype(BF16)


def _inproj(h, shift, scale, w_in, layer, rope_tabs, gq, gk, bd128, *, is_ctx, tm):
    b, s, d = h.shape
    assert s % tm == 0
    rc, ra, rb = rope_tabs
    outs = _inproj_outputs(is_ctx)
    out_specs = [pl.BlockSpec((1, tm, w), lambda bi, i: (bi, i, 0)) if lay == "row"
                 else pl.BlockSpec((1, w, tm), lambda bi, i: (bi, 0, i)) for _, lay, w, _ in outs]
    out_shape = [jax.ShapeDtypeStruct((b, s, w) if lay == "row" else (b, w, s), dt) for _, lay, w, dt in outs]
    tab = pl.BlockSpec((tm, V7X_LANES), lambda bi, i: (i, 0))
    vec = pl.BlockSpec((1, 1, d), lambda bi, i: (bi, 0, 0))
    res = pl.pallas_call(
        functools.partial(_inproj_kernel, is_ctx=is_ctx),
        grid=(b, s // tm),
        in_specs=[pl.BlockSpec((1, tm, d), lambda bi, i: (bi, i, 0)), vec, vec,
                  _layer_spec(w_in.shape, layer), tab, tab, tab,
                  _const_spec(gq.shape), _const_spec(gk.shape), _const_spec(bd128.shape)],
        out_specs=out_specs,
        out_shape=out_shape,
        compiler_params=_cparams(("parallel", "arbitrary")),
        name="inproj_ctx" if is_ctx else "inproj",
    )(h, scale, shift, w_in, rc, ra, rb, gq, gk, bd128)
    return {name: r for (name, _, _, _), r in zip(outs, res)}


def _attend_many(items):
    scores, maxes = [], []
    for qs, parts, sink_col in items:
        ss, m = [], sink_col
        for k, _, bias in parts:
            s = _dot_nt(qs, k)
            if bias is not None:
                s = s + bias
            ss.append(s)
            smax = jnp.max(s, axis=1, keepdims=True)
            m = smax if m is None else jnp.maximum(m, smax)
        scores.append(ss)
        maxes.append(m)
    probs, dens = [], []
    for (qs, parts, sink_col), ss, m in zip(items, scores, maxes):
        den = None if sink_col is None else jnp.exp(sink_col - m)
        ps = []
        for s in ss:
            p = jnp.exp(s - m)
            psum = jnp.sum(p, axis=1, keepdims=True)
            den = psum if den is None else den + psum
            ps.append(p.astype(BF16))
        probs.append(ps)
        dens.append(den)
    outs = []
    for (qs, parts, _), ps, den in zip(items, probs, dens):
        out = None
        for p, (_, v, _) in zip(ps, parts):
            pv = _dot(p, v)
            out = pv if out is None else out + pv
        outs.append(out * (1.0 / den))
    return outs


def _window_kernel(sink_ref, q_ref, *refs, n_pairs, n_blocks):
    n_kblk = 2 * n_pairs + 2
    k_refs, vt_refs = refs[:n_kblk], refs[n_kblk:2 * n_kblk]
    kc_ref, vtc_ref, o_ref = refs[2 * n_kblk:]
    step = pl.program_id(1)
    tq = 2 * BLOCK
    first_blk = step * 2 * n_pairs - 1
    ones_blk = jnp.ones((ONES_ROWS, BLOCK), BF16)
    ones_ctx = jnp.ones((ONES_ROWS, kc_ref.shape[1]), BF16)
    ki = lax.broadcasted_iota(jnp.int32, (BLOCK, tq), 0)
    qi = lax.broadcasted_iota(jnp.int32, (BLOCK, tq), 1)
    lane = lax.broadcasted_iota(jnp.int32, (1, 2 * tq), 1)
    zeros_q = jnp.zeros((HEAD_DIM, tq), BF16)

    def band_bias(jj, j):
        ok = jnp.abs((1 - jj) * BLOCK + qi - ki) <= WIN
        if j == 0:
            ok = ok & (first_blk >= 0)
        elif j == n_kblk - 1:
            ok = ok & (first_blk + j < n_blocks)
        bias = jnp.where(ok, 0.0, NEG_INF).astype(F32)
        return jnp.concatenate([bias, bias], axis=1)

    items = [(p, kv) for p in range(n_pairs) for kv in range(N_KV)]
    biases = {(p, jj): band_bias(jj, 2 * p + jj) for p in range(n_pairs) for jj in range(4)}

    scores, maxes = {}, {}
    for p, kv in items:
        qt = q_ref[0, 2 * HEAD_DIM * kv:2 * HEAD_DIM * (kv + 1), p * tq:(p + 1) * tq]

        def padded(g):
            qg = qt[g * HEAD_DIM:(g + 1) * HEAD_DIM, :]
            return jnp.concatenate([qg, zeros_q] if kv == 0 else [zeros_q, qg], axis=0)

        rhs = jnp.concatenate([padded(0), padded(1)], axis=1)
        ss = [_dot(k_refs[2 * p + jj][0], rhs) + biases[(p, jj)] for jj in range(4)]
        ss.append(_dot(kc_ref[0], rhs))
        sink2 = jnp.where(lane < tq, sink_ref[0, 2 * kv], sink_ref[0, 2 * kv + 1]) * LOG2_E
        m = sink2
        for s in ss:
            m = jnp.maximum(m, jnp.max(s, axis=0, keepdims=True))
        scores[(p, kv)], maxes[(p, kv)] = ss, (m, sink2)
    probs = {}
    for it in items:
        m = maxes[it][0]
        probs[it] = [jnp.exp2(s - m).astype(BF16) for s in scores.pop(it)]
    for p, kv in items:
        ps = probs.pop((p, kv))
        acc = None
        for jj in range(4):
            vt = vt_refs[2 * p + jj][0, kv * HEAD_DIM:(kv + 1) * HEAD_DIM, :]
            pv = _dot(jnp.concatenate([vt, ones_blk], axis=0), ps[jj])
            acc = pv if acc is None else acc + pv
        vtc = vtc_ref[0, kv * HEAD_DIM:(kv + 1) * HEAD_DIM, :]
        acc = acc + _dot(jnp.concatenate([vtc, ones_ctx], axis=0), ps[4])
        m, sink2 = maxes[(p, kv)]
        den = acc[HEAD_DIM:HEAD_DIM + 1] + jnp.exp2(sink2 - m)
        o = acc[0:HEAD_DIM] * (1.0 / den)
        ot = jnp.concatenate([o[:, :tq], o[:, tq:]], axis=0)
        o_ref[0, p * tq:(p + 1) * tq, 2 * HEAD_DIM * kv:2 * HEAD_DIM * (kv + 1)] = ot.T.astype(BF16)


def _window_attn(qt, k, vt, kc, vtc, sink, *, n_pairs):
    b, w, s = qt.shape
    t = kc.shape[1]
    rows = n_pairs * 2 * BLOCK
    n_blocks = s // BLOCK
    n_kblk = 2 * n_pairs + 2
    assert s % rows == 0

    def blk_idx(i, j):
        return jnp.clip(i * 2 * n_pairs - 1 + j, 0, n_blocks - 1)

    k_specs = [pl.BlockSpec((1, BLOCK, KV_W), functools.partial(lambda bi, i, j: (bi, blk_idx(i, j), 0), j=j))
               for j in range(n_kblk)]
    vt_specs = [pl.BlockSpec((1, KV_W, BLOCK), functools.partial(lambda bi, i, j: (bi, 0, blk_idx(i, j)), j=j))
                for j in range(n_kblk)]
    return pl.pallas_call(
        functools.partial(_window_kernel, n_pairs=n_pairs, n_blocks=n_blocks),
        grid=(b, s // rows),
        in_specs=[pl.BlockSpec(memory_space=pltpu.SMEM),
                  pl.BlockSpec((1, w, rows), lambda bi, i: (bi, 0, i))] + k_specs + vt_specs +
                 [pl.BlockSpec((1, t, KV_W), lambda bi, i: (bi, 0, 0)),
                  pl.BlockSpec((1, KV_W, t), lambda bi, i: (bi, 0, 0))],
        out_specs=pl.BlockSpec((1, rows, w), lambda bi, i: (bi, i, 0)),
        out_shape=jax.ShapeDtypeStruct((b, s, w), BF16),
        compiler_params=_cparams(("parallel", "arbitrary")),
        name="window_attn",
    )(sink.reshape(1, N_HEADS).astype(F32), qt, *([k] * n_kblk), *([vt] * n_kblk), kc, vtc)


def _na_kernel(q_ref, k_ref, v_ref, kc_ref, vc_ref, bias_ref, o_ref, *, rows, n_rows):
    step = pl.program_id(1)
    span = NA_ROWS * GRID_W
    ctx_part = (kc_ref[0], vc_ref[0], None)
    items = []
    for j in range(n_rows):
        r = step * n_rows + j
        r0 = jnp.clip(r - NA_ROWS // 2, 0, rows - NA_ROWS)
        start = pl.multiple_of(r0 * GRID_W, GRID_W)
        kw = k_ref[0, pl.ds(start, span), :]
        vw = v_ref[0, pl.ds(start, span), :]
        qs = _stack_heads(q_ref[0, j * GRID_W:(j + 1) * GRID_W, :])
        off0 = r0 - r + (NA_ROWS - 1)
        bias = jnp.concatenate([bias_ref[off0 + 2 * i] for i in range(NA_ROWS // 2)], axis=1)
        items.append((qs, [(kw, vw, bias), ctx_part], None))
    for j, o in enumerate(_attend_many(items)):
        o_ref[0, j * GRID_W:(j + 1) * GRID_W, :] = _unstack_heads(o, GRID_W).astype(BF16)


def _na_bias_table(rpb):
    n_h, n_ro, n_co = rpb.shape
    q = np.arange(GRID_W)
    kc = np.arange(GRID_W)
    coff = np.clip(kc[None, :] - q[:, None], -(NA_COLS - 1), NA_COLS - 1) + (NA_COLS - 1)
    c0 = np.clip(q - NA_COLS // 2, 0, GRID_W - NA_COLS)
    ok = (kc[None, :] >= c0[:, None]) & (kc[None, :] < c0[:, None] + NA_COLS)
    onehot = np.zeros((n_co, GRID_W, GRID_W), np.float32)
    onehot[coff, q[:, None], kc[None, :]] = 1.0
    tab = jnp.einsum("hrc,cqk->hrqk", rpb.astype(F32), jnp.asarray(onehot), precision=lax.Precision.HIGHEST)
    tab = jnp.where(ok[None, None], tab, NEG_INF)
    pairs = jnp.concatenate([tab[:, :-1], tab[:, 1:]], axis=-1)
    return pairs.transpose(1, 0, 2, 3).reshape(n_ro - 1, n_h * GRID_W, 2 * GRID_W)


def _na_attn(q, k, v, kc, vc, bias_tab, *, n_rows):
    b, s, w = q.shape
    t = kc.shape[1]
    rows = s // GRID_W
    assert s % GRID_W == 0 and rows >= NA_ROWS and rows % n_rows == 0
    full = lambda n: pl.BlockSpec((1, n, w), lambda bi, i: (bi, 0, 0))
    blk = n_rows * GRID_W
    return pl.pallas_call(
        functools.partial(_na_kernel, rows=rows, n_rows=n_rows),
        grid=(b, rows // n_rows),
        in_specs=[pl.BlockSpec((1, blk, w), lambda bi, r: (bi, r, 0)),
                  full(s), full(s), full(t), full(t), _const_spec(bias_tab.shape)],
        out_specs=pl.BlockSpec((1, blk, w), lambda bi, r: (bi, r, 0)),
        out_shape=jax.ShapeDtypeStruct((b, s, w), BF16),
        compiler_params=_cparams(("parallel", "arbitrary")),
        name="na_attn",
    )(q, k, v, kc, vc, bias_tab)


def _ctx_attn_kernel(sink_ref, qa_ref, ka_ref, va_ref, qb_ref, kb_ref, vb_ref, qd_ref, kd_ref, vd_ref,
                     oa_ref, ob_ref, od_ref):
    t = qa_ref.shape[1]
    sink_col = jnp.concatenate([jnp.full((t, 1), sink_ref[0, h], F32) for h in range(N_HEADS)], axis=0)
    groups = ((qa_ref, ka_ref, va_ref, sink_col), (qb_ref, kb_ref, vb_ref, None), (qd_ref, kd_ref, vd_ref, None))
    outs = _attend_many([(_stack_heads(q_ref[0]), [(k_ref[0], v_ref[0], None)], sink)
                         for q_ref, k_ref, v_ref, sink in groups])
    for o_ref, o in zip((oa_ref, ob_ref, od_ref), outs):
        o_ref[0] = _unstack_heads(o, t).astype(BF16)


def _ctx_attn(sink, qa, ka, va, qb, kb, vb, qd, kd, vd):
    b, t, w = qa.shape
    blk = pl.BlockSpec((1, t, w), lambda bi: (bi, 0, 0))
    out = jax.ShapeDtypeStruct((b, t, w), BF16)
    return pl.pallas_call(
        _ctx_attn_kernel,
        grid=(b,),
        in_specs=[pl.BlockSpec(memory_space=pltpu.SMEM)] + [blk] * 9,
        out_specs=[blk] * 3,
        out_shape=[out] * 3,
        compiler_params=_cparams(("arbitrary",)),
        name="ctx_attn",
    )(sink.reshape(1, N_HEADS).astype(F32), qa, ka, va, qb, kb, vb, qd, kd, vd)


def _global_kernel(q_ref, k_ref, vt_ref, kc_ref, vtc_ref, o_ref, *, tq, kt):
    kv = pl.program_id(1)
    n_blk = q_ref.shape[2] // tq
    sel = (lax.broadcasted_iota(jnp.int32, (2 * HEAD_DIM, tq), 0) // HEAD_DIM) == kv

    def block_rhs(blk):
        qt = q_ref[0, :, blk * tq:(blk + 1) * tq]
        zero = jnp.zeros_like(qt)

        def padded(g):
            qg = qt[g * HEAD_DIM:(g + 1) * HEAD_DIM, :]
            return jnp.where(sel, jnp.concatenate([qg, qg], axis=0), zero)

        return jnp.concatenate([padded(0), padded(1)], axis=1)

    rhs = [block_rhs(blk) for blk in range(n_blk)]
    n_lat = k_ref.shape[1] // kt
    n_tiles = n_lat + kc_ref.shape[1] // kt
    ones = jnp.ones((ONES_ROWS, kt), BF16)

    def k_tile(i):
        return k_ref[0, i * kt:(i + 1) * kt, :] if i < n_lat else kc_ref[0, (i - n_lat) * kt:(i - n_lat + 1) * kt, :]

    def vt_tile(i):
        vt = (vt_ref[0, :, i * kt:(i + 1) * kt] if i < n_lat
              else vtc_ref[0, :, (i - n_lat) * kt:(i - n_lat + 1) * kt])
        return jnp.concatenate([vt, ones], axis=0)

    scores, tile_max, run_max, probs = {}, {}, {}, {}
    acc = [None] * n_blk
    n_items = n_blk * n_tiles
    for i in range(n_items + 2):
        if i < n_items:
            blk, c = divmod(i, n_tiles)
            scores[i] = _dot(k_tile(c), rhs[blk])
            tile_max[i] = jnp.max(scores[i], axis=0, keepdims=True)
        j = i - 1
        if 0 <= j < n_items:
            first = j % n_tiles == 0
            run_max[j] = tile_max[j] if first else jnp.maximum(run_max[j - 1], tile_max[j])
            probs[j] = jnp.exp2(scores.pop(j) - run_max[j]).astype(BF16)
        j = i - 2
        if 0 <= j < n_items:
            blk, c = divmod(j, n_tiles)
            pv = _dot(vt_tile(c), probs.pop(j))
            acc[blk] = pv if c == 0 else acc[blk] * jnp.exp2(run_max[j - 1] - run_max[j]) + pv
            if c == n_tiles - 1:
                o = acc[blk][0:HEAD_DIM] * (1.0 / acc[blk][HEAD_DIM:HEAD_DIM + 1])
                ot = jnp.concatenate([o[:, :tq], o[:, tq:]], axis=0)
                o_ref[0, blk * tq:(blk + 1) * tq, :] = ot.T.astype(BF16)


def _global_attn(qt, k, vt, kc, vtc, *, tq, kt, n_blk):
    b, w, s = qt.shape
    t = kc.shape[1]
    rows = n_blk * tq
    assert s % rows == 0 and s % kt == 0 and t % kt == 0
    return pl.pallas_call(
        functools.partial(_global_kernel, tq=tq, kt=kt),
        grid=(b, N_KV, s // rows),
        in_specs=[pl.BlockSpec((1, 2 * HEAD_DIM, rows), lambda bi, kv, i: (bi, kv, i)),
                  pl.BlockSpec((1, s, KV_W), lambda bi, kv, i: (bi, 0, 0)),
                  pl.BlockSpec((1, HEAD_DIM, s), lambda bi, kv, i: (bi, kv, 0)),
                  pl.BlockSpec((1, t, KV_W), lambda bi, kv, i: (bi, 0, 0)),
                  pl.BlockSpec((1, HEAD_DIM, t), lambda bi, kv, i: (bi, kv, 0))],
        out_specs=pl.BlockSpec((1, rows, 2 * HEAD_DIM), lambda bi, kv, i: (bi, i, kv)),
        out_shape=jax.ShapeDtypeStruct((b, s, w), BF16),
        compiler_params=_cparams(("parallel", "arbitrary", "arbitrary")),
        name="global_attn",
    )(qt, k, vt, kc, vtc)


def _hgrn_constants():
    n = HGRN_L
    tm = np.zeros((2, 2 * n, n), np.float32)
    mk = np.zeros((2, HGRN_LEVELS + 1, n, n), np.float32)
    for d in range(2):
        pos = np.arange(n) if d == 0 else n - 1 - np.arange(n)
        p, q = pos[:, None], pos[None, :]
        for li in range(HGRN_LEVELS):
            h = n >> (li + 1)
            same_blk = (p // (2 * h)) == (q // (2 * h))
            up_p, up_q = (p % (2 * h)) >= h, (q % (2 * h)) >= h
            mk[d, li] = same_blk & up_p & ~up_q
            if h == 2:
                same_half = same_blk & (up_p == up_q)
                tm[d, 0:n] = np.where(up_p, same_half & (q <= p), same_half & (q > p))
        mk[d, HGRN_LEVELS] = np.eye(n)
        tm[d, n:2 * n] = q <= p
    mk = np.tile(mk.transpose(0, 1, 3, 2), (1, 1, 1, N_HEADS))
    return jnp.asarray(tm, BF16), jnp.asarray(mk, F32)


def _hgrn_seq_kernel(qf_ref, vf_ref, zf_ref, qb_ref, vb_ref, zb_ref, lb_ref, t_ref, mk_ref, bd_ref, *rest,
                     g_chunks, has_init, emit_state):
    rest = list(rest)
    init_ref = rest.pop(0) if has_init else None
    of_ref, ob_ref = rest.pop(0), rest.pop(0)
    so_ref = rest.pop(0) if emit_state else None
    st_ref = rest.pop(0)
    step = pl.program_id(1)
    n = HGRN_L

    @pl.when(step == 0)
    def _():
        st_ref[...] = init_ref[0] if has_init else jnp.zeros_like(st_ref)

    ins = ((qf_ref, vf_ref, zf_ref), (qb_ref, vb_ref, zb_ref))
    outs = (of_ref, ob_ref)
    items = [(d, (g if d == 0 else g_chunks - 1 - g) * n) for g in range(g_chunks) for d in (0, 1)]

    work = []
    for d, r0 in items:
        q = ins[d][0][0, r0:r0 + n, :]
        v = ins[d][1][0, r0:r0 + n, :]
        z = ins[d][2][0, r0:r0 + n, :]
        lb = lb_ref[d]
        f = lb + (1.0 - lb) * jax.nn.sigmoid(z)
        g = jnp.log(f)
        g_hi = g.astype(BF16)
        g_lo = (g - g_hi.astype(F32)).astype(BF16)
        e = _dot(t_ref[d], g_hi) + _dot(t_ref[d], g_lo)
        work.append(dict(qq=_silu(q), kk=1.0 - f, f=f, v=v, e2=e[0:n], b=e[n:2 * n],
                         a=jnp.zeros((n, N_HEADS * n), F32)))

    row = lax.broadcasted_iota(jnp.int32, (n, GROUP_W), 0)

    def level_scale(w, d, h):
        if h == 2:
            return jnp.exp(w["e2"])
        b = w["b"]
        blocks = []
        for j in range(n // (2 * h)):
            m = 2 * h * j + (h - 1 if d == 0 else h)
            blocks.append(jnp.broadcast_to(b[m:m + 1, :], (2 * h, GROUP_W)))
        bm = blocks[0] if len(blocks) == 1 else jnp.concatenate(blocks, axis=0)
        later = ((row // h) % 2) == (1 if d == 0 else 0)
        return jnp.exp(jnp.where(later, b - bm, bm - b))

    for li in range(HGRN_LEVELS + 1):
        h = n >> (li + 1)
        for (d, _), w in zip(items, work):
            if h >= 2:
                sc = level_scale(w, d, h)
                qs, ks = w["qq"] * sc, w["kk"] * sc
            elif h == 1:
                qs, ks = w["qq"] * w["f"], w["kk"]
            else:
                qs, ks = w["qq"], w["kk"]
            w["a"] = w["a"] + _dot_nt(ks.astype(BF16), _stack_heads(qs.astype(BF16))) * mk_ref[d, li]

    for (d, _), w in zip(items, work):
        b_incl = w["b"]
        b_tot = b_incl[n - 1:n] if d == 0 else b_incl[0:1]
        vb = w["v"].astype(BF16)
        w["q_hat"] = (w["qq"] * jnp.exp(b_incl)).astype(BF16)
        k_hat = (w["kk"] * jnp.exp(b_tot - b_incl)).astype(BF16)
        w["decay"] = jnp.exp(b_tot)
        w["o"] = _unstack_heads(_dot(w["a"].T.astype(BF16), vb), n)
        w["upd"] = _dot(w["v"].T.astype(BF16), k_hat) * bd_ref[...]

    st = [st_ref[0], st_ref[1]]
    for (d, r0), w in zip(items, work):
        outs[d][0, r0:r0 + n, :] = w["o"] + _dot_nt(w["q_hat"], st[d].astype(BF16))
        st[d] = st[d] * w["decay"] + w["upd"]
    st_ref[0] = st[0]
    st_ref[1] = st[1]
    if emit_state:
        so_ref[0, 0] = st[0]
        so_ref[0, 1] = st[1]


def _hgrn_seq(c_raw, lb, consts, init_state, *, g_chunks, emit_state):
    tmat, masks, bd = consts
    b, s, _ = c_raw.shape
    rows = g_chunks * HGRN_L
    assert s % rows == 0
    n_steps = s // rows
    has_init = init_state is not None
    fwd = lambda col: pl.BlockSpec((1, rows, GROUP_W), lambda bi, st: (bi, st, col))
    bwd = lambda col: pl.BlockSpec((1, rows, GROUP_W), lambda bi, st: (bi, n_steps - 1 - st, col))
    state_spec = pl.BlockSpec((1, 2, GROUP_W, GROUP_W), lambda bi, st: (bi, 0, 0, 0))
    in_specs = [fwd(0), fwd(1), fwd(2), bwd(0), bwd(1), bwd(3),
                _const_spec((2, 1, GROUP_W)), _const_spec(tmat.shape), _const_spec(masks.shape),
                _const_spec(bd.shape)]
    args = [c_raw] * 6 + [lb.reshape(2, 1, GROUP_W), tmat, masks, bd]
    if has_init:
        in_specs.append(state_spec)
        args.append(init_state)
    out_specs = [fwd(0), bwd(0)]
    out_shape = [jax.ShapeDtypeStruct((b, s, GROUP_W), F32)] * 2
    if emit_state:
        out_specs.append(state_spec)
        out_shape.append(jax.ShapeDtypeStruct((b, 2, GROUP_W, GROUP_W), F32))
    return pl.pallas_call(
        functools.partial(_hgrn_seq_kernel, g_chunks=g_chunks, has_init=has_init, emit_state=emit_state),
        grid=(b, n_steps),
        in_specs=in_specs,
        out_specs=out_specs,
        out_shape=out_shape,
        scratch_shapes=[pltpu.VMEM((2, GROUP_W, GROUP_W), F32)],
        compiler_params=_cparams(("parallel", "arbitrary")),
        name="hgrn_ctx" if emit_state else "hgrn",
    )(*args)


def _outproj_kernel(x_ref, oa_ref, ob_ref, of_ref, obw_ref, gate_ref, od_ref, w_ref, onorm_ref, bd_ref,
                    ga_ref, lng_ref, lnb_ref, o_ref, *, alpha):
    oc = of_ref[0] + obw_ref[0]
    y = oc * lax.rsqrt(_head_mean_sq(oc, bd_ref[...]) + EPS) * onorm_ref[...]
    y = y * _silu(gate_ref[0])
    mix = jnp.concatenate([oa_ref[0], ob_ref[0], y.astype(BF16), od_ref[0]], axis=1)
    z = alpha * x_ref[0] + ga_ref[0] * _dot(mix, w_ref[0])
    o_ref[0] = _layer_norm(z, lng_ref[...], lnb_ref[...])


def _outproj(x, oa, ob, oc_f, oc_b, c_raw, od, w_out, layer, onorm, bd256, gate_a, ln_g, ln_b, *, alpha, tm):
    b, s, d = x.shape
    assert s % tm == 0
    grp = pl.BlockSpec((1, tm, GROUP_W), lambda bi, i: (bi, i, 0))
    vec = pl.BlockSpec((1, 1, d), lambda bi, i: (bi, 0, 0))
    return pl.pallas_call(
        functools.partial(_outproj_kernel, alpha=alpha),
        grid=(b, s // tm),
        in_specs=[pl.BlockSpec((1, tm, d), lambda bi, i: (bi, i, 0)), grp, grp, grp, grp,
                  pl.BlockSpec((1, tm, GROUP_W), lambda bi, i: (bi, i, 4)),
                  grp, _layer_spec(w_out.shape, layer), _const_spec(onorm.shape), _const_spec(bd256.shape),
                  vec, _const_spec(ln_g.shape), _const_spec(ln_b.shape)],
        out_specs=pl.BlockSpec((1, tm, d), lambda bi, i: (bi, i, 0)),
        out_shape=jax.ShapeDtypeStruct((b, s, d), F32),
        compiler_params=_cparams(("parallel", "arbitrary")),
        name="outproj",
    )(x, oa, ob, oc_f, oc_b, c_raw, od, w_out, onorm, bd256, gate_a, ln_g, ln_b)


def _ffn_kernel(h_ref, hp_ref, hn_ref, sc_ref, sh_ref, gt_ref, wup_ref, cw_ref, cb_ref, wdn_ref,
                lng_ref, lnb_ref, o_ref, g_scr, *, alpha, d_ff, cwid, n_tiles):
    i = pl.program_id(1)
    tm = h_ref.shape[1]
    sc = 1.0 + sc_ref[0]
    sh = sh_ref[0]
    h = h_ref[0]
    halo = jnp.concatenate([hp_ref[0], hn_ref[0]], axis=0)
    lhs = jnp.concatenate([(h * sc + sh).astype(BF16), (halo * sc + sh).astype(BF16)], axis=0)
    has_prev = (i > 0).astype(F32)
    has_next = (i < n_tiles - 1).astype(F32)
    row = lax.broadcasted_iota(jnp.int32, (tm, cwid), 0)
    first, last = row == 0, row == tm - 1

    def conv_cols(c0):
        u = _dot(lhs, wup_ref[0, :, c0:c0 + cwid])
        um = u[:tm]
        u_prev = u[tm + 7:tm + 8] * has_prev
        u_next = u[tm + 8:tm + 9] * has_next
        dn = jnp.where(first, u_prev, pltpu.roll(um, 1, 0))
        up = jnp.where(last, u_next, pltpu.roll(um, tm - 1, 0))
        w = cw_ref[:, c0:c0 + cwid]
        return dn * w[0:1] + um * w[1:2] + up * w[2:3] + cb_ref[:, c0:c0 + cwid]

    for j in range(d_ff // cwid):
        a = conv_cols(j * cwid)
        v = conv_cols(d_ff + j * cwid)
        g_scr[:, j * cwid:(j + 1) * cwid] = (_silu(a) * v).astype(BF16)
    z = alpha * h + gt_ref[0] * _dot(g_scr[...], wdn_ref[0])
    o_ref[0] = _layer_norm(z, lng_ref[...], lnb_ref[...])


def _ffn(h, scale, shift, gate, w_up, conv_w, conv_b, w_down, layer, ln_g, ln_b, *, alpha, tm, cwid):
    b, s, d = h.shape
    d_ff = w_down.shape[1]
    assert s % tm == 0 and tm % 16 == 0 and d_ff % cwid == 0
    n_tiles = s // tm
    r8 = tm // 8
    vec = pl.BlockSpec((1, 1, d), lambda bi, i: (bi, 0, 0))
    return pl.pallas_call(
        functools.partial(_ffn_kernel, alpha=alpha, d_ff=d_ff, cwid=cwid, n_tiles=n_tiles),
        grid=(b, n_tiles),
        in_specs=[pl.BlockSpec((1, tm, d), lambda bi, i: (bi, i, 0)),
                  pl.BlockSpec((1, 8, d), lambda bi, i: (bi, jnp.maximum(i * r8 - 1, 0), 0)),
                  pl.BlockSpec((1, 8, d), lambda bi, i: (bi, jnp.minimum((i + 1) * r8, s // 8 - 1), 0)),
                  vec, vec, vec,
                  _layer_spec(w_up.shape, layer), _const_spec(conv_w.shape), _const_spec(conv_b.shape),
                  _layer_spec(w_down.shape, layer), _const_spec(ln_g.shape), _const_spec(ln_b.shape)],
        out_specs=pl.BlockSpec((1, tm, d), lambda bi, i: (bi, i, 0)),
        out_shape=jax.ShapeDtypeStruct((b, s, d), F32),
        scratch_shapes=[pltpu.VMEM((tm, d_ff), BF16)],
        compiler_params=_cparams(("parallel", "arbitrary")),
        name="conv_ffn",
    )(h, h, h, scale, shift, gate, w_up, conv_w, conv_b, w_down, ln_g, ln_b)


def _rope_tables(n, with_rotation):
    lane = np.arange(V7X_LANES)
    if not with_rotation:
        one = jnp.ones((n, V7X_LANES), F32)
        zero = jnp.zeros((n, V7X_LANES), F32)
        return one, zero, zero
    t = np.arange(n)
    pos = np.stack([t // GRID_W, t % GRID_W], -1).astype(np.float64)
    n_freq = HEAD_DIM // 4
    inv = ROPE_BASE ** (-np.arange(n_freq, dtype=np.float64) / n_freq)
    ang = pos[:, :, None] * inv
    axis = (lane % HEAD_DIM) // (HEAD_DIM // 2)
    freq = lane % n_freq
    ang_l = ang[:, axis, freq]
    first = (lane % (2 * n_freq)) < n_freq
    cos, sin = np.cos(ang_l), np.sin(ang_l)
    tabs = (cos, np.where(first, -sin, 0.0), np.where(first, 0.0, sin))
    return tuple(jnp.asarray(a, F32) for a in tabs)


def _block_diag_mean(width):
    lane = np.arange(width)
    same = (lane[:, None] // HEAD_DIM) == (lane[None, :] // HEAD_DIM)
    return jnp.asarray(same / HEAD_DIM, BF16)


def kernel(x, c, ctx, c_ctx, w_mod, b_mod, w_in, w_out, ln_g, ln_b, attn_sink, qk_norm, hgrn_lb_logits,
           hgrn_onorm, na_rpb, ffn_w_up, ffn_conv_w, ffn_conv_b, ffn_w_down):
    depth = w_mod.shape[0]
    b, s, d = x.shape
    t = ctx.shape[1]
    alpha = (2 * depth) ** 0.25

    mods = _modulation(c, c_ctx, w_mod, b_mod)
    rope_lat = _rope_tables(s, True)
    rope_ctx = _rope_tables(t, False)
    bd128, bd256 = _block_diag_mean(KV_W), _block_diag_mean(GROUP_W)
    lane = np.arange(GROUP_W)
    head_bd = jnp.asarray((lane[:, None] // HEAD_DIM) == (lane[None, :] // HEAD_DIM), F32)
    hgrn_consts = _hgrn_constants() + (head_bd,)
    p_lb = jax.nn.softmax(hgrn_lb_logits.astype(F32), axis=1)
    lb_all = jnp.cumsum(p_lb, axis=1) - p_lb[:, :1]
    w_in_b, w_out_b = w_in.astype(BF16), w_out.astype(BF16)
    w_up_b, w_dn_b = ffn_w_up.astype(BF16), ffn_w_down.astype(BF16)

    h_x, h_c = x, ctx
    for l in range(depth):
        need_ctx = l < depth - 1
        mod = mods[l]
        sa, ca, ga, sf, cf, gf = (mod[:, i * d:(i + 1) * d] for i in range(6))
        lat = lambda a: a[:b, None, :]
        con = lambda a: jnp.broadcast_to(a[b][None, None, :], (b, 1, d))
        gq = jnp.tile(qk_norm[l, 0], 2)[None, :]
        gk = jnp.tile(qk_norm[l, 1], 2)[None, :]

        px = _inproj(h_x, lat(sa), lat(ca), w_in_b, l, rope_lat, gq, gk, bd128, is_ctx=False, tm=1024)
        pc = _inproj(h_c, con(sa), con(ca), w_in_b, l, rope_ctx, gq, gk, bd128, is_ctx=True, tm=t)
        c_raw, c_raw_c = px["c"], pc["c"]

        o_a = _window_attn(px["qa"], px["ka"], px["vat"], pc["ka"], pc["vat"], attn_sink[l],
                           n_pairs=WINDOW_PAIRS_PER_STEP)

        o_b = _global_attn(px["qb"], px["kb"], px["vbt"], pc["kb"], pc["vbt"], tq=256, kt=256,
                           n_blk=GLOBAL_BLOCKS_PER_STEP)

        ocf_c, ocb_c, ctx_state = _hgrn_seq(c_raw_c, lb_all[:, l], hgrn_consts, None,
                                            g_chunks=t // HGRN_L, emit_state=True)
        oc_f, oc_b = _hgrn_seq(c_raw, lb_all[:, l], hgrn_consts, ctx_state,
                               g_chunks=HGRN_CHUNKS_PER_STEP, emit_state=False)

        o_d = _na_attn(px["qd"], px["kd"], px["vd"], pc["kd"], pc["vd"], _na_bias_table(na_rpb[l]),
                       n_rows=NA_ROWS_PER_STEP)

        onorm = hgrn_onorm[l][None, :]
        lng0, lnb0 = ln_g[l, 0][None, :], ln_b[l, 0][None, :]
        lng1, lnb1 = ln_g[l, 1][None, :], ln_b[l, 1][None, :]
        cw, cb = ffn_conv_w[l], ffn_conv_b[l][None, :]
        h_x = _outproj(h_x, o_a, o_b, oc_f, oc_b, c_raw, o_d, w_out_b, l, onorm, bd256, lat(ga), lng0, lnb0,
                       alpha=alpha, tm=1024)
        h_x = _ffn(h_x, lat(cf), lat(sf), lat(gf), w_up_b, cw, cb, w_dn_b, l, lng1, lnb1,
                   alpha=alpha, tm=512, cwid=256)
        if need_ctx:
            oa_c, ob_c, od_c = _ctx_attn(attn_sink[l], pc["qa"], pc["kax"], pc["vax"], pc["qb"], pc["kbx"],
                                         pc["vbx"], pc["qd"], pc["kd"], pc["vd"])
            h_c = _outproj(h_c, oa_c, ob_c, ocf_c, ocb_c, c_raw_c, od_c, w_out_b, l, onorm, bd256, con(ga),
                           lng0, lnb0, alpha=alpha, tm=t)
            h_c = _ffn(h_c, con(cf), con(sf), con(gf), w_up_b, cw, cb, w_dn_b, l, lng1, lnb1,
                       alpha=alpha, tm=t, cwid=256)
    return h_x
```

```python
import functools

import numpy as np
import jax
import jax.numpy as jnp
from jax import lax
from jax.experimental import pallas as pl
from jax.experimental.pallas import tpu as pltpu

F32 = jnp.float32
BF16 = jnp.bfloat16

HEAD_DIM = 64
N_HEADS = 4
N_KV = 2
GROUP_W = N_HEADS * HEAD_DIM
KV_W = N_KV * HEAD_DIM
GRID_W = 64
WIN = 128
BLOCK = 128
NA_ROWS = 8
NA_COLS = 16
ROPE_BASE = 10000.0
EPS = 1e-6
HGRN_L = 64
HGRN_LEVELS = 6
HGRN_CHUNKS_PER_STEP = 4
WINDOW_PAIRS_PER_STEP = 4
NA_ROWS_PER_STEP = 8
NA_ROWS_PER_WAVE = 2
GLOBAL_BLOCKS_PER_STEP = 4
ONES_ROWS = 16

V7X_LANES = 128
V7X_VMEM_LIMIT_BYTES = 56 * 1024 * 1024

COL_QA, COL_KA, COL_VA = 0, 256, 384
COL_QB, COL_KB, COL_VB = 512, 768, 896
COL_C = 1024
COL_QD, COL_KD, COL_VD = 2304, 2560, 2816
IN_WIDTH = 3072
C_WIDTH = 5 * GROUP_W

NEG_INF = float("-inf")
LOG2_E = 1.4426950408889634


def _cparams(sem):
    return pltpu.CompilerParams(dimension_semantics=sem, vmem_limit_bytes=V7X_VMEM_LIMIT_BYTES)


def _const_spec(shape):
    nd = len(shape)
    return pl.BlockSpec(shape, lambda *_: (0,) * nd, pipeline_mode=pl.Buffered(1))


def _layer_spec(shape, layer):
    nd = len(shape)
    return pl.BlockSpec((1,) + tuple(shape[1:]), lambda *_: (layer,) + (0,) * (nd - 1),
                        pipeline_mode=pl.Buffered(1))


def _silu(x):
    return x * jax.nn.sigmoid(x)


def _dot(a, b):
    return jnp.dot(a, b, preferred_element_type=F32)


def _dot_nt(a, b):
    return lax.dot_general(a, b, (((1,), (1,)), ((), ())), preferred_element_type=F32)


def _head_mask(shape, h):
    lane = lax.broadcasted_iota(jnp.int32, shape, len(shape) - 1)
    return (lane // HEAD_DIM) == h


def _stack_heads(q):
    zero = jnp.zeros_like(q)
    return jnp.concatenate([jnp.where(_head_mask(q.shape, h), q, zero) for h in range(N_HEADS)], axis=0)


def _unstack_heads(o, r):
    out = jnp.zeros((r, o.shape[1]), o.dtype)
    for h in range(N_HEADS):
        blk = o[h * r:(h + 1) * r]
        out = jnp.where(_head_mask(blk.shape, h), blk, out)
    return out


def _head_mean_sq(x, bd):
    sq = x * x
    hi = sq.astype(BF16)
    lo = (sq - hi.astype(F32)).astype(BF16)
    return _dot(hi, bd) + _dot(lo, bd)


def _layer_norm(z, g, b):
    mu = jnp.mean(z, axis=-1, keepdims=True)
    zc = z - mu
    var = jnp.mean(zc * zc, axis=-1, keepdims=True)
    return zc * lax.rsqrt(var + EPS) * g + b


def _mod_kernel(s_ref, w_ref, b_ref, o_ref):
    s = _silu(s_ref[...])
    o_ref[0] = jnp.dot(s, w_ref[0], preferred_element_type=F32,
                       precision=lax.Precision.HIGHEST) + b_ref[0]


def _modulation(c, c_ctx, w_mod, b_mod):
    depth, d, d6 = w_mod.shape
    b = c.shape[0]
    rows = 8
    assert b < rows
    s_in = jnp.zeros((rows, d), F32).at[:b].set(c).at[b].set(c_ctx)
    tn = d6 // 4
    return pl.pallas_call(
        _mod_kernel,
        grid=(depth, d6 // tn),
        in_specs=[pl.BlockSpec((rows, d), lambda l, j: (0, 0)),
                  pl.BlockSpec((1, d, tn), lambda l, j: (l, 0, j)),
                  pl.BlockSpec((1, 1, tn), lambda l, j: (l, 0, j))],
        out_specs=pl.BlockSpec((1, rows, tn), lambda l, j: (l, 0, j)),
        out_shape=jax.ShapeDtypeStruct((depth, rows, d6), F32),
        compiler_params=_cparams(("arbitrary", "arbitrary")),
        name="modulation",
    )(s_in, w_mod, b_mod.reshape(depth, 1, d6))


def _dup_heads(x):
    r = pltpu.roll(x, HEAD_DIM, 1)
    lo = lax.broadcasted_iota(jnp.int32, x.shape, 1) < HEAD_DIM
    return jnp.concatenate([jnp.where(lo, x, r), jnp.where(lo, r, x)], axis=1)


def _inproj_outputs(is_ctx):
    gqa = lambda g: [("q" + g, "row" if is_ctx else "col", GROUP_W, BF16), ("k" + g, "row", KV_W, BF16),
                     ("v" + g + "t", "col", KV_W, BF16)]
    dup = lambda g: [("k" + g + "x", "row", GROUP_W, BF16), ("v" + g + "x", "row", GROUP_W, BF16)]
    outs = gqa("a") + gqa("b")
    if is_ctx:
        outs += dup("a") + dup("b")
    return outs + [("c", "row", C_WIDTH, F32), ("qd", "row", GROUP_W, BF16), ("kd", "row", GROUP_W, BF16),
                   ("vd", "row", GROUP_W, BF16)]


def _inproj_kernel(h_ref, sc_ref, sh_ref, w_ref, rc_ref, ra_ref, rb_ref, gq_ref, gk_ref, bd_ref,
                   *outs, is_ctx):
    o = {name: ref for (name, _, _, _), ref in zip(_inproj_outputs(is_ctx), outs)}
    hm = (h_ref[0] * (1.0 + sc_ref[0]) + sh_ref[0]).astype(BF16)
    rc, ra, rb = rc_ref[...], ra_ref[...], rb_ref[...]
    bd = bd_ref[...]
    q_scale = HEAD_DIM ** -0.5 * (1.0 if is_ctx else LOG2_E)

    def proj(c0, width):
        return _dot(hm, w_ref[0, :, c0:c0 + width])

    def rope(x):
        return x * rc + pltpu.roll(x, V7X_LANES - 16, 1) * ra + pltpu.roll(x, 16, 1) * rb

    def rms(x, gain):
        return x * lax.rsqrt(_head_mean_sq(x, bd) + EPS) * gain

    def gqa_group(g, c_q, c_k, c_v, q_fn, k_fn):
        xq = proj(c_q, GROUP_W)
        for j in range(2):
            sl = slice(j * V7X_LANES, (j + 1) * V7X_LANES)
            y = rope(q_fn(xq[:, sl])) * q_scale
            if is_ctx:
                o["q" + g][0, :, sl] = y.astype(BF16)
            else:
                o["q" + g][0, sl, :] = y.T.astype(BF16)
        k = rope(k_fn(proj(c_k, KV_W)))
        v = proj(c_v, KV_W)
        o["k" + g][0] = k.astype(BF16)
        o["v" + g + "t"][0] = v.T.astype(BF16)
        if is_ctx:
            o["k" + g + "x"][0] = _dup_heads(k).astype(BF16)
            o["v" + g + "x"][0] = _dup_heads(v).astype(BF16)

    gq, gk = gq_ref[...], gk_ref[...]
    gqa_group("a", COL_QA, COL_KA, COL_VA, lambda x: x, lambda x: x)
    gqa_group("b", COL_QB, COL_KB, COL_VB, lambda x: rms(x, gq), lambda x: rms(x, gk))

    for j in range(C_WIDTH // GROUP_W):
        o["c"][0, :, j * GROUP_W:(j + 1) * GROUP_W] = proj(COL_C + j * GROUP_W, GROUP_W)

    o["qd"][0] = (proj(COL_QD, GROUP_W) * HEAD_DIM ** -0.5).astype(BF16)
    o["kd"][0] = proj(COL_KD, GROUP_W).astype(BF16)
    o["vd"][0] = proj(COL_VD, GROUP_W).astype(BF16)


def _inproj(h, shift, scale, w_in, layer, rope_tabs, gq, gk, bd128, *, is_ctx, tm):
    b, s, d = h.shape
    assert s % tm == 0
    rc, ra, rb = rope_tabs
    outs = _inproj_outputs(is_ctx)
    out_specs = [pl.BlockSpec((1, tm, w), lambda bi, i: (bi, i, 0)) if lay == "row"
                 else pl.BlockSpec((1, w, tm), lambda bi, i: (bi, 0, i)) for _, lay, w, _ in outs]
    out_shape = [jax.ShapeDtypeStruct((b, s, w) if lay == "row" else (b, w, s), dt) for _, lay, w, dt in outs]
    tab = pl.BlockSpec((tm, V7X_LANES), lambda bi, i: (i, 0))
    vec = pl.BlockSpec((1, 1, d), lambda bi, i: (bi, 0, 0))
    res = pl.pallas_call(
        functools.partial(_inproj_kernel, is_ctx=is_ctx),
        grid=(b, s // tm),
        in_specs=[pl.BlockSpec((1, tm, d), lambda bi, i: (bi, i, 0)), vec, vec,
                  _layer_spec(w_in.shape, layer), tab, tab, tab,
                  _const_spec(gq.shape), _const_spec(gk.shape), _const_spec(bd128.shape)],
        out_specs=out_specs,
        out_shape=out_shape,
        compiler_params=_cparams(("parallel", "arbitrary")),
        name="inproj_ctx" if is_ctx else "inproj",
    )(h, scale, shift, w_in, rc, ra, rb, gq, gk, bd128)
    return {name: r for (name, _, _, _), r in zip(outs, res)}


def _attend_many(items, wave=None):
    n = len(items)
    wave = wave or n
    scores, maxes, probs, dens, outs = {}, {}, {}, {}, {}

    def stage_scores(i):
        qs, parts, sink_col = items[i]
        ss, m = [], sink_col
        for k, _, bias in parts:
            s = _dot_nt(qs, k)
            if bias is not None:
                s = s + bias
            ss.append(s)
            smax = jnp.max(s, axis=1, keepdims=True)
            m = smax if m is None else jnp.maximum(m, smax)
        scores[i], maxes[i] = ss, m

    def stage_probs(i):
        sink_col, m = items[i][2], maxes[i]
        den = None if sink_col is None else jnp.exp(sink_col - m)
        ps = []
        for s in scores.pop(i):
            p = jnp.exp(s - m)
            psum = jnp.sum(p, axis=1, keepdims=True)
            den = psum if den is None else den + psum
            ps.append(p.astype(BF16))
        probs[i], dens[i] = ps, den

    def stage_pv(i):
        out = None
        for p, (_, v, _) in zip(probs.pop(i), items[i][1]):
            pv = _dot(p, v)
            out = pv if out is None else out + pv
        outs[i] = out * (1.0 / dens[i])

    n_waves = -(-n // wave)
    for w in range(n_waves + 2):
        for stage, lag in ((stage_scores, 0), (stage_probs, 1), (stage_pv, 2)):
            ww = w - lag
            if 0 <= ww < n_waves:
                for i in range(ww * wave, min((ww + 1) * wave, n)):
                    stage(i)
    return [outs[i] for i in range(n)]


def _window_kernel(sink_ref, q_ref, *refs, n_pairs, n_blocks):
    n_kblk = 2 * n_pairs + 2
    k_refs, vt_refs = refs[:n_kblk], refs[n_kblk:2 * n_kblk]
    kc_ref, vtc_ref, o_ref = refs[2 * n_kblk:]
    step = pl.program_id(1)
    tq = 2 * BLOCK
    first_blk = step * 2 * n_pairs - 1
    ones_blk = jnp.ones((ONES_ROWS, BLOCK), BF16)
    ones_ctx = jnp.ones((ONES_ROWS, kc_ref.shape[1]), BF16)
    ki = lax.broadcasted_iota(jnp.int32, (BLOCK, tq), 0)
    qi = lax.broadcasted_iota(jnp.int32, (BLOCK, tq), 1)
    lane = lax.broadcasted_iota(jnp.int32, (1, 2 * tq), 1)
    zeros_q = jnp.zeros((HEAD_DIM, tq), BF16)

    def band_bias(jj, j):
        ok = jnp.abs((1 - jj) * BLOCK + qi - ki) <= WIN
        if j == 0:
            ok = ok & (first_blk >= 0)
        elif j == n_kblk - 1:
            ok = ok & (first_blk + j < n_blocks)
        bias = jnp.where(ok, 0.0, NEG_INF).astype(F32)
        return jnp.concatenate([bias, bias], axis=1)

    items = [(p, kv) for p in range(n_pairs) for kv in range(N_KV)]
    biases = {(p, jj): band_bias(jj, 2 * p + jj) for p in range(n_pairs) for jj in range(4)}

    scores, maxes = {}, {}
    for p, kv in items:
        qt = q_ref[0, 2 * HEAD_DIM * kv:2 * HEAD_DIM * (kv + 1), p * tq:(p + 1) * tq]

        def padded(g):
            qg = qt[g * HEAD_DIM:(g + 1) * HEAD_DIM, :]
            return jnp.concatenate([qg, zeros_q] if kv == 0 else [zeros_q, qg], axis=0)

        rhs = jnp.concatenate([padded(0), padded(1)], axis=1)
        ss = [_dot(k_refs[2 * p + jj][0], rhs) + biases[(p, jj)] for jj in range(4)]
        ss.append(_dot(kc_ref[0], rhs))
        sink2 = jnp.where(lane < tq, sink_ref[0, 2 * kv], sink_ref[0, 2 * kv + 1]) * LOG2_E
        m = sink2
        for s in ss:
            m = jnp.maximum(m, jnp.max(s, axis=0, keepdims=True))
        scores[(p, kv)], maxes[(p, kv)] = ss, (m, sink2)
    probs = {}
    for it in items:
        m = maxes[it][0]
        probs[it] = [jnp.exp2(s - m).astype(BF16) for s in scores.pop(it)]
    for p, kv in items:
        ps = probs.pop((p, kv))
        acc = None
        for jj in range(4):
            vt = vt_refs[2 * p + jj][0, kv * HEAD_DIM:(kv + 1) * HEAD_DIM, :]
            pv = _dot(jnp.concatenate([vt, ones_blk], axis=0), ps[jj])
            acc = pv if acc is None else acc + pv
        vtc = vtc_ref[0, kv * HEAD_DIM:(kv + 1) * HEAD_DIM, :]
        acc = acc + _dot(jnp.concatenate([vtc, ones_ctx], axis=0), ps[4])
        m, sink2 = maxes[(p, kv)]
        den = acc[HEAD_DIM:HEAD_DIM + 1] + jnp.exp2(sink2 - m)
        o = acc[0:HEAD_DIM] * (1.0 / den)
        ot = jnp.concatenate([o[:, :tq], o[:, tq:]], axis=0)
        o_ref[0, p * tq:(p + 1) * tq, 2 * HEAD_DIM * kv:2 * HEAD_DIM * (kv + 1)] = ot.T.astype(BF16)


def _window_attn(qt, k, vt, kc, vtc, sink, *, n_pairs):
    b, w, s = qt.shape
    t = kc.shape[1]
    rows = n_pairs * 2 * BLOCK
    n_blocks = s // BLOCK
    n_kblk = 2 * n_pairs + 2
    assert s % rows == 0

    def blk_idx(i, j):
        return jnp.clip(i * 2 * n_pairs - 1 + j, 0, n_blocks - 1)

    k_specs = [pl.BlockSpec((1, BLOCK, KV_W), functools.partial(lambda bi, i, j: (bi, blk_idx(i, j), 0), j=j))
               for j in range(n_kblk)]
    vt_specs = [pl.BlockSpec((1, KV_W, BLOCK), functools.partial(lambda bi, i, j: (bi, 0, blk_idx(i, j)), j=j))
                for j in range(n_kblk)]
    return pl.pallas_call(
        functools.partial(_window_kernel, n_pairs=n_pairs, n_blocks=n_blocks),
        grid=(b, s // rows),
        in_specs=[pl.BlockSpec(memory_space=pltpu.SMEM),
                  pl.BlockSpec((1, w, rows), lambda bi, i: (bi, 0, i))] + k_specs + vt_specs +
                 [pl.BlockSpec((1, t, KV_W), lambda bi, i: (bi, 0, 0)),
                  pl.BlockSpec((1, KV_W, t), lambda bi, i: (bi, 0, 0))],
        out_specs=pl.BlockSpec((1, rows, w), lambda bi, i: (bi, i, 0)),
        out_shape=jax.ShapeDtypeStruct((b, s, w), BF16),
        compiler_params=_cparams(("parallel", "arbitrary")),
        name="window_attn",
    )(sink.reshape(1, N_HEADS).astype(F32), qt, *([k] * n_kblk), *([vt] * n_kblk), kc, vtc)


def _na_kernel(q_ref, k_ref, v_ref, kc_ref, vc_ref, bias_ref, o_ref, *, rows, n_rows):
    step = pl.program_id(1)
    span = NA_ROWS * GRID_W
    ctx_part = (kc_ref[0], vc_ref[0], None)
    items = []
    for j in range(n_rows):
        r = step * n_rows + j
        r0 = jnp.clip(r - NA_ROWS // 2, 0, rows - NA_ROWS)
        start = pl.multiple_of(r0 * GRID_W, GRID_W)
        kw = k_ref[0, pl.ds(start, span), :]
        vw = v_ref[0, pl.ds(start, span), :]
        qs = _stack_heads(q_ref[0, j * GRID_W:(j + 1) * GRID_W, :])
        off0 = r0 - r + (NA_ROWS - 1)
        bias = jnp.concatenate([bias_ref[off0 + 2 * i] for i in range(NA_ROWS // 2)], axis=1)
        items.append((qs, [(kw, vw, bias), ctx_part], None))
    for j, o in enumerate(_attend_many(items, wave=NA_ROWS_PER_WAVE)):
        o_ref[0, j * GRID_W:(j + 1) * GRID_W, :] = _unstack_heads(o, GRID_W).astype(BF16)


def _na_bias_table(rpb):
    n_h, n_ro, n_co = rpb.shape
    q = np.arange(GRID_W)
    kc = np.arange(GRID_W)
    coff = np.clip(kc[None, :] - q[:, None], -(NA_COLS - 1), NA_COLS - 1) + (NA_COLS - 1)
    c0 = np.clip(q - NA_COLS // 2, 0, GRID_W - NA_COLS)
    ok = (kc[None, :] >= c0[:, None]) & (kc[None, :] < c0[:, None] + NA_COLS)
    onehot = np.zeros((n_co, GRID_W, GRID_W), np.float32)
    onehot[coff, q[:, None], kc[None, :]] = 1.0
    tab = jnp.einsum("hrc,cqk->hrqk", rpb.astype(F32), jnp.asarray(onehot), precision=lax.Precision.HIGHEST)
    tab = jnp.where(ok[None, None], tab, NEG_INF)
    pairs = jnp.concatenate([tab[:, :-1], tab[:, 1:]], axis=-1)
    return pairs.transpose(1, 0, 2, 3).reshape(n_ro - 1, n_h * GRID_W, 2 * GRID_W)


def _na_attn(q, k, v, kc, vc, bias_tab, *, n_rows):
    b, s, w = q.shape
    t = kc.shape[1]
    rows = s // GRID_W
    assert s % GRID_W == 0 and rows >= NA_ROWS and rows % n_rows == 0
    full = lambda n: pl.BlockSpec((1, n, w), lambda bi, i: (bi, 0, 0))
    blk = n_rows * GRID_W
    return pl.pallas_call(
        functools.partial(_na_kernel, rows=rows, n_rows=n_rows),
        grid=(b, rows // n_rows),
        in_specs=[pl.BlockSpec((1, blk, w), lambda bi, r: (bi, r, 0)),
                  full(s), full(s), full(t), full(t), _const_spec(bias_tab.shape)],
        out_specs=pl.BlockSpec((1, blk, w), lambda bi, r: (bi, r, 0)),
        out_shape=jax.ShapeDtypeStruct((b, s, w), BF16),
        compiler_params=_cparams(("parallel", "arbitrary")),
        name="na_attn",
    )(q, k, v, kc, vc, bias_tab)


def _ctx_attn_kernel(sink_ref, qa_ref, ka_ref, va_ref, qb_ref, kb_ref, vb_ref, qd_ref, kd_ref, vd_ref,
                     oa_ref, ob_ref, od_ref):
    t = qa_ref.shape[1]
    sink_col = jnp.concatenate([jnp.full((t, 1), sink_ref[0, h], F32) for h in range(N_HEADS)], axis=0)
    groups = ((qa_ref, ka_ref, va_ref, sink_col), (qb_ref, kb_ref, vb_ref, None), (qd_ref, kd_ref, vd_ref, None))
    outs = _attend_many([(_stack_heads(q_ref[0]), [(k_ref[0], v_ref[0], None)], sink)
                         for q_ref, k_ref, v_ref, sink in groups])
    for o_ref, o in zip((oa_ref, ob_ref, od_ref), outs):
        o_ref[0] = _unstack_heads(o, t).astype(BF16)


def _ctx_attn(sink, qa, ka, va, qb, kb, vb, qd, kd, vd):
    b, t, w = qa.shape
    blk = pl.BlockSpec((1, t, w), lambda bi: (bi, 0, 0))
    out = jax.ShapeDtypeStruct((b, t, w), BF16)
    return pl.pallas_call(
        _ctx_attn_kernel,
        grid=(b,),
        in_specs=[pl.BlockSpec(memory_space=pltpu.SMEM)] + [blk] * 9,
        out_specs=[blk] * 3,
        out_shape=[out] * 3,
        compiler_params=_cparams(("arbitrary",)),
        name="ctx_attn",
    )(sink.reshape(1, N_HEADS).astype(F32), qa, ka, va, qb, kb, vb, qd, kd, vd)


def _global_kernel(q_ref, k_ref, vt_ref, kc_ref, vtc_ref, o_ref, *, tq, kt):
    kv = pl.program_id(1)
    n_blk = q_ref.shape[2] // tq
    sel = (lax.broadcasted_iota(jnp.int32, (2 * HEAD_DIM, tq), 0) // HEAD_DIM) == kv

    def block_rhs(blk):
        qt = q_ref[0, :, blk * tq:(blk + 1) * tq]
        zero = jnp.zeros_like(qt)

        def padded(g):
            qg = qt[g * HEAD_DIM:(g + 1) * HEAD_DIM, :]
            return jnp.where(sel, jnp.concatenate([qg, qg], axis=0), zero)

        return jnp.concatenate([padded(0), padded(1)], axis=1)

    rhs = [block_rhs(blk) for blk in range(n_blk)]
    n_lat = k_ref.shape[1] // kt
    n_tiles = n_lat + kc_ref.shape[1] // kt
    ones = jnp.ones((ONES_ROWS, kt), BF16)

    def k_tile(i):
        return k_ref[0, i * kt:(i + 1) * kt, :] if i < n_lat else kc_ref[0, (i - n_lat) * kt:(i - n_lat + 1) * kt, :]

    def vt_tile(i):
        vt = (vt_ref[0, :, i * kt:(i + 1) * kt] if i < n_lat
              else vtc_ref[0, :, (i - n_lat) * kt:(i - n_lat + 1) * kt])
        return jnp.concatenate([vt, ones], axis=0)

    scores, tile_max, run_max, probs = {}, {}, {}, {}
    acc = [None] * n_blk
    n_items = n_blk * n_tiles
    for i in range(n_items + 2):
        if i < n_items:
            blk, c = divmod(i, n_tiles)
            scores[i] = _dot(k_tile(c), rhs[blk])
            tile_max[i] = jnp.max(scores[i], axis=0, keepdims=True)
        j = i - 1
        if 0 <= j < n_items:
            first = j % n_tiles == 0
            run_max[j] = tile_max[j] if first else jnp.maximum(run_max[j - 1], tile_max[j])
            probs[j] = jnp.exp2(scores.pop(j) - run_max[j]).astype(BF16)
        j = i - 2
        if 0 <= j < n_items:
            blk, c = divmod(j, n_tiles)
            pv = _dot(vt_tile(c), probs.pop(j))
            acc[blk] = pv if c == 0 else acc[blk] * jnp.exp2(run_max[j - 1] - run_max[j]) + pv
            if c == n_tiles - 1:
                o = acc[blk][0:HEAD_DIM] * (1.0 / acc[blk][HEAD_DIM:HEAD_DIM + 1])
                ot = jnp.concatenate([o[:, :tq], o[:, tq:]], axis=0)
                o_ref[0, blk * tq:(blk + 1) * tq, :] = ot.T.astype(BF16)


def _global_attn(qt, k, vt, kc, vtc, *, tq, kt, n_blk):
    b, w, s = qt.shape
    t = kc.shape[1]
    rows = n_blk * tq
    assert s % rows == 0 and s % kt == 0 and t % kt == 0
    return pl.pallas_call(
        functools.partial(_global_kernel, tq=tq, kt=kt),
        grid=(b, N_KV, s // rows),
        in_specs=[pl.BlockSpec((1, 2 * HEAD_DIM, rows), lambda bi, kv, i: (bi, kv, i)),
                  pl.BlockSpec((1, s, KV_W), lambda bi, kv, i: (bi, 0, 0)),
                  pl.BlockSpec((1, HEAD_DIM, s), lambda bi, kv, i: (bi, kv, 0)),
                  pl.BlockSpec((1, t, KV_W), lambda bi, kv, i: (bi, 0, 0)),
                  pl.BlockSpec((1, HEAD_DIM, t), lambda bi, kv, i: (bi, kv, 0))],
        out_specs=pl.BlockSpec((1, rows, 2 * HEAD_DIM), lambda bi, kv, i: (bi, i, kv)),
        out_shape=jax.ShapeDtypeStruct((b, s, w), BF16),
        compiler_params=_cparams(("parallel", "arbitrary", "arbitrary")),
        name="global_attn",
    )(qt, k, vt, kc, vtc)


def _hgrn_constants():
    n = HGRN_L
    tm = np.zeros((2, 2 * n, n), np.float32)
    mk = np.zeros((2, HGRN_LEVELS + 1, n, n), np.float32)
    for d in range(2):
        pos = np.arange(n) if d == 0 else n - 1 - np.arange(n)
        p, q = pos[:, None], pos[None, :]
        for li in range(HGRN_LEVELS):
            h = n >> (li + 1)
            same_blk = (p // (2 * h)) == (q // (2 * h))
            up_p, up_q = (p % (2 * h)) >= h, (q % (2 * h)) >= h
            mk[d, li] = same_blk & up_p & ~up_q
            if h == 2:
                same_half = same_blk & (up_p == up_q)
                tm[d, 0:n] = np.where(up_p, same_half & (q <= p), same_half & (q > p))
        mk[d, HGRN_LEVELS] = np.eye(n)
        tm[d, n:2 * n] = q <= p
    mk = np.tile(mk, (1, 1, N_HEADS, 1))
    return jnp.asarray(tm, BF16), jnp.asarray(mk, F32)


def _hgrn_seq_kernel(qf_ref, vf_ref, zf_ref, qb_ref, vb_ref, zb_ref, lb_ref, t_ref, mk_ref, bd_ref, *rest,
                     g_chunks, has_init, emit_state):
    rest = list(rest)
    init_ref = rest.pop(0) if has_init else None
    of_ref, ob_ref = rest.pop(0), rest.pop(0)
    so_ref = rest.pop(0) if emit_state else None
    st_ref = rest.pop(0)
    step = pl.program_id(1)
    n = HGRN_L

    @pl.when(step == 0)
    def _():
        st_ref[...] = init_ref[0] if has_init else jnp.zeros_like(st_ref)

    ins = ((qf_ref, vf_ref, zf_ref), (qb_ref, vb_ref, zb_ref))
    outs = (of_ref, ob_ref)
    items = [(d, (g if d == 0 else g_chunks - 1 - g) * n) for g in range(g_chunks) for d in (0, 1)]

    work = []
    for d, r0 in items:
        q = ins[d][0][0, r0:r0 + n, :]
        v = ins[d][1][0, r0:r0 + n, :]
        z = ins[d][2][0, r0:r0 + n, :]
        lb = lb_ref[d]
        f = lb + (1.0 - lb) * jax.nn.sigmoid(z)
        g = jnp.log(f)
        g_hi = g.astype(BF16)
        g_lo = (g - g_hi.astype(F32)).astype(BF16)
        e = _dot(t_ref[d], g_hi) + _dot(t_ref[d], g_lo)
        work.append(dict(qq=_silu(q), kk=1.0 - f, f=f, v=v, e2=e[0:n], b=e[n:2 * n],
                         a=jnp.zeros((N_HEADS * n, n), F32)))

    row = lax.broadcasted_iota(jnp.int32, (n, GROUP_W), 0)

    def level_scale(w, d, h):
        if h == 2:
            return jnp.exp(w["e2"])
        b = w["b"]
        blocks = []
        for j in range(n // (2 * h)):
            m = 2 * h * j + (h - 1 if d == 0 else h)
            blocks.append(jnp.broadcast_to(b[m:m + 1, :], (2 * h, GROUP_W)))
        bm = blocks[0] if len(blocks) == 1 else jnp.concatenate(blocks, axis=0)
        later = ((row // h) % 2) == (1 if d == 0 else 0)
        return jnp.exp(jnp.where(later, b - bm, bm - b))

    for li in range(HGRN_LEVELS + 1):
        h = n >> (li + 1)
        for (d, _), w in zip(items, work):
            if h >= 2:
                sc = level_scale(w, d, h)
                qs, ks = w["qq"] * sc, w["kk"] * sc
            elif h == 1:
                qs, ks = w["qq"] * w["f"], w["kk"]
            else:
                qs, ks = w["qq"], w["kk"]
            w["a"] = w["a"] + _dot_nt(_stack_heads(qs.astype(BF16)), ks.astype(BF16)) * mk_ref[d, li]

    for (d, _), w in zip(items, work):
        b_incl = w["b"]
        b_tot = b_incl[n - 1:n] if d == 0 else b_incl[0:1]
        vb = w["v"].astype(BF16)
        w["q_hat"] = (w["qq"] * jnp.exp(b_incl)).astype(BF16)
        k_hat = (w["kk"] * jnp.exp(b_tot - b_incl)).astype(BF16)
        w["decay"] = jnp.exp(b_tot)
        w["o"] = _unstack_heads(_dot(w["a"].astype(BF16), vb), n)
        w["upd"] = _dot(w["v"].T.astype(BF16), k_hat) * bd_ref[...]

    st = [st_ref[0], st_ref[1]]
    for (d, r0), w in zip(items, work):
        outs[d][0, r0:r0 + n, :] = w["o"] + _dot_nt(w["q_hat"], st[d].astype(BF16))
        st[d] = st[d] * w["decay"] + w["upd"]
    st_ref[0] = st[0]
    st_ref[1] = st[1]
    if emit_state:
        so_ref[0, 0] = st[0]
        so_ref[0, 1] = st[1]


def _hgrn_seq(c_raw, lb, consts, init_state, *, g_chunks, emit_state):
    tmat, masks, bd = consts
    b, s, _ = c_raw.shape
    rows = g_chunks * HGRN_L
    assert s % rows == 0
    n_steps = s // rows
    has_init = init_state is not None
    fwd = lambda col: pl.BlockSpec((1, rows, GROUP_W), lambda bi, st: (bi, st, col))
    bwd = lambda col: pl.BlockSpec((1, rows, GROUP_W), lambda bi, st: (bi, n_steps - 1 - st, col))
    state_spec = pl.BlockSpec((1, 2, GROUP_W, GROUP_W), lambda bi, st: (bi, 0, 0, 0))
    in_specs = [fwd(0), fwd(1), fwd(2), bwd(0), bwd(1), bwd(3),
                _const_spec((2, 1, GROUP_W)), _const_spec(tmat.shape), _const_spec(masks.shape),
                _const_spec(bd.shape)]
    args = [c_raw] * 6 + [lb.reshape(2, 1, GROUP_W), tmat, masks, bd]
    if has_init:
        in_specs.append(state_spec)
        args.append(init_state)
    out_specs = [fwd(0), bwd(0)]
    out_shape = [jax.ShapeDtypeStruct((b, s, GROUP_W), F32)] * 2
    if emit_state:
        out_specs.append(state_spec)
        out_shape.append(jax.ShapeDtypeStruct((b, 2, GROUP_W, GROUP_W), F32))
    return pl.pallas_call(
        functools.partial(_hgrn_seq_kernel, g_chunks=g_chunks, has_init=has_init, emit_state=emit_state),
        grid=(b, n_steps),
        in_specs=in_specs,
        out_specs=out_specs,
        out_shape=out_shape,
        scratch_shapes=[pltpu.VMEM((2, GROUP_W, GROUP_W), F32)],
        compiler_params=_cparams(("parallel", "arbitrary")),
        name="hgrn_ctx" if emit_state else "hgrn",
    )(*args)


def _outproj_kernel(x_ref, oa_ref, ob_ref, of_ref, obw_ref, gate_ref, od_ref, w_ref, onorm_ref, bd_ref,
                    ga_ref, lng_ref, lnb_ref, o_ref, *, alpha):
    oc = of_ref[0] + obw_ref[0]
    y = oc * lax.rsqrt(_head_mean_sq(oc, bd_ref[...]) + EPS) * onorm_ref[...]
    y = y * _silu(gate_ref[0])
    mix = jnp.concatenate([oa_ref[0], ob_ref[0], y.astype(BF16), od_ref[0]], axis=1)
    z = alpha * x_ref[0] + ga_ref[0] * _dot(mix, w_ref[0])
    o_ref[0] = _layer_norm(z, lng_ref[...], lnb_ref[...])


def _outproj(x, oa, ob, oc_f, oc_b, c_raw, od, w_out, layer, onorm, bd256, gate_a, ln_g, ln_b, *, alpha, tm):
    b, s, d = x.shape
    assert s % tm == 0
    grp = pl.BlockSpec((1, tm, GROUP_W), lambda bi, i: (bi, i, 0))
    vec = pl.BlockSpec((1, 1, d), lambda bi, i: (bi, 0, 0))
    return pl.pallas_call(
        functools.partial(_outproj_kernel, alpha=alpha),
        grid=(b, s // tm),
        in_specs=[pl.BlockSpec((1, tm, d), lambda bi, i: (bi, i, 0)), grp, grp, grp, grp,
                  pl.BlockSpec((1, tm, GROUP_W), lambda bi, i: (bi, i, 4)),
                  grp, _layer_spec(w_out.shape, layer), _const_spec(onorm.shape), _const_spec(bd256.shape),
                  vec, _const_spec(ln_g.shape), _const_spec(ln_b.shape)],
        out_specs=pl.BlockSpec((1, tm, d), lambda bi, i: (bi, i, 0)),
        out_shape=jax.ShapeDtypeStruct((b, s, d), F32),
        compiler_params=_cparams(("parallel", "arbitrary")),
        name="outproj",
    )(x, oa, ob, oc_f, oc_b, c_raw, od, w_out, onorm, bd256, gate_a, ln_g, ln_b)


def _ffn_kernel(h_ref, hp_ref, hn_ref, sc_ref, sh_ref, gt_ref, wup_ref, cw_ref, cb_ref, wdn_ref,
                lng_ref, lnb_ref, o_ref, g_scr, *, alpha, d_ff, cwid, n_tiles):
    i = pl.program_id(1)
    tm = h_ref.shape[1]
    sc = 1.0 + sc_ref[0]
    sh = sh_ref[0]
    h = h_ref[0]
    halo = jnp.concatenate([hp_ref[0], hn_ref[0]], axis=0)
    lhs = jnp.concatenate([(h * sc + sh).astype(BF16), (halo * sc + sh).astype(BF16)], axis=0)
    has_prev = (i > 0).astype(F32)
    has_next = (i < n_tiles - 1).astype(F32)
    row = lax.broadcasted_iota(jnp.int32, (tm, cwid), 0)
    first, last = row == 0, row == tm - 1

    def conv_cols(c0):
        u = _dot(lhs, wup_ref[0, :, c0:c0 + cwid])
        um = u[:tm]
        u_prev = u[tm + 7:tm + 8] * has_prev
        u_next = u[tm + 8:tm + 9] * has_next
        dn = jnp.where(first, u_prev, pltpu.roll(um, 1, 0))
        up = jnp.where(last, u_next, pltpu.roll(um, tm - 1, 0))
        w = cw_ref[:, c0:c0 + cwid]
        return dn * w[0:1] + um * w[1:2] + up * w[2:3] + cb_ref[:, c0:c0 + cwid]

    for j in range(d_ff // cwid):
        a = conv_cols(j * cwid)
        v = conv_cols(d_ff + j * cwid)
        g_scr[:, j * cwid:(j + 1) * cwid] = (_silu(a) * v).astype(BF16)
    z = alpha * h + gt_ref[0] * _dot(g_scr[...], wdn_ref[0])
    o_ref[0] = _layer_norm(z, lng_ref[...], lnb_ref[...])


def _ffn(h, scale, shift, gate, w_up, conv_w, conv_b, w_down, layer, ln_g, ln_b, *, alpha, tm, cwid):
    b, s, d = h.shape
    d_ff = w_down.shape[1]
    assert s % tm == 0 and tm % 16 == 0 and d_ff % cwid == 0
    n_tiles = s // tm
    r8 = tm // 8
    vec = pl.BlockSpec((1, 1, d), lambda bi, i: (bi, 0, 0))
    return pl.pallas_call(
        functools.partial(_ffn_kernel, alpha=alpha, d_ff=d_ff, cwid=cwid, n_tiles=n_tiles),
        grid=(b, n_tiles),
        in_specs=[pl.BlockSpec((1, tm, d), lambda bi, i: (bi, i, 0)),
                  pl.BlockSpec((1, 8, d), lambda bi, i: (bi, jnp.maximum(i * r8 - 1, 0), 0)),
                  pl.BlockSpec((1, 8, d), lambda bi, i: (bi, jnp.minimum((i + 1) * r8, s // 8 - 1), 0)),
                  vec, vec, vec,
                  _layer_spec(w_up.shape, layer), _const_spec(conv_w.shape), _const_spec(conv_b.shape),
                  _layer_spec(w_down.shape, layer), _const_spec(ln_g.shape), _const_spec(ln_b.shape)],
        out_specs=pl.BlockSpec((1, tm, d), lambda bi, i: (bi, i, 0)),
        out_shape=jax.ShapeDtypeStruct((b, s, d), F32),
        scratch_shapes=[pltpu.VMEM((tm, d_ff), BF16)],
        compiler_params=_cparams(("parallel", "arbitrary")),
        name="conv_ffn",
    )(h, h, h, scale, shift, gate, w_up, conv_w, conv_b, w_down, ln_g, ln_b)


def _rope_tables(n, with_rotation):
    lane = np.arange(V7X_LANES)
    if not with_rotation:
        one = jnp.ones((n, V7X_LANES), F32)
        zero = jnp.zeros((n, V7X_LANES), F32)
        return one, zero, zero
    t = np.arange(n)
    pos = np.stack([t // GRID_W, t % GRID_W], -1).astype(np.float64)
    n_freq = HEAD_DIM // 4
    inv = ROPE_BASE ** (-np.arange(n_freq, dtype=np.float64) / n_freq)
    ang = pos[:, :, None] * inv
    axis = (lane % HEAD_DIM) // (HEAD_DIM // 2)
    freq = lane % n_freq
    ang_l = ang[:, axis, freq]
    first = (lane % (2 * n_freq)) < n_freq
    cos, sin = np.cos(ang_l), np.sin(ang_l)
    tabs = (cos, np.where(first, -sin, 0.0), np.where(first, 0.0, sin))
    return tuple(jnp.asarray(a, F32) for a in tabs)


def _block_diag_mean(width):
    lane = np.arange(width)
    same = (lane[:, None] // HEAD_DIM) == (lane[None, :] // HEAD_DIM)
    return jnp.asarray(same / HEAD_DIM, BF16)


def kernel(x, c, ctx, c_ctx, w_mod, b_mod, w_in, w_out, ln_g, ln_b, attn_sink, qk_norm, hgrn_lb_logits,
           hgrn_onorm, na_rpb, ffn_w_up, ffn_conv_w, ffn_conv_b, ffn_w_down):
    depth = w_mod.shape[0]
    b, s, d = x.shape
    t = ctx.shape[1]
    alpha = (2 * depth) ** 0.25

    mods = _modulation(c, c_ctx, w_mod, b_mod)
    rope_lat = _rope_tables(s, True)
    rope_ctx = _rope_tables(t, False)
    bd128, bd256 = _block_diag_mean(KV_W), _block_diag_mean(GROUP_W)
    lane = np.arange(GROUP_W)
    head_bd = jnp.asarray((lane[:, None] // HEAD_DIM) == (lane[None, :] // HEAD_DIM), F32)
    hgrn_consts = _hgrn_constants() + (head_bd,)
    p_lb = jax.nn.softmax(hgrn_lb_logits.astype(F32), axis=1)
    lb_all = jnp.cumsum(p_lb, axis=1) - p_lb[:, :1]
    w_in_b, w_out_b = w_in.astype(BF16), w_out.astype(BF16)
    w_up_b, w_dn_b = ffn_w_up.astype(BF16), ffn_w_down.astype(BF16)

    h_x, h_c = x, ctx
    for l in range(depth):
        need_ctx = l < depth - 1
        mod = mods[l]
        sa, ca, ga, sf, cf, gf = (mod[:, i * d:(i + 1) * d] for i in range(6))
        lat = lambda a: a[:b, None, :]
        con = lambda a: jnp.broadcast_to(a[b][None, None, :], (b, 1, d))
        gq = jnp.tile(qk_norm[l, 0], 2)[None, :]
        gk = jnp.tile(qk_norm[l, 1], 2)[None, :]

        px = _inproj(h_x, lat(sa), lat(ca), w_in_b, l, rope_lat, gq, gk, bd128, is_ctx=False, tm=1024)
        pc = _inproj(h_c, con(sa), con(ca), w_in_b, l, rope_ctx, gq, gk, bd128, is_ctx=True, tm=t)
        c_raw, c_raw_c = px["c"], pc["c"]

        o_a = _window_attn(px["qa"], px["ka"], px["vat"], pc["ka"], pc["vat"], attn_sink[l],
                           n_pairs=WINDOW_PAIRS_PER_STEP)

        o_b = _global_attn(px["qb"], px["kb"], px["vbt"], pc["kb"], pc["vbt"], tq=256, kt=256,
                           n_blk=GLOBAL_BLOCKS_PER_STEP)

        ocf_c, ocb_c, ctx_state = _hgrn_seq(c_raw_c, lb_all[:, l], hgrn_consts, None,
                                            g_chunks=t // HGRN_L, emit_state=True)
        oc_f, oc_b = _hgrn_seq(c_raw, lb_all[:, l], hgrn_consts, ctx_state,
                               g_chunks=HGRN_CHUNKS_PER_STEP, emit_state=False)

        o_d = _na_attn(px["qd"], px["kd"], px["vd"], pc["kd"], pc["vd"], _na_bias_table(na_rpb[l]),
                       n_rows=NA_ROWS_PER_STEP)

        onorm = hgrn_onorm[l][None, :]
        lng0, lnb0 = ln_g[l, 0][None, :], ln_b[l, 0][None, :]
        lng1, lnb1 = ln_g[l, 1][None, :], ln_b[l, 1][None, :]
        cw, cb = ffn_conv_w[l], ffn_conv_b[l][None, :]
        h_x = _outproj(h_x, o_a, o_b, oc_f, oc_b, c_raw, o_d, w_out_b, l, onorm, bd256, lat(ga), lng0, lnb0,
                       alpha=alpha, tm=1024)
        h_x = _ffn(h_x, lat(cf), lat(sf), lat(gf), w_up_b, cw, cb, w_dn_b, l, lng1, lnb1,
                   alpha=alpha, tm=512, cwid=256)
        if need_ctx:
            oa_c, ob_c, od_c = _ctx_attn(attn_sink[l], pc["qa"], pc["kax"], pc["vax"], pc["qb"], pc["kbx"],
                                         pc["vbx"], pc["qd"], pc["kd"], pc["vd"])
            h_c = _outproj(h_c, oa_c, ob_c, ocf_c, ocb_c, c_raw_c, od_c, w_out_b, l, onorm, bd256, con(ga),
                           lng0, lnb0, alpha=alpha, tm=t)
            h_c = _ffn(h_c, con(cf), con(sf), con(gf), w_up_b, cw, cb, w_dn_b, l, lng1, lnb1,
                       alpha=alpha, tm=t, cwid=256)
    return h_x
```

```python
import functools

import numpy as np
import jax
import jax.numpy as jnp
from jax import lax
from jax.experimental import pallas as pl
from jax.experimental.pallas import tpu as pltpu

F32 = jnp.float32
BF16 = jnp.bfloat16

HEAD_DIM = 64
N_HEADS = 4
N_KV = 2
GROUP_W = N_HEADS * HEAD_DIM
KV_W = N_KV * HEAD_DIM
GRID_W = 64
WIN = 128
BLOCK = 128
NA_ROWS = 8
NA_COLS = 16
ROPE_BASE = 10000.0
EPS = 1e-6
HGRN_L = 64
HGRN_LEVELS = 6
HGRN_CHUNKS_PER_STEP = 8
HGRN_ITEMS_PER_WAVE = 4
WINDOW_PAIRS_PER_STEP = 4
WINDOW_ITEMS_PER_WAVE = 4
NA_ROWS_PER_STEP = 8
NA_ROWS_PER_WAVE = 2
GLOBAL_BLOCKS_PER_STEP = 4
OUTPROJ_SUB_ROWS = 256
ONES_ROWS = 16

V7X_LANES = 128
V7X_VMEM_LIMIT_BYTES = 56 * 1024 * 1024

COL_QA, COL_KA, COL_VA = 0, 256, 384
COL_QB, COL_KB, COL_VB = 512, 768, 896
COL_C = 1024
COL_QD, COL_KD, COL_VD = 2304, 2560, 2816
IN_WIDTH = 3072
C_WIDTH = 5 * GROUP_W

NEG_INF = float("-inf")
LOG2_E = 1.4426950408889634


def _cparams(sem):
    return pltpu.CompilerParams(dimension_semantics=sem, vmem_limit_bytes=V7X_VMEM_LIMIT_BYTES)


def _const_spec(shape):
    nd = len(shape)
    return pl.BlockSpec(shape, lambda *_: (0,) * nd, pipeline_mode=pl.Buffered(1))


def _layer_spec(shape, layer):
    nd = len(shape)
    return pl.BlockSpec((1,) + tuple(shape[1:]), lambda *_: (layer,) + (0,) * (nd - 1),
                        pipeline_mode=pl.Buffered(1))


def _silu(x):
    return x * jax.nn.sigmoid(x)


def _dot(a, b):
    return jnp.dot(a, b, preferred_element_type=F32)


def _dot_nt(a, b):
    return lax.dot_general(a, b, (((1,), (1,)), ((), ())), preferred_element_type=F32)


def _head_mask(shape, h):
    lane = lax.broadcasted_iota(jnp.int32, shape, len(shape) - 1)
    return (lane // HEAD_DIM) == h


def _stack_heads(q):
    zero = jnp.zeros_like(q)
    return jnp.concatenate([jnp.where(_head_mask(q.shape, h), q, zero) for h in range(N_HEADS)], axis=0)


def _unstack_heads(o, r):
    out = jnp.zeros((r, o.shape[1]), o.dtype)
    for h in range(N_HEADS):
        blk = o[h * r:(h + 1) * r]
        out = jnp.where(_head_mask(blk.shape, h), blk, out)
    return out


def _head_mean_sq(x, bd):
    sq = x * x
    hi = sq.astype(BF16)
    lo = (sq - hi.astype(F32)).astype(BF16)
    return _dot(hi, bd) + _dot(lo, bd)


def _layer_norm(z, g, b):
    mu = jnp.mean(z, axis=-1, keepdims=True)
    zc = z - mu
    var = jnp.mean(zc * zc, axis=-1, keepdims=True)
    return zc * lax.rsqrt(var + EPS) * g + b


def _mod_kernel(s_ref, w_ref, b_ref, o_ref):
    s = _silu(s_ref[...])
    o_ref[0] = jnp.dot(s, w_ref[0], preferred_element_type=F32,
                       precision=lax.Precision.HIGHEST) + b_ref[0]


def _modulation(c, c_ctx, w_mod, b_mod):
    depth, d, d6 = w_mod.shape
    b = c.shape[0]
    rows = 8
    assert b < rows
    s_in = jnp.zeros((rows, d), F32).at[:b].set(c).at[b].set(c_ctx)
    tn = d6 // 4
    return pl.pallas_call(
        _mod_kernel,
        grid=(depth, d6 // tn),
        in_specs=[pl.BlockSpec((rows, d), lambda l, j: (0, 0)),
                  pl.BlockSpec((1, d, tn), lambda l, j: (l, 0, j)),
                  pl.BlockSpec((1, 1, tn), lambda l, j: (l, 0, j))],
        out_specs=pl.BlockSpec((1, rows, tn), lambda l, j: (l, 0, j)),
        out_shape=jax.ShapeDtypeStruct((depth, rows, d6), F32),
        compiler_params=_cparams(("arbitrary", "arbitrary")),
        name="modulation",
    )(s_in, w_mod, b_mod.reshape(depth, 1, d6))


def _dup_heads(x):
    r = pltpu.roll(x, HEAD_DIM, 1)
    lo = lax.broadcasted_iota(jnp.int32, x.shape, 1) < HEAD_DIM
    return jnp.concatenate([jnp.where(lo, x, r), jnp.where(lo, r, x)], axis=1)


def _inproj_outputs(is_ctx):
    gqa = lambda g: [("q" + g, "row" if is_ctx else "col", GROUP_W, BF16), ("k" + g, "row", KV_W, BF16),
                     ("v" + g + "t", "col", KV_W, BF16)]
    dup = lambda g: [("k" + g + "x", "row", GROUP_W, BF16), ("v" + g + "x", "row", GROUP_W, BF16)]
    outs = gqa("a") + gqa("b")
    if is_ctx:
        outs += dup("a") + dup("b")
    return outs + [("c", "row", C_WIDTH, F32), ("qd", "row", GROUP_W, BF16), ("kd", "row", GROUP_W, BF16),
                   ("vd", "row", GROUP_W, BF16)]


def _inproj_kernel(h_ref, sc_ref, sh_ref, w_ref, rc_ref, ra_ref, rb_ref, gq_ref, gk_ref, bd_ref,
                   *outs, is_ctx):
    o = {name: ref for (name, _, _, _), ref in zip(_inproj_outputs(is_ctx), outs)}
    hm = (h_ref[0] * (1.0 + sc_ref[0]) + sh_ref[0]).astype(BF16)
    rc, ra, rb = rc_ref[...], ra_ref[...], rb_ref[...]
    bd = bd_ref[...]
    q_scale = HEAD_DIM ** -0.5 * (1.0 if is_ctx else LOG2_E)

    def proj(c0, width):
        return _dot(hm, w_ref[0, :, c0:c0 + width])

    def rope(x):
        return x * rc + pltpu.roll(x, V7X_LANES - 16, 1) * ra + pltpu.roll(x, 16, 1) * rb

    def rms(x, gain):
        return x * lax.rsqrt(_head_mean_sq(x, bd) + EPS) * gain

    def gqa_group(g, c_q, c_k, c_v, q_fn, k_fn):
        xq = proj(c_q, GROUP_W)
        for j in range(2):
            sl = slice(j * V7X_LANES, (j + 1) * V7X_LANES)
            y = rope(q_fn(xq[:, sl])) * q_scale
            if is_ctx:
                o["q" + g][0, :, sl] = y.astype(BF16)
            else:
                o["q" + g][0, sl, :] = y.T.astype(BF16)
        k = rope(k_fn(proj(c_k, KV_W)))
        v = proj(c_v, KV_W)
        o["k" + g][0] = k.astype(BF16)
        o["v" + g + "t"][0] = v.T.astype(BF16)
        if is_ctx:
            o["k" + g + "x"][0] = _dup_heads(k).astype(BF16)
            o["v" + g + "x"][0] = _dup_heads(v).astype(BF16)

    gq, gk = gq_ref[...], gk_ref[...]
    gqa_group("a", COL_QA, COL_KA, COL_VA, lambda x: x, lambda x: x)
    gqa_group("b", COL_QB, COL_KB, COL_VB, lambda x: rms(x, gq), lambda x: rms(x, gk))

    for j in range(C_WIDTH // GROUP_W):
        o["c"][0, :, j * GROUP_W:(j + 1) * GROUP_W] = proj(COL_C + j * GROUP_W, GROUP_W)

    o["qd"][0] = (proj(COL_QD, GROUP_W) * HEAD_DIM ** -0.5).astype(BF16)
    o["kd"][0] = proj(COL_KD, GROUP_W).astype(BF16)
    o["vd"][0] = proj(COL_VD, GROUP_W).astype(BF16)


def _inproj(h, shift, scale, w_in, layer, rope_tabs, gq, gk, bd128, *, is_ctx, tm):
    b, s, d = h.shape
    assert s % tm == 0
    rc, ra, rb = rope_tabs
    outs = _inproj_outputs(is_ctx)
    out_specs = [pl.BlockSpec((1, tm, w), lambda bi, i: (bi, i, 0)) if lay == "row"
                 else pl.BlockSpec((1, w, tm), lambda bi, i: (bi, 0, i)) for _, lay, w, _ in outs]
    out_shape = [jax.ShapeDtypeStruct((b, s, w) if lay == "row" else (b, w, s), dt) for _, lay, w, dt in outs]
    tab = pl.BlockSpec((tm, V7X_LANES), lambda bi, i: (i, 0))
    vec = pl.BlockSpec((1, 1, d), lambda bi, i: (bi, 0, 0))
    res = pl.pallas_call(
        functools.partial(_inproj_kernel, is_ctx=is_ctx),
        grid=(b, s // tm),
        in_specs=[pl.BlockSpec((1, tm, d), lambda bi, i: (bi, i, 0)), vec, vec,
                  _layer_spec(w_in.shape, layer), tab, tab, tab,
                  _const_spec(gq.shape), _const_spec(gk.shape), _const_spec(bd128.shape)],
        out_specs=out_specs,
        out_shape=out_shape,
        compiler_params=_cparams(("parallel", "arbitrary")),
        name="inproj_ctx" if is_ctx else "inproj",
    )(h, scale, shift, w_in, rc, ra, rb, gq, gk, bd128)
    return {name: r for (name, _, _, _), r in zip(outs, res)}


def _attend_many(items, wave=None):
    n = len(items)
    wave = wave or n
    scores, maxes, probs, dens, outs = {}, {}, {}, {}, {}

    def stage_scores(i):
        qs, parts, sink_col = items[i]
        ss, m = [], sink_col
        for k, _, bias in parts:
            s = _dot_nt(qs, k)
            if bias is not None:
                s = s + bias
            ss.append(s)
            smax = jnp.max(s, axis=1, keepdims=True)
            m = smax if m is None else jnp.maximum(m, smax)
        scores[i], maxes[i] = ss, m

    def stage_probs(i):
        sink_col, m = items[i][2], maxes[i]
        den = None if sink_col is None else jnp.exp(sink_col - m)
        ps = []
        for s in scores.pop(i):
            p = jnp.exp(s - m)
            psum = jnp.sum(p, axis=1, keepdims=True)
            den = psum if den is None else den + psum
            ps.append(p.astype(BF16))
        probs[i], dens[i] = ps, den

    def stage_pv(i):
        out = None
        for p, (_, v, _) in zip(probs.pop(i), items[i][1]):
            pv = _dot(p, v)
            out = pv if out is None else out + pv
        outs[i] = out * (1.0 / dens[i])

    n_waves = -(-n // wave)
    for w in range(n_waves + 2):
        for stage, lag in ((stage_scores, 0), (stage_probs, 1), (stage_pv, 2)):
            ww = w - lag
            if 0 <= ww < n_waves:
                for i in range(ww * wave, min((ww + 1) * wave, n)):
                    stage(i)
    return [outs[i] for i in range(n)]


def _window_kernel(sink_ref, q_ref, *refs, n_pairs, n_blocks):
    n_kblk = 2 * n_pairs + 2
    k_refs, vt_refs = refs[:n_kblk], refs[n_kblk:2 * n_kblk]
    kc_ref, vtc_ref, o_ref = refs[2 * n_kblk:]
    step = pl.program_id(1)
    tq = 2 * BLOCK
    first_blk = step * 2 * n_pairs - 1
    ones_blk = jnp.ones((ONES_ROWS, BLOCK), BF16)
    ones_ctx = jnp.ones((ONES_ROWS, kc_ref.shape[1]), BF16)
    ki = lax.broadcasted_iota(jnp.int32, (BLOCK, tq), 0)
    qi = lax.broadcasted_iota(jnp.int32, (BLOCK, tq), 1)
    lane = lax.broadcasted_iota(jnp.int32, (1, 2 * tq), 1)
    zeros_q = jnp.zeros((HEAD_DIM, tq), BF16)

    def band_bias(jj, j):
        ok = jnp.abs((1 - jj) * BLOCK + qi - ki) <= WIN
        if j == 0:
            ok = ok & (first_blk >= 0)
        elif j == n_kblk - 1:
            ok = ok & (first_blk + j < n_blocks)
        bias = jnp.where(ok, 0.0, NEG_INF).astype(F32)
        return jnp.concatenate([bias, bias], axis=1)

    items = [(p, kv) for p in range(n_pairs) for kv in range(N_KV)]
    biases = {(p, jj): band_bias(jj, 2 * p + jj) for p in range(n_pairs) for jj in range(4)}

    scores, maxes, probs = {}, {}, {}

    def stage_scores(it):
        p, kv = it
        qt = q_ref[0, 2 * HEAD_DIM * kv:2 * HEAD_DIM * (kv + 1), p * tq:(p + 1) * tq]

        def padded(g):
            qg = qt[g * HEAD_DIM:(g + 1) * HEAD_DIM, :]
            return jnp.concatenate([qg, zeros_q] if kv == 0 else [zeros_q, qg], axis=0)

        rhs = jnp.concatenate([padded(0), padded(1)], axis=1)
        ss = [_dot(k_refs[2 * p + jj][0], rhs) + biases[(p, jj)] for jj in range(4)]
        ss.append(_dot(kc_ref[0], rhs))
        sink2 = jnp.where(lane < tq, sink_ref[0, 2 * kv], sink_ref[0, 2 * kv + 1]) * LOG2_E
        m = sink2
        for s in ss:
            m = jnp.maximum(m, jnp.max(s, axis=0, keepdims=True))
        scores[it], maxes[it] = ss, (m, sink2)

    def stage_probs(it):
        m = maxes[it][0]
        probs[it] = [jnp.exp2(s - m).astype(BF16) for s in scores.pop(it)]

    def stage_pv(it):
        p, kv = it
        ps = probs.pop(it)
        acc = None
        for jj in range(4):
            vt = vt_refs[2 * p + jj][0, kv * HEAD_DIM:(kv + 1) * HEAD_DIM, :]
            pv = _dot(jnp.concatenate([vt, ones_blk], axis=0), ps[jj])
            acc = pv if acc is None else acc + pv
        vtc = vtc_ref[0, kv * HEAD_DIM:(kv + 1) * HEAD_DIM, :]
        acc = acc + _dot(jnp.concatenate([vtc, ones_ctx], axis=0), ps[4])
        m, sink2 = maxes[(p, kv)]
        den = acc[HEAD_DIM:HEAD_DIM + 1] + jnp.exp2(sink2 - m)
        o = acc[0:HEAD_DIM] * (1.0 / den)
        ot = jnp.concatenate([o[:, :tq], o[:, tq:]], axis=0)
        o_ref[0, p * tq:(p + 1) * tq, 2 * HEAD_DIM * kv:2 * HEAD_DIM * (kv + 1)] = ot.T.astype(BF16)

    wave = min(WINDOW_ITEMS_PER_WAVE, len(items))
    n_waves = len(items) // wave
    for w in range(n_waves + 2):
        for stage, lag in ((stage_scores, 0), (stage_probs, 1), (stage_pv, 2)):
            if 0 <= w - lag < n_waves:
                for it in items[(w - lag) * wave:(w - lag + 1) * wave]:
                    stage(it)


def _window_attn(qt, k, vt, kc, vtc, sink, *, n_pairs):
    b, w, s = qt.shape
    t = kc.shape[1]
    rows = n_pairs * 2 * BLOCK
    n_blocks = s // BLOCK
    n_kblk = 2 * n_pairs + 2
    assert s % rows == 0

    def blk_idx(i, j):
        return jnp.clip(i * 2 * n_pairs - 1 + j, 0, n_blocks - 1)

    k_specs = [pl.BlockSpec((1, BLOCK, KV_W), functools.partial(lambda bi, i, j: (bi, blk_idx(i, j), 0), j=j))
               for j in range(n_kblk)]
    vt_specs = [pl.BlockSpec((1, KV_W, BLOCK), functools.partial(lambda bi, i, j: (bi, 0, blk_idx(i, j)), j=j))
                for j in range(n_kblk)]
    return pl.pallas_call(
        functools.partial(_window_kernel, n_pairs=n_pairs, n_blocks=n_blocks),
        grid=(b, s // rows),
        in_specs=[pl.BlockSpec(memory_space=pltpu.SMEM),
                  pl.BlockSpec((1, w, rows), lambda bi, i: (bi, 0, i))] + k_specs + vt_specs +
                 [pl.BlockSpec((1, t, KV_W), lambda bi, i: (bi, 0, 0)),
                  pl.BlockSpec((1, KV_W, t), lambda bi, i: (bi, 0, 0))],
        out_specs=pl.BlockSpec((1, rows, w), lambda bi, i: (bi, i, 0)),
        out_shape=jax.ShapeDtypeStruct((b, s, w), BF16),
        compiler_params=_cparams(("parallel", "arbitrary")),
        name="window_attn",
    )(sink.reshape(1, N_HEADS).astype(F32), qt, *([k] * n_kblk), *([vt] * n_kblk), kc, vtc)


def _na_kernel(q_ref, k_ref, v_ref, kc_ref, vc_ref, bias_ref, o_ref, *, rows, n_rows):
    step = pl.program_id(1)
    span = NA_ROWS * GRID_W
    ctx_part = (kc_ref[0], vc_ref[0], None)
    items = []
    for j in range(n_rows):
        r = step * n_rows + j
        r0 = jnp.clip(r - NA_ROWS // 2, 0, rows - NA_ROWS)
        start = pl.multiple_of(r0 * GRID_W, GRID_W)
        kw = k_ref[0, pl.ds(start, span), :]
        vw = v_ref[0, pl.ds(start, span), :]
        qs = _stack_heads(q_ref[0, j * GRID_W:(j + 1) * GRID_W, :])
        off0 = r0 - r + (NA_ROWS - 1)
        bias = jnp.concatenate([bias_ref[off0 + 2 * i] for i in range(NA_ROWS // 2)], axis=1)
        items.append((qs, [(kw, vw, bias), ctx_part], None))
    for j, o in enumerate(_attend_many(items, wave=NA_ROWS_PER_WAVE)):
        o_ref[0, j * GRID_W:(j + 1) * GRID_W, :] = _unstack_heads(o, GRID_W).astype(BF16)


def _na_bias_table(rpb):
    n_h, n_ro, n_co = rpb.shape
    q = np.arange(GRID_W)
    kc = np.arange(GRID_W)
    coff = np.clip(kc[None, :] - q[:, None], -(NA_COLS - 1), NA_COLS - 1) + (NA_COLS - 1)
    c0 = np.clip(q - NA_COLS // 2, 0, GRID_W - NA_COLS)
    ok = (kc[None, :] >= c0[:, None]) & (kc[None, :] < c0[:, None] + NA_COLS)
    onehot = np.zeros((n_co, GRID_W, GRID_W), np.float32)
    onehot[coff, q[:, None], kc[None, :]] = 1.0
    tab = jnp.einsum("hrc,cqk->hrqk", rpb.astype(F32), jnp.asarray(onehot), precision=lax.Precision.HIGHEST)
    tab = jnp.where(ok[None, None], tab, NEG_INF)
    pairs = jnp.concatenate([tab[:, :-1], tab[:, 1:]], axis=-1)
    return pairs.transpose(1, 0, 2, 3).reshape(n_ro - 1, n_h * GRID_W, 2 * GRID_W)


def _na_attn(q, k, v, kc, vc, bias_tab, *, n_rows):
    b, s, w = q.shape
    t = kc.shape[1]
    rows = s // GRID_W
    assert s % GRID_W == 0 and rows >= NA_ROWS and rows % n_rows == 0
    full = lambda n: pl.BlockSpec((1, n, w), lambda bi, i: (bi, 0, 0))
    blk = n_rows * GRID_W
    return pl.pallas_call(
        functools.partial(_na_kernel, rows=rows, n_rows=n_rows),
        grid=(b, rows // n_rows),
        in_specs=[pl.BlockSpec((1, blk, w), lambda bi, r: (bi, r, 0)),
                  full(s), full(s), full(t), full(t), _const_spec(bias_tab.shape)],
        out_specs=pl.BlockSpec((1, blk, w), lambda bi, r: (bi, r, 0)),
        out_shape=jax.ShapeDtypeStruct((b, s, w), BF16),
        compiler_params=_cparams(("parallel", "arbitrary")),
        name="na_attn",
    )(q, k, v, kc, vc, bias_tab)


def _ctx_attn_kernel(sink_ref, qa_ref, ka_ref, va_ref, qb_ref, kb_ref, vb_ref, qd_ref, kd_ref, vd_ref,
                     oa_ref, ob_ref, od_ref):
    t = qa_ref.shape[1]
    sink_col = jnp.concatenate([jnp.full((t, 1), sink_ref[0, h], F32) for h in range(N_HEADS)], axis=0)
    groups = ((qa_ref, ka_ref, va_ref, sink_col), (qb_ref, kb_ref, vb_ref, None), (qd_ref, kd_ref, vd_ref, None))
    outs = _attend_many([(_stack_heads(q_ref[0]), [(k_ref[0], v_ref[0], None)], sink)
                         for q_ref, k_ref, v_ref, sink in groups])
    for o_ref, o in zip((oa_ref, ob_ref, od_ref), outs):
        o_ref[0] = _unstack_heads(o, t).astype(BF16)


def _ctx_attn(sink, qa, ka, va, qb, kb, vb, qd, kd, vd):
    b, t, w = qa.shape
    blk = pl.BlockSpec((1, t, w), lambda bi: (bi, 0, 0))
    out = jax.ShapeDtypeStruct((b, t, w), BF16)
    return pl.pallas_call(
        _ctx_attn_kernel,
        grid=(b,),
        in_specs=[pl.BlockSpec(memory_space=pltpu.SMEM)] + [blk] * 9,
        out_specs=[blk] * 3,
        out_shape=[out] * 3,
        compiler_params=_cparams(("arbitrary",)),
        name="ctx_attn",
    )(sink.reshape(1, N_HEADS).astype(F32), qa, ka, va, qb, kb, vb, qd, kd, vd)


def _global_kernel(q_ref, k_ref, vt_ref, kc_ref, vtc_ref, o_ref, *, tq, kt):
    kv = pl.program_id(1)
    n_blk = q_ref.shape[2] // tq
    sel = (lax.broadcasted_iota(jnp.int32, (2 * HEAD_DIM, tq), 0) // HEAD_DIM) == kv

    def block_rhs(blk):
        qt = q_ref[0, :, blk * tq:(blk + 1) * tq]
        zero = jnp.zeros_like(qt)

        def padded(g):
            qg = qt[g * HEAD_DIM:(g + 1) * HEAD_DIM, :]
            return jnp.where(sel, jnp.concatenate([qg, qg], axis=0), zero)

        return jnp.concatenate([padded(0), padded(1)], axis=1)

    rhs = [block_rhs(blk) for blk in range(n_blk)]
    n_lat = k_ref.shape[1] // kt
    n_tiles = n_lat + kc_ref.shape[1] // kt
    ones = jnp.ones((ONES_ROWS, kt), BF16)

    def k_tile(i):
        return k_ref[0, i * kt:(i + 1) * kt, :] if i < n_lat else kc_ref[0, (i - n_lat) * kt:(i - n_lat + 1) * kt, :]

    def vt_tile(i):
        vt = (vt_ref[0, :, i * kt:(i + 1) * kt] if i < n_lat
              else vtc_ref[0, :, (i - n_lat) * kt:(i - n_lat + 1) * kt])
        return jnp.concatenate([vt, ones], axis=0)

    scores, tile_max, run_max, probs = {}, {}, {}, {}
    acc = [None] * n_blk
    n_items = n_blk * n_tiles
    for i in range(n_items + 2):
        if i < n_items:
            blk, c = divmod(i, n_tiles)
            scores[i] = _dot(k_tile(c), rhs[blk])
            tile_max[i] = jnp.max(scores[i], axis=0, keepdims=True)
        j = i - 1
        if 0 <= j < n_items:
            first = j % n_tiles == 0
            run_max[j] = tile_max[j] if first else jnp.maximum(run_max[j - 1], tile_max[j])
            probs[j] = jnp.exp2(scores.pop(j) - run_max[j]).astype(BF16)
        j = i - 2
        if 0 <= j < n_items:
            blk, c = divmod(j, n_tiles)
            pv = _dot(vt_tile(c), probs.pop(j))
            acc[blk] = pv if c == 0 else acc[blk] * jnp.exp2(run_max[j - 1] - run_max[j]) + pv
            if c == n_tiles - 1:
                o = acc[blk][0:HEAD_DIM] * (1.0 / acc[blk][HEAD_DIM:HEAD_DIM + 1])
                ot = jnp.concatenate([o[:, :tq], o[:, tq:]], axis=0)
                o_ref[0, blk * tq:(blk + 1) * tq, :] = ot.T.astype(BF16)


def _global_attn(qt, k, vt, kc, vtc, *, tq, kt, n_blk):
    b, w, s = qt.shape
    t = kc.shape[1]
    rows = n_blk * tq
    assert s % rows == 0 and s % kt == 0 and t % kt == 0
    return pl.pallas_call(
        functools.partial(_global_kernel, tq=tq, kt=kt),
        grid=(b, N_KV, s // rows),
        in_specs=[pl.BlockSpec((1, 2 * HEAD_DIM, rows), lambda bi, kv, i: (bi, kv, i)),
                  pl.BlockSpec((1, s, KV_W), lambda bi, kv, i: (bi, 0, 0)),
                  pl.BlockSpec((1, HEAD_DIM, s), lambda bi, kv, i: (bi, kv, 0)),
                  pl.BlockSpec((1, t, KV_W), lambda bi, kv, i: (bi, 0, 0)),
                  pl.BlockSpec((1, HEAD_DIM, t), lambda bi, kv, i: (bi, kv, 0))],
        out_specs=pl.BlockSpec((1, rows, 2 * HEAD_DIM), lambda bi, kv, i: (bi, i, kv)),
        out_shape=jax.ShapeDtypeStruct((b, s, w), BF16),
        compiler_params=_cparams(("parallel", "arbitrary", "arbitrary")),
        name="global_attn",
    )(qt, k, vt, kc, vtc)


def _hgrn_constants():
    n = HGRN_L
    tm = np.zeros((2, 2 * n, n), np.float32)
    mk = np.zeros((2, HGRN_LEVELS + 1, n, n), np.float32)
    for d in range(2):
        pos = np.arange(n) if d == 0 else n - 1 - np.arange(n)
        p, q = pos[:, None], pos[None, :]
        for li in range(HGRN_LEVELS):
            h = n >> (li + 1)
            same_blk = (p // (2 * h)) == (q // (2 * h))
            up_p, up_q = (p % (2 * h)) >= h, (q % (2 * h)) >= h
            mk[d, li] = same_blk & up_p & ~up_q
            if h == 2:
                same_half = same_blk & (up_p == up_q)
                tm[d, 0:n] = np.where(up_p, same_half & (q <= p), same_half & (q > p))
        mk[d, HGRN_LEVELS] = np.eye(n)
        tm[d, n:2 * n] = q <= p
    mk = np.tile(mk, (1, 1, N_HEADS, 1))
    return jnp.asarray(tm, BF16), jnp.asarray(mk, F32)


def _hgrn_seq_kernel(qf_ref, vf_ref, zf_ref, qb_ref, vb_ref, zb_ref, lb_ref, t_ref, mk_ref, bd_ref, *rest,
                     g_chunks, has_init, emit_state):
    rest = list(rest)
    init_ref = rest.pop(0) if has_init else None
    of_ref, ob_ref = rest.pop(0), rest.pop(0)
    so_ref = rest.pop(0) if emit_state else None
    st_ref = rest.pop(0)
    step = pl.program_id(1)
    n = HGRN_L

    @pl.when(step == 0)
    def _():
        st_ref[...] = init_ref[0] if has_init else jnp.zeros_like(st_ref)

    ins = ((qf_ref, vf_ref, zf_ref), (qb_ref, vb_ref, zb_ref))
    outs = (of_ref, ob_ref)
    items = [(d, (g if d == 0 else g_chunks - 1 - g) * n) for g in range(g_chunks) for d in (0, 1)]

    row = lax.broadcasted_iota(jnp.int32, (n, GROUP_W), 0)
    work = {}

    def stage_gates(i):
        d, r0 = items[i]
        q = ins[d][0][0, r0:r0 + n, :]
        v = ins[d][1][0, r0:r0 + n, :]
        z = ins[d][2][0, r0:r0 + n, :]
        lb = lb_ref[d]
        f = lb + (1.0 - lb) * jax.nn.sigmoid(z)
        g = jnp.log(f)
        g_hi = g.astype(BF16)
        g_lo = (g - g_hi.astype(F32)).astype(BF16)
        e = _dot(t_ref[d], g_hi) + _dot(t_ref[d], g_lo)
        work[i] = dict(qq=_silu(q), kk=1.0 - f, f=f, v=v, e2=e[0:n], b=e[n:2 * n],
                       a=jnp.zeros((N_HEADS * n, n), F32))

    def level_scale(w, d, h):
        if h == 2:
            return jnp.exp(w["e2"])
        b = w["b"]
        blocks = []
        for j in range(n // (2 * h)):
            m = 2 * h * j + (h - 1 if d == 0 else h)
            blocks.append(jnp.broadcast_to(b[m:m + 1, :], (2 * h, GROUP_W)))
        bm = blocks[0] if len(blocks) == 1 else jnp.concatenate(blocks, axis=0)
        later = ((row // h) % 2) == (1 if d == 0 else 0)
        return jnp.exp(jnp.where(later, b - bm, bm - b))

    def stage_level(i, li):
        d, w = items[i][0], work[i]
        h = n >> (li + 1)
        if h >= 2:
            sc = level_scale(w, d, h)
            qs, ks = w["qq"] * sc, w["kk"] * sc
        elif h == 1:
            qs, ks = w["qq"] * w["f"], w["kk"]
        else:
            qs, ks = w["qq"], w["kk"]
        w["a"] = w["a"] + _dot_nt(_stack_heads(qs.astype(BF16)), ks.astype(BF16)) * mk_ref[d, li]

    st = [st_ref[0], st_ref[1]]

    def stage_out(i):
        (d, r0), w = items[i], work.pop(i)
        b_incl = w["b"]
        b_tot = b_incl[n - 1:n] if d == 0 else b_incl[0:1]
        vb = w["v"].astype(BF16)
        q_hat = (w["qq"] * jnp.exp(b_incl)).astype(BF16)
        k_hat = (w["kk"] * jnp.exp(b_tot - b_incl)).astype(BF16)
        o = _unstack_heads(_dot(w["a"].astype(BF16), vb), n)
        upd = _dot(w["v"].T.astype(BF16), k_hat) * bd_ref[...]
        outs[d][0, r0:r0 + n, :] = o + _dot_nt(q_hat, st[d].astype(BF16))
        st[d] = st[d] * jnp.exp(b_tot) + upd

    wave = min(HGRN_ITEMS_PER_WAVE, len(items))
    n_waves = len(items) // wave
    for w in range(n_waves + 2):
        if w < n_waves:
            for i in range(w * wave, (w + 1) * wave):
                stage_gates(i)
        if 0 <= w - 1 < n_waves:
            for li in range(HGRN_LEVELS + 1):
                for i in range((w - 1) * wave, w * wave):
                    stage_level(i, li)
        if 0 <= w - 2 < n_waves:
            for i in range((w - 2) * wave, (w - 1) * wave):
                stage_out(i)
    st_ref[0] = st[0]
    st_ref[1] = st[1]
    if emit_state:
        so_ref[0, 0] = st[0]
        so_ref[0, 1] = st[1]


def _hgrn_seq(c_raw, lb, consts, init_state, *, g_chunks, emit_state):
    tmat, masks, bd = consts
    b, s, _ = c_raw.shape
    rows = g_chunks * HGRN_L
    assert s % rows == 0
    n_steps = s // rows
    has_init = init_state is not None
    fwd = lambda col: pl.BlockSpec((1, rows, GROUP_W), lambda bi, st: (bi, st, col))
    bwd = lambda col: pl.BlockSpec((1, rows, GROUP_W), lambda bi, st: (bi, n_steps - 1 - st, col))
    state_spec = pl.BlockSpec((1, 2, GROUP_W, GROUP_W), lambda bi, st: (bi, 0, 0, 0))
    in_specs = [fwd(0), fwd(1), fwd(2), bwd(0), bwd(1), bwd(3),
                _const_spec((2, 1, GROUP_W)), _const_spec(tmat.shape), _const_spec(masks.shape),
                _const_spec(bd.shape)]
    args = [c_raw] * 6 + [lb.reshape(2, 1, GROUP_W), tmat, masks, bd]
    if has_init:
        in_specs.append(state_spec)
        args.append(init_state)
    out_specs = [fwd(0), bwd(0)]
    out_shape = [jax.ShapeDtypeStruct((b, s, GROUP_W), F32)] * 2
    if emit_state:
        out_specs.append(state_spec)
        out_shape.append(jax.ShapeDtypeStruct((b, 2, GROUP_W, GROUP_W), F32))
    return pl.pallas_call(
        functools.partial(_hgrn_seq_kernel, g_chunks=g_chunks, has_init=has_init, emit_state=emit_state),
        grid=(b, n_steps),
        in_specs=in_specs,
        out_specs=out_specs,
        out_shape=out_shape,
        scratch_shapes=[pltpu.VMEM((2, GROUP_W, GROUP_W), F32)],
        compiler_params=_cparams(("parallel", "arbitrary")),
        name="hgrn_ctx" if emit_state else "hgrn",
    )(*args)


def _outproj_kernel(x_ref, oa_ref, ob_ref, of_ref, obw_ref, gate_ref, od_ref, w_ref, onorm_ref, bd_ref,
                    ga_ref, lng_ref, lnb_ref, o_ref, *, alpha):
    tm = x_ref.shape[1]
    sub = min(tm, OUTPROJ_SUB_ROWS)
    acc = {}
    for i in range(tm // sub + 1):
        if i < tm // sub:
            r = slice(i * sub, (i + 1) * sub)
            oc = of_ref[0, r, :] + obw_ref[0, r, :]
            y = oc * lax.rsqrt(_head_mean_sq(oc, bd_ref[...]) + EPS) * onorm_ref[...]
            y = y * _silu(gate_ref[0, r, :])
            mix = jnp.concatenate([oa_ref[0, r, :], ob_ref[0, r, :], y.astype(BF16), od_ref[0, r, :]], axis=1)
            acc[i] = _dot(mix, w_ref[0])
        if i >= 1:
            r = slice((i - 1) * sub, i * sub)
            z = alpha * x_ref[0, r, :] + ga_ref[0] * acc.pop(i - 1)
            o_ref[0, r, :] = _layer_norm(z, lng_ref[...], lnb_ref[...])


def _outproj(x, oa, ob, oc_f, oc_b, c_raw, od, w_out, layer, onorm, bd256, gate_a, ln_g, ln_b, *, alpha, tm):
    b, s, d = x.shape
    assert s % tm == 0
    grp = pl.BlockSpec((1, tm, GROUP_W), lambda bi, i: (bi, i, 0))
    vec = pl.BlockSpec((1, 1, d), lambda bi, i: (bi, 0, 0))
    return pl.pallas_call(
        functools.partial(_outproj_kernel, alpha=alpha),
        grid=(b, s // tm),
        in_specs=[pl.BlockSpec((1, tm, d), lambda bi, i: (bi, i, 0)), grp, grp, grp, grp,
                  pl.BlockSpec((1, tm, GROUP_W), lambda bi, i: (bi, i, 4)),
                  grp, _layer_spec(w_out.shape, layer), _const_spec(onorm.shape), _const_spec(bd256.shape),
                  vec, _const_spec(ln_g.shape), _const_spec(ln_b.shape)],
        out_specs=pl.BlockSpec((1, tm, d), lambda bi, i: (bi, i, 0)),
        out_shape=jax.ShapeDtypeStruct((b, s, d), F32),
        compiler_params=_cparams(("parallel", "arbitrary")),
        name="outproj",
    )(x, oa, ob, oc_f, oc_b, c_raw, od, w_out, onorm, bd256, gate_a, ln_g, ln_b)


def _ffn_kernel(h_ref, hp_ref, hn_ref, sc_ref, sh_ref, gt_ref, wup_ref, cw_ref, cb_ref, wdn_ref,
                lng_ref, lnb_ref, o_ref, g_scr, *, alpha, d_ff, cwid, n_tiles):
    i = pl.program_id(1)
    tm = h_ref.shape[1]
    sc = 1.0 + sc_ref[0]
    sh = sh_ref[0]
    h = h_ref[0]
    halo = jnp.concatenate([hp_ref[0], hn_ref[0]], axis=0)
    lhs = jnp.concatenate([(h * sc + sh).astype(BF16), (halo * sc + sh).astype(BF16)], axis=0)
    has_prev = (i > 0).astype(F32)
    has_next = (i < n_tiles - 1).astype(F32)
    row = lax.broadcasted_iota(jnp.int32, (tm, cwid), 0)
    first, last = row == 0, row == tm - 1

    def up_cols(c0):
        return _dot(lhs, wup_ref[0, :, c0:c0 + cwid])

    def conv_cols(u, c0):
        um = u[:tm]
        u_prev = u[tm + 7:tm + 8] * has_prev
        u_next = u[tm + 8:tm + 9] * has_next
        dn = jnp.where(first, u_prev, pltpu.roll(um, 1, 0))
        up = jnp.where(last, u_next, pltpu.roll(um, tm - 1, 0))
        w = cw_ref[:, c0:c0 + cwid]
        return dn * w[0:1] + um * w[1:2] + up * w[2:3] + cb_ref[:, c0:c0 + cwid]

    for j in range(d_ff // cwid):
        a = conv_cols(up_cols(j * cwid), j * cwid)
        v = conv_cols(up_cols(d_ff + j * cwid), d_ff + j * cwid)
        g_scr[:, j * cwid:(j + 1) * cwid] = (_silu(a) * v).astype(BF16)
    z = alpha * h + gt_ref[0] * _dot(g_scr[...], wdn_ref[0])
    o_ref[0] = _layer_norm(z, lng_ref[...], lnb_ref[...])


def _ffn(h, scale, shift, gate, w_up, conv_w, conv_b, w_down, layer, ln_g, ln_b, *, alpha, tm, cwid):
    b, s, d = h.shape
    d_ff = w_down.shape[1]
    assert s % tm == 0 and tm % 16 == 0 and d_ff % cwid == 0
    n_tiles = s // tm
    r8 = tm // 8
    vec = pl.BlockSpec((1, 1, d), lambda bi, i: (bi, 0, 0))
    return pl.pallas_call(
        functools.partial(_ffn_kernel, alpha=alpha, d_ff=d_ff, cwid=cwid, n_tiles=n_tiles),
        grid=(b, n_tiles),
        in_specs=[pl.BlockSpec((1, tm, d), lambda bi, i: (bi, i, 0)),
                  pl.BlockSpec((1, 8, d), lambda bi, i: (bi, jnp.maximum(i * r8 - 1, 0), 0)),
                  pl.BlockSpec((1, 8, d), lambda bi, i: (bi, jnp.minimum((i + 1) * r8, s // 8 - 1), 0)),
                  vec, vec, vec,
                  _layer_spec(w_up.shape, layer), _const_spec(conv_w.shape), _const_spec(conv_b.shape),
                  _layer_spec(w_down.shape, layer), _const_spec(ln_g.shape), _const_spec(ln_b.shape)],
        out_specs=pl.BlockSpec((1, tm, d), lambda bi, i: (bi, i, 0)),
        out_shape=jax.ShapeDtypeStruct((b, s, d), F32),
        scratch_shapes=[pltpu.VMEM((tm, d_ff), BF16)],
        compiler_params=_cparams(("parallel", "arbitrary")),
        name="conv_ffn",
    )(h, h, h, scale, shift, gate, w_up, conv_w, conv_b, w_down, ln_g, ln_b)


def _rope_tables(n, with_rotation):
    lane = np.arange(V7X_LANES)
    if not with_rotation:
        one = jnp.ones((n, V7X_LANES), F32)
        zero = jnp.zeros((n, V7X_LANES), F32)
        return one, zero, zero
    t = np.arange(n)
    pos = np.stack([t // GRID_W, t % GRID_W], -1).astype(np.float64)
    n_freq = HEAD_DIM // 4
    inv = ROPE_BASE ** (-np.arange(n_freq, dtype=np.float64) / n_freq)
    ang = pos[:, :, None] * inv
    axis = (lane % HEAD_DIM) // (HEAD_DIM // 2)
    freq = lane % n_freq
    ang_l = ang[:, axis, freq]
    first = (lane % (2 * n_freq)) < n_freq
    cos, sin = np.cos(ang_l), np.sin(ang_l)
    tabs = (cos, np.where(first, -sin, 0.0), np.where(first, 0.0, sin))
    return tuple(jnp.asarray(a, F32) for a in tabs)


def _block_diag_mean(width):
    lane = np.arange(width)
    same = (lane[:, None] // HEAD_DIM) == (lane[None, :] // HEAD_DIM)
    return jnp.asarray(same / HEAD_DIM, BF16)


def kernel(x, c, ctx, c_ctx, w_mod, b_mod, w_in, w_out, ln_g, ln_b, attn_sink, qk_norm, hgrn_lb_logits,
           hgrn_onorm, na_rpb, ffn_w_up, ffn_conv_w, ffn_conv_b, ffn_w_down):
    depth = w_mod.shape[0]
    b, s, d = x.shape
    t = ctx.shape[1]
    alpha = (2 * depth) ** 0.25

    mods = _modulation(c, c_ctx, w_mod, b_mod)
    rope_lat = _rope_tables(s, True)
    rope_ctx = _rope_tables(t, False)
    bd128, bd256 = _block_diag_mean(KV_W), _block_diag_mean(GROUP_W)
    lane = np.arange(GROUP_W)
    head_bd = jnp.asarray((lane[:, None] // HEAD_DIM) == (lane[None, :] // HEAD_DIM), F32)
    hgrn_consts = _hgrn_constants() + (head_bd,)
    p_lb = jax.nn.softmax(hgrn_lb_logits.astype(F32), axis=1)
    lb_all = jnp.cumsum(p_lb, axis=1) - p_lb[:, :1]
    w_in_b, w_out_b = w_in.astype(BF16), w_out.astype(BF16)
    w_up_b, w_dn_b = ffn_w_up.astype(BF16), ffn_w_down.astype(BF16)

    h_x, h_c = x, ctx
    for l in range(depth):
        need_ctx = l < depth - 1
        mod = mods[l]
        sa, ca, ga, sf, cf, gf = (mod[:, i * d:(i + 1) * d] for i in range(6))
        lat = lambda a: a[:b, None, :]
        con = lambda a: jnp.broadcast_to(a[b][None, None, :], (b, 1, d))
        gq = jnp.tile(qk_norm[l, 0], 2)[None, :]
        gk = jnp.tile(qk_norm[l, 1], 2)[None, :]

        px = _inproj(h_x, lat(sa), lat(ca), w_in_b, l, rope_lat, gq, gk, bd128, is_ctx=False, tm=1024)
        pc = _inproj(h_c, con(sa), con(ca), w_in_b, l, rope_ctx, gq, gk, bd128, is_ctx=True, tm=t)
        c_raw, c_raw_c = px["c"], pc["c"]

        o_a = _window_attn(px["qa"], px["ka"], px["vat"], pc["ka"], pc["vat"], attn_sink[l],
                           n_pairs=WINDOW_PAIRS_PER_STEP)

        o_b = _global_attn(px["qb"], px["kb"], px["vbt"], pc["kb"], pc["vbt"], tq=256, kt=256,
                           n_blk=GLOBAL_BLOCKS_PER_STEP)

        ocf_c, ocb_c, ctx_state = _hgrn_seq(c_raw_c, lb_all[:, l], hgrn_consts, None,
                                            g_chunks=t // HGRN_L, emit_state=True)
        oc_f, oc_b = _hgrn_seq(c_raw, lb_all[:, l], hgrn_consts, ctx_state,
                               g_chunks=HGRN_CHUNKS_PER_STEP, emit_state=False)

        o_d = _na_attn(px["qd"], px["kd"], px["vd"], pc["kd"], pc["vd"], _na_bias_table(na_rpb[l]),
                       n_rows=NA_ROWS_PER_STEP)

        onorm = hgrn_onorm[l][None, :]
        lng0, lnb0 = ln_g[l, 0][None, :], ln_b[l, 0][None, :]
        lng1, lnb1 = ln_g[l, 1][None, :], ln_b[l, 1][None, :]
        cw, cb = ffn_conv_w[l], ffn_conv_b[l][None, :]
        h_x = _outproj(h_x, o_a, o_b, oc_f, oc_b, c_raw, o_d, w_out_b, l, onorm, bd256, lat(ga), lng0, lnb0,
                       alpha=alpha, tm=1024)
        h_x = _ffn(h_x, lat(cf), lat(sf), lat(gf), w_up_b, cw, cb, w_dn_b, l, lng1, lnb1,
                   alpha=alpha, tm=512, cwid=256)
        if need_ctx:
            oa_c, ob_c, od_c = _ctx_attn(attn_sink[l], pc["qa"], pc["kax"], pc["vax"], pc["qb"], pc["kbx"],
                                         pc["vbx"], pc["qd"], pc["kd"], pc["vd"])
            h_c = _outproj(h_c, oa_c, ob_c, ocf_c, ocb_c, c_raw_c, od_c, w_out_b, l, onorm, bd256, con(ga),
                           lng0, lnb0, alpha=alpha, tm=t)
            h_c = _ffn(h_c, con(cf), con(sf), con(gf), w_up_b, cw, cb, w_dn_b, l, lng1, lnb1,
                       alpha=alpha, tm=t, cwid=256)
    return h_x
```

```python
import functools

import numpy as np
import jax
import jax.numpy as jnp
from jax import lax
from jax.experimental import pallas as pl
from jax.experimental.pallas import tpu as pltpu

F32 = jnp.float32
BF16 = jnp.bfloat16

HEAD_DIM = 64
N_HEADS = 4
N_KV = 2
GROUP_W = N_HEADS * HEAD_DIM
KV_W = N_KV * HEAD_DIM
GRID_W = 64
WIN = 128
BLOCK = 128
NA_ROWS = 8
NA_COLS = 16
ROPE_BASE = 10000.0
EPS = 1e-6
HGRN_L = 64
HGRN_LEVELS = 6
HGRN_CHUNKS_PER_STEP = 8
HGRN_ITEMS_PER_WAVE = 2
WINDOW_PAIRS_PER_STEP = 4
WINDOW_ITEMS_PER_WAVE = 4
NA_ROWS_PER_STEP = 8
NA_ROWS_PER_WAVE = 2
GLOBAL_BLOCKS_PER_STEP = 4
OUTPROJ_SUB_ROWS = 256
ONES_ROWS = 16

V7X_LANES = 128
V7X_VMEM_LIMIT_BYTES = 56 * 1024 * 1024

COL_QA, COL_KA, COL_VA = 0, 256, 384
COL_QB, COL_KB, COL_VB = 512, 768, 896
COL_C = 1024
COL_QD, COL_KD, COL_VD = 2304, 2560, 2816
IN_WIDTH = 3072
C_WIDTH = 5 * GROUP_W

NEG_INF = float("-inf")
LOG2_E = 1.4426950408889634


def _cparams(sem):
    return pltpu.CompilerParams(dimension_semantics=sem, vmem_limit_bytes=V7X_VMEM_LIMIT_BYTES)


def _const_spec(shape):
    nd = len(shape)
    return pl.BlockSpec(shape, lambda *_: (0,) * nd, pipeline_mode=pl.Buffered(1))


def _layer_spec(shape, layer):
    nd = len(shape)
    return pl.BlockSpec((1,) + tuple(shape[1:]), lambda *_: (layer,) + (0,) * (nd - 1),
                        pipeline_mode=pl.Buffered(1))


def _silu(x):
    return x * jax.nn.sigmoid(x)


def _dot(a, b):
    return jnp.dot(a, b, preferred_element_type=F32)


def _dot_nt(a, b):
    return lax.dot_general(a, b, (((1,), (1,)), ((), ())), preferred_element_type=F32)


def _head_mask(shape, h):
    lane = lax.broadcasted_iota(jnp.int32, shape, len(shape) - 1)
    return (lane // HEAD_DIM) == h


def _stack_heads(q):
    zero = jnp.zeros_like(q)
    return jnp.concatenate([jnp.where(_head_mask(q.shape, h), q, zero) for h in range(N_HEADS)], axis=0)


def _unstack_heads(o, r):
    out = jnp.zeros((r, o.shape[1]), o.dtype)
    for h in range(N_HEADS):
        blk = o[h * r:(h + 1) * r]
        out = jnp.where(_head_mask(blk.shape, h), blk, out)
    return out


def _head_mean_sq(x, bd):
    sq = x * x
    hi = sq.astype(BF16)
    lo = (sq - hi.astype(F32)).astype(BF16)
    return _dot(hi, bd) + _dot(lo, bd)


def _layer_norm(z, g, b):
    mu = jnp.mean(z, axis=-1, keepdims=True)
    zc = z - mu
    var = jnp.mean(zc * zc, axis=-1, keepdims=True)
    return zc * lax.rsqrt(var + EPS) * g + b


def _mod_kernel(s_ref, w_ref, b_ref, o_ref):
    s = _silu(s_ref[...])
    o_ref[0] = jnp.dot(s, w_ref[0], preferred_element_type=F32,
                       precision=lax.Precision.HIGHEST) + b_ref[0]


def _modulation(c, c_ctx, w_mod, b_mod):
    depth, d, d6 = w_mod.shape
    b = c.shape[0]
    rows = 8
    assert b < rows
    s_in = jnp.zeros((rows, d), F32).at[:b].set(c).at[b].set(c_ctx)
    tn = d6 // 4
    return pl.pallas_call(
        _mod_kernel,
        grid=(depth, d6 // tn),
        in_specs=[pl.BlockSpec((rows, d), lambda l, j: (0, 0)),
                  pl.BlockSpec((1, d, tn), lambda l, j: (l, 0, j)),
                  pl.BlockSpec((1, 1, tn), lambda l, j: (l, 0, j))],
        out_specs=pl.BlockSpec((1, rows, tn), lambda l, j: (l, 0, j)),
        out_shape=jax.ShapeDtypeStruct((depth, rows, d6), F32),
        compiler_params=_cparams(("arbitrary", "arbitrary")),
        name="modulation",
    )(s_in, w_mod, b_mod.reshape(depth, 1, d6))


def _dup_heads(x):
    r = pltpu.roll(x, HEAD_DIM, 1)
    lo = lax.broadcasted_iota(jnp.int32, x.shape, 1) < HEAD_DIM
    return jnp.concatenate([jnp.where(lo, x, r), jnp.where(lo, r, x)], axis=1)


def _inproj_outputs(is_ctx):
    gqa = lambda g: [("q" + g, "row" if is_ctx else "col", GROUP_W, BF16), ("k" + g, "row", KV_W, BF16),
                     ("v" + g + "t", "col", KV_W, BF16)]
    dup = lambda g: [("k" + g + "x", "row", GROUP_W, BF16), ("v" + g + "x", "row", GROUP_W, BF16)]
    outs = gqa("a") + gqa("b")
    if is_ctx:
        outs += dup("a") + dup("b")
    return outs + [("c", "row", C_WIDTH, F32), ("qd", "row", GROUP_W, BF16), ("kd", "row", GROUP_W, BF16),
                   ("vd", "row", GROUP_W, BF16)]


def _inproj_kernel(h_ref, sc_ref, sh_ref, w_ref, rc_ref, ra_ref, rb_ref, gq_ref, gk_ref, bd_ref,
                   *outs, is_ctx):
    o = {name: ref for (name, _, _, _), ref in zip(_inproj_outputs(is_ctx), outs)}
    hm = (h_ref[0] * (1.0 + sc_ref[0]) + sh_ref[0]).astype(BF16)
    rc, ra, rb = rc_ref[...], ra_ref[...], rb_ref[...]
    bd = bd_ref[...]
    q_scale = HEAD_DIM ** -0.5 * (1.0 if is_ctx else LOG2_E)

    def proj(c0, width):
        return _dot(hm, w_ref[0, :, c0:c0 + width])

    def rope(x):
        return x * rc + pltpu.roll(x, V7X_LANES - 16, 1) * ra + pltpu.roll(x, 16, 1) * rb

    def rms(x, gain):
        return x * lax.rsqrt(_head_mean_sq(x, bd) + EPS) * gain

    def gqa_group(g, c_q, c_k, c_v, q_fn, k_fn):
        xq = proj(c_q, GROUP_W)
        for j in range(2):
            sl = slice(j * V7X_LANES, (j + 1) * V7X_LANES)
            y = rope(q_fn(xq[:, sl])) * q_scale
            if is_ctx:
                o["q" + g][0, :, sl] = y.astype(BF16)
            else:
                o["q" + g][0, sl, :] = y.T.astype(BF16)
        k = rope(k_fn(proj(c_k, KV_W)))
        v = proj(c_v, KV_W)
        o["k" + g][0] = k.astype(BF16)
        o["v" + g + "t"][0] = v.T.astype(BF16)
        if is_ctx:
            o["k" + g + "x"][0] = _dup_heads(k).astype(BF16)
            o["v" + g + "x"][0] = _dup_heads(v).astype(BF16)

    gq, gk = gq_ref[...], gk_ref[...]
    gqa_group("a", COL_QA, COL_KA, COL_VA, lambda x: x, lambda x: x)
    gqa_group("b", COL_QB, COL_KB, COL_VB, lambda x: rms(x, gq), lambda x: rms(x, gk))

    for j in range(C_WIDTH // GROUP_W):
        o["c"][0, :, j * GROUP_W:(j + 1) * GROUP_W] = proj(COL_C + j * GROUP_W, GROUP_W)

    o["qd"][0] = (proj(COL_QD, GROUP_W) * HEAD_DIM ** -0.5).astype(BF16)
    o["kd"][0] = proj(COL_KD, GROUP_W).astype(BF16)
    o["vd"][0] = proj(COL_VD, GROUP_W).astype(BF16)


def _inproj(h, shift, scale, w_in, layer, rope_tabs, gq, gk, bd128, *, is_ctx, tm):
    b, s, d = h.shape
    assert s % tm == 0
    rc, ra, rb = rope_tabs
    outs = _inproj_outputs(is_ctx)
    out_specs = [pl.BlockSpec((1, tm, w), lambda bi, i: (bi, i, 0)) if lay == "row"
                 else pl.BlockSpec((1, w, tm), lambda bi, i: (bi, 0, i)) for _, lay, w, _ in outs]
    out_shape = [jax.ShapeDtypeStruct((b, s, w) if lay == "row" else (b, w, s), dt) for _, lay, w, dt in outs]
    tab = pl.BlockSpec((tm, V7X_LANES), lambda bi, i: (i, 0))
    vec = pl.BlockSpec((1, 1, d), lambda bi, i: (bi, 0, 0))
    res = pl.pallas_call(
        functools.partial(_inproj_kernel, is_ctx=is_ctx),
        grid=(b, s // tm),
        in_specs=[pl.BlockSpec((1, tm, d), lambda bi, i: (bi, i, 0)), vec, vec,
                  _layer_spec(w_in.shape, layer), tab, tab, tab,
                  _const_spec(gq.shape), _const_spec(gk.shape), _const_spec(bd128.shape)],
        out_specs=out_specs,
        out_shape=out_shape,
        compiler_params=_cparams(("parallel", "arbitrary")),
        name="inproj_ctx" if is_ctx else "inproj",
    )(h, scale, shift, w_in, rc, ra, rb, gq, gk, bd128)
    return {name: r for (name, _, _, _), r in zip(outs, res)}


def _attend_many(items, wave=None):
    n = len(items)
    wave = wave or n
    scores, maxes, probs, dens, outs = {}, {}, {}, {}, {}

    def stage_scores(i):
        qs, parts, sink_col = items[i]
        ss, m = [], sink_col
        for k, _, bias in parts:
            s = _dot_nt(qs, k)
            if bias is not None:
                s = s + bias
            ss.append(s)
            smax = jnp.max(s, axis=1, keepdims=True)
            m = smax if m is None else jnp.maximum(m, smax)
        scores[i], maxes[i] = ss, m

    def stage_probs(i):
        sink_col, m = items[i][2], maxes[i]
        den = None if sink_col is None else jnp.exp(sink_col - m)
        ps = []
        for s in scores.pop(i):
            p = jnp.exp(s - m)
            psum = jnp.sum(p, axis=1, keepdims=True)
            den = psum if den is None else den + psum
            ps.append(p.astype(BF16))
        probs[i], dens[i] = ps, den

    def stage_pv(i):
        out = None
        for p, (_, v, _) in zip(probs.pop(i), items[i][1]):
            pv = _dot(p, v)
            out = pv if out is None else out + pv
        outs[i] = out * (1.0 / dens[i])

    n_waves = -(-n // wave)
    for w in range(n_waves + 2):
        for stage, lag in ((stage_scores, 0), (stage_probs, 1), (stage_pv, 2)):
            ww = w - lag
            if 0 <= ww < n_waves:
                for i in range(ww * wave, min((ww + 1) * wave, n)):
                    stage(i)
    return [outs[i] for i in range(n)]


def _window_kernel(sink_ref, q_ref, *refs, n_pairs, n_blocks):
    n_kblk = 2 * n_pairs + 2
    k_refs, vt_refs = refs[:n_kblk], refs[n_kblk:2 * n_kblk]
    kc_ref, vtc_ref, o_ref = refs[2 * n_kblk:]
    step = pl.program_id(1)
    tq = 2 * BLOCK
    first_blk = step * 2 * n_pairs - 1
    ones_blk = jnp.ones((ONES_ROWS, BLOCK), BF16)
    ones_ctx = jnp.ones((ONES_ROWS, kc_ref.shape[1]), BF16)
    ki = lax.broadcasted_iota(jnp.int32, (BLOCK, tq), 0)
    qi = lax.broadcasted_iota(jnp.int32, (BLOCK, tq), 1)
    lane = lax.broadcasted_iota(jnp.int32, (1, 2 * tq), 1)
    zeros_q = jnp.zeros((HEAD_DIM, tq), BF16)

    def band_bias(jj, j):
        ok = jnp.abs((1 - jj) * BLOCK + qi - ki) <= WIN
        if j == 0:
            ok = ok & (first_blk >= 0)
        elif j == n_kblk - 1:
            ok = ok & (first_blk + j < n_blocks)
        bias = jnp.where(ok, 0.0, NEG_INF).astype(F32)
        return jnp.concatenate([bias, bias], axis=1)

    items = [(p, kv) for p in range(n_pairs) for kv in range(N_KV)]
    biases = {(p, jj): band_bias(jj, 2 * p + jj) for p in range(n_pairs) for jj in range(4)}

    scores, maxes, probs = {}, {}, {}

    def stage_scores(it):
        p, kv = it
        qt = q_ref[0, 2 * HEAD_DIM * kv:2 * HEAD_DIM * (kv + 1), p * tq:(p + 1) * tq]

        def padded(g):
            qg = qt[g * HEAD_DIM:(g + 1) * HEAD_DIM, :]
            return jnp.concatenate([qg, zeros_q] if kv == 0 else [zeros_q, qg], axis=0)

        rhs = jnp.concatenate([padded(0), padded(1)], axis=1)
        ss = [_dot(k_refs[2 * p + jj][0], rhs) + biases[(p, jj)] for jj in range(4)]
        ss.append(_dot(kc_ref[0], rhs))
        sink2 = jnp.where(lane < tq, sink_ref[0, 2 * kv], sink_ref[0, 2 * kv + 1]) * LOG2_E
        m = sink2
        for s in ss:
            m = jnp.maximum(m, jnp.max(s, axis=0, keepdims=True))
        scores[it], maxes[it] = ss, (m, sink2)

    def stage_probs(it):
        m = maxes[it][0]
        probs[it] = [jnp.exp2(s - m).astype(BF16) for s in scores.pop(it)]

    def stage_pv(it):
        p, kv = it
        ps = probs.pop(it)
        acc = None
        for jj in range(4):
            vt = vt_refs[2 * p + jj][0, kv * HEAD_DIM:(kv + 1) * HEAD_DIM, :]
            pv = _dot(jnp.concatenate([vt, ones_blk], axis=0), ps[jj])
            acc = pv if acc is None else acc + pv
        vtc = vtc_ref[0, kv * HEAD_DIM:(kv + 1) * HEAD_DIM, :]
        acc = acc + _dot(jnp.concatenate([vtc, ones_ctx], axis=0), ps[4])
        m, sink2 = maxes[(p, kv)]
        den = acc[HEAD_DIM:HEAD_DIM + 1] + jnp.exp2(sink2 - m)
        o = acc[0:HEAD_DIM] * (1.0 / den)
        ot = jnp.concatenate([o[:, :tq], o[:, tq:]], axis=0)
        o_ref[0, p * tq:(p + 1) * tq, 2 * HEAD_DIM * kv:2 * HEAD_DIM * (kv + 1)] = ot.T.astype(BF16)

    wave = min(WINDOW_ITEMS_PER_WAVE, len(items))
    n_waves = len(items) // wave
    for w in range(n_waves + 2):
        for stage, lag in ((stage_scores, 0), (stage_probs, 1), (stage_pv, 2)):
            if 0 <= w - lag < n_waves:
                for it in items[(w - lag) * wave:(w - lag + 1) * wave]:
                    stage(it)


def _window_attn(qt, k, vt, kc, vtc, sink, *, n_pairs):
    b, w, s = qt.shape
    t = kc.shape[1]
    rows = n_pairs * 2 * BLOCK
    n_blocks = s // BLOCK
    n_kblk = 2 * n_pairs + 2
    assert s % rows == 0

    def blk_idx(i, j):
        return jnp.clip(i * 2 * n_pairs - 1 + j, 0, n_blocks - 1)

    k_specs = [pl.BlockSpec((1, BLOCK, KV_W), functools.partial(lambda bi, i, j: (bi, blk_idx(i, j), 0), j=j))
               for j in range(n_kblk)]
    vt_specs = [pl.BlockSpec((1, KV_W, BLOCK), functools.partial(lambda bi, i, j: (bi, 0, blk_idx(i, j)), j=j))
                for j in range(n_kblk)]
    return pl.pallas_call(
        functools.partial(_window_kernel, n_pairs=n_pairs, n_blocks=n_blocks),
        grid=(b, s // rows),
        in_specs=[pl.BlockSpec(memory_space=pltpu.SMEM),
                  pl.BlockSpec((1, w, rows), lambda bi, i: (bi, 0, i))] + k_specs + vt_specs +
                 [pl.BlockSpec((1, t, KV_W), lambda bi, i: (bi, 0, 0)),
                  pl.BlockSpec((1, KV_W, t), lambda bi, i: (bi, 0, 0))],
        out_specs=pl.BlockSpec((1, rows, w), lambda bi, i: (bi, i, 0)),
        out_shape=jax.ShapeDtypeStruct((b, s, w), BF16),
        compiler_params=_cparams(("parallel", "arbitrary")),
        name="window_attn",
    )(sink.reshape(1, N_HEADS).astype(F32), qt, *([k] * n_kblk), *([vt] * n_kblk), kc, vtc)


def _na_kernel(q_ref, k_ref, v_ref, kc_ref, vc_ref, bias_ref, o_ref, *, rows, n_rows):
    step = pl.program_id(1)
    span = NA_ROWS * GRID_W
    ctx_part = (kc_ref[0], vc_ref[0], None)
    items = []
    for j in range(n_rows):
        r = step * n_rows + j
        r0 = jnp.clip(r - NA_ROWS // 2, 0, rows - NA_ROWS)
        start = pl.multiple_of(r0 * GRID_W, GRID_W)
        kw = k_ref[0, pl.ds(start, span), :]
        vw = v_ref[0, pl.ds(start, span), :]
        qs = _stack_heads(q_ref[0, j * GRID_W:(j + 1) * GRID_W, :])
        off0 = r0 - r + (NA_ROWS - 1)
        bias = jnp.concatenate([bias_ref[off0 + 2 * i] for i in range(NA_ROWS // 2)], axis=1)
        items.append((qs, [(kw, vw, bias), ctx_part], None))
    for j, o in enumerate(_attend_many(items, wave=NA_ROWS_PER_WAVE)):
        o_ref[0, j * GRID_W:(j + 1) * GRID_W, :] = _unstack_heads(o, GRID_W).astype(BF16)


def _na_bias_table(rpb):
    n_h, n_ro, n_co = rpb.shape
    q = np.arange(GRID_W)
    kc = np.arange(GRID_W)
    coff = np.clip(kc[None, :] - q[:, None], -(NA_COLS - 1), NA_COLS - 1) + (NA_COLS - 1)
    c0 = np.clip(q - NA_COLS // 2, 0, GRID_W - NA_COLS)
    ok = (kc[None, :] >= c0[:, None]) & (kc[None, :] < c0[:, None] + NA_COLS)
    onehot = np.zeros((n_co, GRID_W, GRID_W), np.float32)
    onehot[coff, q[:, None], kc[None, :]] = 1.0
    tab = jnp.einsum("hrc,cqk->hrqk", rpb.astype(F32), jnp.asarray(onehot), precision=lax.Precision.HIGHEST)
    tab = jnp.where(ok[None, None], tab, NEG_INF)
    pairs = jnp.concatenate([tab[:, :-1], tab[:, 1:]], axis=-1)
    return pairs.transpose(1, 0, 2, 3).reshape(n_ro - 1, n_h * GRID_W, 2 * GRID_W)


def _na_attn(q, k, v, kc, vc, bias_tab, *, n_rows):
    b, s, w = q.shape
    t = kc.shape[1]
    rows = s // GRID_W
    assert s % GRID_W == 0 and rows >= NA_ROWS and rows % n_rows == 0
    full = lambda n: pl.BlockSpec((1, n, w), lambda bi, i: (bi, 0, 0))
    blk = n_rows * GRID_W
    return pl.pallas_call(
        functools.partial(_na_kernel, rows=rows, n_rows=n_rows),
        grid=(b, rows // n_rows),
        in_specs=[pl.BlockSpec((1, blk, w), lambda bi, r: (bi, r, 0)),
                  full(s), full(s), full(t), full(t), _const_spec(bias_tab.shape)],
        out_specs=pl.BlockSpec((1, blk, w), lambda bi, r: (bi, r, 0)),
        out_shape=jax.ShapeDtypeStruct((b, s, w), BF16),
        compiler_params=_cparams(("parallel", "arbitrary")),
        name="na_attn",
    )(q, k, v, kc, vc, bias_tab)


def _ctx_attn_kernel(sink_ref, qa_ref, ka_ref, va_ref, qb_ref, kb_ref, vb_ref, qd_ref, kd_ref, vd_ref,
                     oa_ref, ob_ref, od_ref):
    t = qa_ref.shape[1]
    sink_col = jnp.concatenate([jnp.full((t, 1), sink_ref[0, h], F32) for h in range(N_HEADS)], axis=0)
    groups = ((qa_ref, ka_ref, va_ref, sink_col), (qb_ref, kb_ref, vb_ref, None), (qd_ref, kd_ref, vd_ref, None))
    outs = _attend_many([(_stack_heads(q_ref[0]), [(k_ref[0], v_ref[0], None)], sink)
                         for q_ref, k_ref, v_ref, sink in groups])
    for o_ref, o in zip((oa_ref, ob_ref, od_ref), outs):
        o_ref[0] = _unstack_heads(o, t).astype(BF16)


def _ctx_attn(sink, qa, ka, va, qb, kb, vb, qd, kd, vd):
    b, t, w = qa.shape
    blk = pl.BlockSpec((1, t, w), lambda bi: (bi, 0, 0))
    out = jax.ShapeDtypeStruct((b, t, w), BF16)
    return pl.pallas_call(
        _ctx_attn_kernel,
        grid=(b,),
        in_specs=[pl.BlockSpec(memory_space=pltpu.SMEM)] + [blk] * 9,
        out_specs=[blk] * 3,
        out_shape=[out] * 3,
        compiler_params=_cparams(("arbitrary",)),
        name="ctx_attn",
    )(sink.reshape(1, N_HEADS).astype(F32), qa, ka, va, qb, kb, vb, qd, kd, vd)


def _global_kernel(q_ref, k_ref, vt_ref, kc_ref, vtc_ref, o_ref, *, tq, kt):
    kv = pl.program_id(1)
    n_blk = q_ref.shape[2] // tq
    sel = (lax.broadcasted_iota(jnp.int32, (2 * HEAD_DIM, tq), 0) // HEAD_DIM) == kv

    def block_rhs(blk):
        qt = q_ref[0, :, blk * tq:(blk + 1) * tq]
        zero = jnp.zeros_like(qt)

        def padded(g):
            qg = qt[g * HEAD_DIM:(g + 1) * HEAD_DIM, :]
            return jnp.where(sel, jnp.concatenate([qg, qg], axis=0), zero)

        return jnp.concatenate([padded(0), padded(1)], axis=1)

    rhs = [block_rhs(blk) for blk in range(n_blk)]
    n_lat = k_ref.shape[1] // kt
    n_tiles = n_lat + kc_ref.shape[1] // kt
    ones = jnp.ones((ONES_ROWS, kt), BF16)

    def k_tile(i):
        return k_ref[0, i * kt:(i + 1) * kt, :] if i < n_lat else kc_ref[0, (i - n_lat) * kt:(i - n_lat + 1) * kt, :]

    def vt_tile(i):
        vt = (vt_ref[0, :, i * kt:(i + 1) * kt] if i < n_lat
              else vtc_ref[0, :, (i - n_lat) * kt:(i - n_lat + 1) * kt])
        return jnp.concatenate([vt, ones], axis=0)

    scores, tile_max, run_max, probs = {}, {}, {}, {}
    acc = [None] * n_blk
    n_items = n_blk * n_tiles
    for i in range(n_items + 2):
        if i < n_items:
            blk, c = divmod(i, n_tiles)
            scores[i] = _dot(k_tile(c), rhs[blk])
            tile_max[i] = jnp.max(scores[i], axis=0, keepdims=True)
        j = i - 1
        if 0 <= j < n_items:
            first = j % n_tiles == 0
            run_max[j] = tile_max[j] if first else jnp.maximum(run_max[j - 1], tile_max[j])
            probs[j] = jnp.exp2(scores.pop(j) - run_max[j]).astype(BF16)
        j = i - 2
        if 0 <= j < n_items:
            blk, c = divmod(j, n_tiles)
            pv = _dot(vt_tile(c), probs.pop(j))
            acc[blk] = pv if c == 0 else acc[blk] * jnp.exp2(run_max[j - 1] - run_max[j]) + pv
            if c == n_tiles - 1:
                o = acc[blk][0:HEAD_DIM] * (1.0 / acc[blk][HEAD_DIM:HEAD_DIM + 1])
                ot = jnp.concatenate([o[:, :tq], o[:, tq:]], axis=0)
                o_ref[0, blk * tq:(blk + 1) * tq, :] = ot.T.astype(BF16)


def _global_attn(qt, k, vt, kc, vtc, *, tq, kt, n_blk):
    b, w, s = qt.shape
    t = kc.shape[1]
    rows = n_blk * tq
    assert s % rows == 0 and s % kt == 0 and t % kt == 0
    return pl.pallas_call(
        functools.partial(_global_kernel, tq=tq, kt=kt),
        grid=(b, N_KV, s // rows),
        in_specs=[pl.BlockSpec((1, 2 * HEAD_DIM, rows), lambda bi, kv, i: (bi, kv, i)),
                  pl.BlockSpec((1, s, KV_W), lambda bi, kv, i: (bi, 0, 0)),
                  pl.BlockSpec((1, HEAD_DIM, s), lambda bi, kv, i: (bi, kv, 0)),
                  pl.BlockSpec((1, t, KV_W), lambda bi, kv, i: (bi, 0, 0)),
                  pl.BlockSpec((1, HEAD_DIM, t), lambda bi, kv, i: (bi, kv, 0))],
        out_specs=pl.BlockSpec((1, rows, 2 * HEAD_DIM), lambda bi, kv, i: (bi, i, kv)),
        out_shape=jax.ShapeDtypeStruct((b, s, w), BF16),
        compiler_params=_cparams(("parallel", "arbitrary", "arbitrary")),
        name="global_attn",
    )(qt, k, vt, kc, vtc)


def _hgrn_constants():
    n = HGRN_L
    tm = np.zeros((2, 2 * n, n), np.float32)
    mk = np.zeros((2, HGRN_LEVELS + 1, n, n), np.float32)
    for d in range(2):
        pos = np.arange(n) if d == 0 else n - 1 - np.arange(n)
        p, q = pos[:, None], pos[None, :]
        for li in range(HGRN_LEVELS):
            h = n >> (li + 1)
            same_blk = (p // (2 * h)) == (q // (2 * h))
            up_p, up_q = (p % (2 * h)) >= h, (q % (2 * h)) >= h
            mk[d, li] = same_blk & up_p & ~up_q
            if h == 2:
                same_half = same_blk & (up_p == up_q)
                tm[d, 0:n] = np.where(up_p, same_half & (q <= p), same_half & (q > p))
        mk[d, HGRN_LEVELS] = np.eye(n)
        tm[d, n:2 * n] = q <= p
    mkt = np.tile(mk[:, [HGRN_LEVELS, HGRN_LEVELS - 1]].transpose(0, 1, 3, 2), (1, 1, 1, N_HEADS))
    mk = np.tile(mk[:, :HGRN_LEVELS - 1], (1, 1, N_HEADS, 1))
    return jnp.asarray(tm, BF16), jnp.asarray(mk, F32), jnp.asarray(mkt, F32)


def _hgrn_seq_kernel(qf_ref, vf_ref, zf_ref, qb_ref, vb_ref, zb_ref, lb_ref, t_ref, mk_ref, mkt_ref, bd_ref,
                     *rest, g_chunks, has_init, emit_state):
    rest = list(rest)
    init_ref = rest.pop(0) if has_init else None
    of_ref, ob_ref = rest.pop(0), rest.pop(0)
    so_ref = rest.pop(0) if emit_state else None
    st_ref = rest.pop(0)
    step = pl.program_id(1)
    n = HGRN_L

    @pl.when(step == 0)
    def _():
        st_ref[...] = init_ref[0] if has_init else jnp.zeros_like(st_ref)

    ins = ((qf_ref, vf_ref, zf_ref), (qb_ref, vb_ref, zb_ref))
    outs = (of_ref, ob_ref)
    items = [(d, (g if d == 0 else g_chunks - 1 - g) * n) for g in range(g_chunks) for d in (0, 1)]

    row = lax.broadcasted_iota(jnp.int32, (n, GROUP_W), 0)
    work = {}

    def stage_gates(i):
        d, r0 = items[i]
        q = ins[d][0][0, r0:r0 + n, :]
        v = ins[d][1][0, r0:r0 + n, :]
        z = ins[d][2][0, r0:r0 + n, :]
        lb = lb_ref[d]
        f = lb + (1.0 - lb) * jax.nn.sigmoid(z)
        g = jnp.log(f)
        g_hi = g.astype(BF16)
        g_lo = (g - g_hi.astype(F32)).astype(BF16)
        e = _dot(t_ref[d], g_hi) + _dot(t_ref[d], g_lo)
        work[i] = dict(qq=_silu(q), kk=1.0 - f, f=f, v=v, e2=e[0:n], b=e[n:2 * n],
                       a=jnp.zeros((N_HEADS * n, n), F32))

    def level_scale(w, d, h):
        if h == 2:
            return jnp.exp(w["e2"])
        b = w["b"]
        blocks = []
        for j in range(n // (2 * h)):
            m = 2 * h * j + (h - 1 if d == 0 else h)
            blocks.append(jnp.broadcast_to(b[m:m + 1, :], (2 * h, GROUP_W)))
        bm = blocks[0] if len(blocks) == 1 else jnp.concatenate(blocks, axis=0)
        later = ((row // h) % 2) == (1 if d == 0 else 0)
        return jnp.exp(jnp.where(later, b - bm, bm - b))

    def stage_level(i, li):
        d, w = items[i][0], work[i]
        sc = level_scale(w, d, n >> (li + 1))
        qs, ks = w["qq"] * sc, w["kk"] * sc
        w["a"] = w["a"] + _dot_nt(_stack_heads(qs.astype(BF16)), ks.astype(BF16)) * mk_ref[d, li]

    head_ones = bd_ref[...].astype(BF16)

    def stage_pairs(i):
        d, w = items[i][0], work[i]
        qf = w["qq"] * w["f"]
        nb = pltpu.roll(qf, n - 1, 0) if d == 0 else pltpu.roll(qf, 1, 0)
        prod = jnp.concatenate([w["qq"] * w["kk"], nb * w["kk"]], axis=0).astype(BF16)
        sums = _dot(prod, head_ones)
        a_t = sums[0:n] * mkt_ref[d, 0] + sums[n:2 * n] * mkt_ref[d, 1]
        w["a"] = w["a"] + a_t.T

    st = [st_ref[0], st_ref[1]]

    def stage_out(i):
        (d, r0), w = items[i], work.pop(i)
        b_incl = w["b"]
        b_tot = b_incl[n - 1:n] if d == 0 else b_incl[0:1]
        vb = w["v"].astype(BF16)
        q_hat = (w["qq"] * jnp.exp(b_incl)).astype(BF16)
        k_hat = (w["kk"] * jnp.exp(b_tot - b_incl)).astype(BF16)
        o = _unstack_heads(_dot(w["a"].astype(BF16), vb), n)
        upd = _dot(w["v"].T.astype(BF16), k_hat) * bd_ref[...]
        outs[d][0, r0:r0 + n, :] = o + _dot_nt(q_hat, st[d].astype(BF16))
        st[d] = st[d] * jnp.exp(b_tot) + upd

    wave = min(HGRN_ITEMS_PER_WAVE, len(items))
    n_waves = len(items) // wave
    for w in range(n_waves + 2):
        if w < n_waves:
            for i in range(w * wave, (w + 1) * wave):
                stage_gates(i)
        if 0 <= w - 1 < n_waves:
            for li in range(HGRN_LEVELS - 1):
                for i in range((w - 1) * wave, w * wave):
                    stage_level(i, li)
            for i in range((w - 1) * wave, w * wave):
                stage_pairs(i)
        if 0 <= w - 2 < n_waves:
            for i in range((w - 2) * wave, (w - 1) * wave):
                stage_out(i)
    st_ref[0] = st[0]
    st_ref[1] = st[1]
    if emit_state:
        so_ref[0, 0] = st[0]
        so_ref[0, 1] = st[1]


def _hgrn_seq(c_raw, lb, consts, init_state, *, g_chunks, emit_state):
    tmat, masks, masks_t, bd = consts
    b, s, _ = c_raw.shape
    rows = g_chunks * HGRN_L
    assert s % rows == 0
    n_steps = s // rows
    has_init = init_state is not None
    fwd = lambda col: pl.BlockSpec((1, rows, GROUP_W), lambda bi, st: (bi, st, col))
    bwd = lambda col: pl.BlockSpec((1, rows, GROUP_W), lambda bi, st: (bi, n_steps - 1 - st, col))
    state_spec = pl.BlockSpec((1, 2, GROUP_W, GROUP_W), lambda bi, st: (bi, 0, 0, 0))
    in_specs = [fwd(0), fwd(1), fwd(2), bwd(0), bwd(1), bwd(3),
                _const_spec((2, 1, GROUP_W)), _const_spec(tmat.shape), _const_spec(masks.shape),
                _const_spec(masks_t.shape), _const_spec(bd.shape)]
    args = [c_raw] * 6 + [lb.reshape(2, 1, GROUP_W), tmat, masks, masks_t, bd]
    if has_init:
        in_specs.append(state_spec)
        args.append(init_state)
    out_specs = [fwd(0), bwd(0)]
    out_shape = [jax.ShapeDtypeStruct((b, s, GROUP_W), F32)] * 2
    if emit_state:
        out_specs.append(state_spec)
        out_shape.append(jax.ShapeDtypeStruct((b, 2, GROUP_W, GROUP_W), F32))
    return pl.pallas_call(
        functools.partial(_hgrn_seq_kernel, g_chunks=g_chunks, has_init=has_init, emit_state=emit_state),
        grid=(b, n_steps),
        in_specs=in_specs,
        out_specs=out_specs,
        out_shape=out_shape,
        scratch_shapes=[pltpu.VMEM((2, GROUP_W, GROUP_W), F32)],
        compiler_params=_cparams(("parallel", "arbitrary")),
        name="hgrn_ctx" if emit_state else "hgrn",
    )(*args)


def _outproj_kernel(x_ref, oa_ref, ob_ref, of_ref, obw_ref, gate_ref, od_ref, w_ref, onorm_ref, bd_ref,
                    ga_ref, lng_ref, lnb_ref, o_ref, *, alpha):
    tm = x_ref.shape[1]
    sub = min(tm, OUTPROJ_SUB_ROWS)
    acc = {}
    for i in range(tm // sub + 1):
        if i < tm // sub:
            r = slice(i * sub, (i + 1) * sub)
            oc = of_ref[0, r, :] + obw_ref[0, r, :]
            y = oc * lax.rsqrt(_head_mean_sq(oc, bd_ref[...]) + EPS) * onorm_ref[...]
            y = y * _silu(gate_ref[0, r, :])
            mix = jnp.concatenate([oa_ref[0, r, :], ob_ref[0, r, :], y.astype(BF16), od_ref[0, r, :]], axis=1)
            acc[i] = _dot(mix, w_ref[0])
        if i >= 1:
            r = slice((i - 1) * sub, i * sub)
            z = alpha * x_ref[0, r, :] + ga_ref[0] * acc.pop(i - 1)
            o_ref[0, r, :] = _layer_norm(z, lng_ref[...], lnb_ref[...])


def _outproj(x, oa, ob, oc_f, oc_b, c_raw, od, w_out, layer, onorm, bd256, gate_a, ln_g, ln_b, *, alpha, tm):
    b, s, d = x.shape
    assert s % tm == 0
    grp = pl.BlockSpec((1, tm, GROUP_W), lambda bi, i: (bi, i, 0))
    vec = pl.BlockSpec((1, 1, d), lambda bi, i: (bi, 0, 0))
    return pl.pallas_call(
        functools.partial(_outproj_kernel, alpha=alpha),
        grid=(b, s // tm),
        in_specs=[pl.BlockSpec((1, tm, d), lambda bi, i: (bi, i, 0)), grp, grp, grp, grp,
                  pl.BlockSpec((1, tm, GROUP_W), lambda bi, i: (bi, i, 4)),
                  grp, _layer_spec(w_out.shape, layer), _const_spec(onorm.shape), _const_spec(bd256.shape),
                  vec, _const_spec(ln_g.shape), _const_spec(ln_b.shape)],
        out_specs=pl.BlockSpec((1, tm, d), lambda bi, i: (bi, i, 0)),
        out_shape=jax.ShapeDtypeStruct((b, s, d), F32),
        compiler_params=_cparams(("parallel", "arbitrary")),
        name="outproj",
    )(x, oa, ob, oc_f, oc_b, c_raw, od, w_out, onorm, bd256, gate_a, ln_g, ln_b)


def _ffn_kernel(h_ref, hp_ref, hn_ref, sc_ref, sh_ref, gt_ref, wup_ref, cw_ref, cb_ref, wdn_ref,
                lng_ref, lnb_ref, o_ref, g_scr, *, alpha, d_ff, cwid, n_tiles):
    i = pl.program_id(1)
    tm = h_ref.shape[1]
    sc = 1.0 + sc_ref[0]
    sh = sh_ref[0]
    h = h_ref[0]
    halo = jnp.concatenate([hp_ref[0], hn_ref[0]], axis=0)
    lhs = jnp.concatenate([(h * sc + sh).astype(BF16), (halo * sc + sh).astype(BF16)], axis=0)
    has_prev = (i > 0).astype(F32)
    has_next = (i < n_tiles - 1).astype(F32)
    row = lax.broadcasted_iota(jnp.int32, (tm, cwid), 0)
    first, last = row == 0, row == tm - 1

    def up_cols(c0):
        return _dot(lhs, wup_ref[0, :, c0:c0 + cwid])

    def conv_cols(u, c0):
        um = u[:tm]
        u_prev = u[tm + 7:tm + 8] * has_prev
        u_next = u[tm + 8:tm + 9] * has_next
        dn = jnp.where(first, u_prev, pltpu.roll(um, 1, 0))
        up = jnp.where(last, u_next, pltpu.roll(um, tm - 1, 0))
        w = cw_ref[:, c0:c0 + cwid]
        return dn * w[0:1] + um * w[1:2] + up * w[2:3] + cb_ref[:, c0:c0 + cwid]

    for j in range(d_ff // cwid):
        a = conv_cols(up_cols(j * cwid), j * cwid)
        v = conv_cols(up_cols(d_ff + j * cwid), d_ff + j * cwid)
        g_scr[:, j * cwid:(j + 1) * cwid] = (_silu(a) * v).astype(BF16)
    z = alpha * h + gt_ref[0] * _dot(g_scr[...], wdn_ref[0])
    o_ref[0] = _layer_norm(z, lng_ref[...], lnb_ref[...])


def _ffn(h, scale, shift, gate, w_up, conv_w, conv_b, w_down, layer, ln_g, ln_b, *, alpha, tm, cwid):
    b, s, d = h.shape
    d_ff = w_down.shape[1]
    assert s % tm == 0 and tm % 16 == 0 and d_ff % cwid == 0
    n_tiles = s // tm
    r8 = tm // 8
    vec = pl.BlockSpec((1, 1, d), lambda bi, i: (bi, 0, 0))
    return pl.pallas_call(
        functools.partial(_ffn_kernel, alpha=alpha, d_ff=d_ff, cwid=cwid, n_tiles=n_tiles),
        grid=(b, n_tiles),
        in_specs=[pl.BlockSpec((1, tm, d), lambda bi, i: (bi, i, 0)),
                  pl.BlockSpec((1, 8, d), lambda bi, i: (bi, jnp.maximum(i * r8 - 1, 0), 0)),
                  pl.BlockSpec((1, 8, d), lambda bi, i: (bi, jnp.minimum((i + 1) * r8, s // 8 - 1), 0)),
                  vec, vec, vec,
                  _layer_spec(w_up.shape, layer), _const_spec(conv_w.shape), _const_spec(conv_b.shape),
                  _layer_spec(w_down.shape, layer), _const_spec(ln_g.shape), _const_spec(ln_b.shape)],
        out_specs=pl.BlockSpec((1, tm, d), lambda bi, i: (bi, i, 0)),
        out_shape=jax.ShapeDtypeStruct((b, s, d), F32),
        scratch_shapes=[pltpu.VMEM((tm, d_ff), BF16)],
        compiler_params=_cparams(("parallel", "arbitrary")),
        name="conv_ffn",
    )(h, h, h, scale, shift, gate, w_up, conv_w, conv_b, w_down, ln_g, ln_b)


def _rope_tables(n, with_rotation):
    lane = np.arange(V7X_LANES)
    if not with_rotation:
        one = jnp.ones((n, V7X_LANES), F32)
        zero = jnp.zeros((n, V7X_LANES), F32)
        return one, zero, zero
    t = np.arange(n)
    pos = np.stack([t // GRID_W, t % GRID_W], -1).astype(np.float64)
    n_freq = HEAD_DIM // 4
    inv = ROPE_BASE ** (-np.arange(n_freq, dtype=np.float64) / n_freq)
    ang = pos[:, :, None] * inv
    axis = (lane % HEAD_DIM) // (HEAD_DIM // 2)
    freq = lane % n_freq
    ang_l = ang[:, axis, freq]
    first = (lane % (2 * n_freq)) < n_freq
    cos, sin = np.cos(ang_l), np.sin(ang_l)
    tabs = (cos, np.where(first, -sin, 0.0), np.where(first, 0.0, sin))
    return tuple(jnp.asarray(a, F32) for a in tabs)


def _block_diag_mean(width):
    lane = np.arange(width)
    same = (lane[:, None] // HEAD_DIM) == (lane[None, :] // HEAD_DIM)
    return jnp.asarray(same / HEAD_DIM, BF16)


def kernel(x, c, ctx, c_ctx, w_mod, b_mod, w_in, w_out, ln_g, ln_b, attn_sink, qk_norm, hgrn_lb_logits,
           hgrn_onorm, na_rpb, ffn_w_up, ffn_conv_w, ffn_conv_b, ffn_w_down):
    depth = w_mod.shape[0]
    b, s, d = x.shape
    t = ctx.shape[1]
    alpha = (2 * depth) ** 0.25

    mods = _modulation(c, c_ctx, w_mod, b_mod)
    rope_lat = _rope_tables(s, True)
    rope_ctx = _rope_tables(t, False)
    bd128, bd256 = _block_diag_mean(KV_W), _block_diag_mean(GROUP_W)
    lane = np.arange(GROUP_W)
    head_bd = jnp.asarray((lane[:, None] // HEAD_DIM) == (lane[None, :] // HEAD_DIM), F32)
    hgrn_consts = _hgrn_constants() + (head_bd,)
    p_lb = jax.nn.softmax(hgrn_lb_logits.astype(F32), axis=1)
    lb_all = jnp.cumsum(p_lb, axis=1) - p_lb[:, :1]
    w_in_b, w_out_b = w_in.astype(BF16), w_out.astype(BF16)
    w_up_b, w_dn_b = ffn_w_up.astype(BF16), ffn_w_down.astype(BF16)

    h_x, h_c = x, ctx
    for l in range(depth):
        need_ctx = l < depth - 1
        mod = mods[l]
        sa, ca, ga, sf, cf, gf = (mod[:, i * d:(i + 1) * d] for i in range(6))
        lat = lambda a: a[:b, None, :]
        con = lambda a: jnp.broadcast_to(a[b][None, None, :], (b, 1, d))
        gq = jnp.tile(qk_norm[l, 0], 2)[None, :]
        gk = jnp.tile(qk_norm[l, 1], 2)[None, :]

        px = _inproj(h_x, lat(sa), lat(ca), w_in_b, l, rope_lat, gq, gk, bd128, is_ctx=False, tm=1024)
        pc = _inproj(h_c, con(sa), con(ca), w_in_b, l, rope_ctx, gq, gk, bd128, is_ctx=True, tm=t)
        c_raw, c_raw_c = px["c"], pc["c"]

        o_a = _window_attn(px["qa"], px["ka"], px["vat"], pc["ka"], pc["vat"], attn_sink[l],
                           n_pairs=WINDOW_PAIRS_PER_STEP)

        o_b = _global_attn(px["qb"], px["kb"], px["vbt"], pc["kb"], pc["vbt"], tq=256, kt=256,
                           n_blk=GLOBAL_BLOCKS_PER_STEP)

        ocf_c, ocb_c, ctx_state = _hgrn_seq(c_raw_c, lb_all[:, l], hgrn_consts, None,
                                            g_chunks=t // HGRN_L, emit_state=True)
        oc_f, oc_b = _hgrn_seq(c_raw, lb_all[:, l], hgrn_consts, ctx_state,
                               g_chunks=HGRN_CHUNKS_PER_STEP, emit_state=False)

        o_d = _na_attn(px["qd"], px["kd"], px["vd"], pc["kd"], pc["vd"], _na_bias_table(na_rpb[l]),
                       n_rows=NA_ROWS_PER_STEP)

        onorm = hgrn_onorm[l][None, :]
        lng0, lnb0 = ln_g[l, 0][None, :], ln_b[l, 0][None, :]
        lng1, lnb1 = ln_g[l, 1][None, :], ln_b[l, 1][None, :]
        cw, cb = ffn_conv_w[l], ffn_conv_b[l][None, :]
        h_x = _outproj(h_x, o_a, o_b, oc_f, oc_b, c_raw, o_d, w_out_b, l, onorm, bd256, lat(ga), lng0, lnb0,
                       alpha=alpha, tm=1024)
        h_x = _ffn(h_x, lat(cf), lat(sf), lat(gf), w_up_b, cw, cb, w_dn_b, l, lng1, lnb1,
                   alpha=alpha, tm=512, cwid=256)
        if need_ctx:
            oa_c, ob_c, od_c = _ctx_attn(attn_sink[l], pc["qa"], pc["kax"], pc["vax"], pc["qb"], pc["kbx"],
                                         pc["vbx"], pc["qd"], pc["kd"], pc["vd"])
            h_c = _outproj(h_c, oa_c, ob_c, ocf_c, ocb_c, c_raw_c, od_c, w_out_b, l, onorm, bd256, con(ga),
                           lng0, lnb0, alpha=alpha, tm=t)
            h_c = _ffn(h_c, con(cf), con(sf), con(gf), w_up_b, cw, cb, w_dn_b, l, lng1, lnb1,
                       alpha=alpha, tm=t, cwid=256)
    return h_x
```

```python
import functools

import numpy as np
import jax
import jax.numpy as jnp
from jax import lax
from jax.experimental import pallas as pl
from jax.experimental.pallas import tpu as pltpu

F32 = jnp.float32
BF16 = jnp.bfloat16

HEAD_DIM = 64
N_HEADS = 4
N_KV = 2
GROUP_W = N_HEADS * HEAD_DIM
KV_W = N_KV * HEAD_DIM
GRID_W = 64
WIN = 128
BLOCK = 128
NA_ROWS = 8
NA_COLS = 16
ROPE_BASE = 10000.0
EPS = 1e-6
HGRN_L = 64
HGRN_LEVELS = 6
HGRN_CHUNKS_PER_STEP = 8
HGRN_ITEMS_PER_WAVE = 2
WINDOW_PAIRS_PER_STEP = 4
WINDOW_ITEMS_PER_WAVE = 4
NA_ROWS_PER_STEP = 8
NA_ROWS_PER_WAVE = 2
GLOBAL_BLOCKS_PER_STEP = 4
OUTPROJ_SUB_ROWS = 256
ONES_ROWS = 16

V7X_LANES = 128
V7X_VMEM_LIMIT_BYTES = 56 * 1024 * 1024

COL_QA, COL_KA, COL_VA = 0, 256, 384
COL_QB, COL_KB, COL_VB = 512, 768, 896
COL_C = 1024
COL_QD, COL_KD, COL_VD = 2304, 2560, 2816
IN_WIDTH = 3072
C_WIDTH = 5 * GROUP_W

NEG_INF = float("-inf")
LOG2_E = 1.4426950408889634


def _cparams(sem):
    return pltpu.CompilerParams(dimension_semantics=sem, vmem_limit_bytes=V7X_VMEM_LIMIT_BYTES)


def _const_spec(shape):
    nd = len(shape)
    return pl.BlockSpec(shape, lambda *_: (0,) * nd, pipeline_mode=pl.Buffered(1))


def _layer_spec(shape, layer):
    nd = len(shape)
    return pl.BlockSpec((1,) + tuple(shape[1:]), lambda *_: (layer,) + (0,) * (nd - 1),
                        pipeline_mode=pl.Buffered(1))


def _silu(x):
    return x * jax.nn.sigmoid(x)


def _dot(a, b):
    return jnp.dot(a, b, preferred_element_type=F32)


def _dot_nt(a, b):
    return lax.dot_general(a, b, (((1,), (1,)), ((), ())), preferred_element_type=F32)


def _head_mask(shape, h):
    lane = lax.broadcasted_iota(jnp.int32, shape, len(shape) - 1)
    return (lane // HEAD_DIM) == h


def _stack_heads(q):
    zero = jnp.zeros_like(q)
    return jnp.concatenate([jnp.where(_head_mask(q.shape, h), q, zero) for h in range(N_HEADS)], axis=0)


def _unstack_heads(o, r):
    out = jnp.zeros((r, o.shape[1]), o.dtype)
    for h in range(N_HEADS):
        blk = o[h * r:(h + 1) * r]
        out = jnp.where(_head_mask(blk.shape, h), blk, out)
    return out


def _head_mean_sq(x, bd):
    sq = x * x
    hi = sq.astype(BF16)
    lo = (sq - hi.astype(F32)).astype(BF16)
    return _dot(hi, bd) + _dot(lo, bd)


def _layer_norm(z, g, b):
    mu = jnp.mean(z, axis=-1, keepdims=True)
    zc = z - mu
    var = jnp.mean(zc * zc, axis=-1, keepdims=True)
    return zc * lax.rsqrt(var + EPS) * g + b


def _mod_kernel(s_ref, w_ref, b_ref, o_ref):
    s = _silu(s_ref[...])
    o_ref[0] = jnp.dot(s, w_ref[0], preferred_element_type=F32,
                       precision=lax.Precision.HIGHEST) + b_ref[0]


def _modulation(c, c_ctx, w_mod, b_mod):
    depth, d, d6 = w_mod.shape
    b = c.shape[0]
    rows = 8
    assert b < rows
    s_in = jnp.zeros((rows, d), F32).at[:b].set(c).at[b].set(c_ctx)
    tn = d6 // 4
    return pl.pallas_call(
        _mod_kernel,
        grid=(depth, d6 // tn),
        in_specs=[pl.BlockSpec((rows, d), lambda l, j: (0, 0)),
                  pl.BlockSpec((1, d, tn), lambda l, j: (l, 0, j)),
                  pl.BlockSpec((1, 1, tn), lambda l, j: (l, 0, j))],
        out_specs=pl.BlockSpec((1, rows, tn), lambda l, j: (l, 0, j)),
        out_shape=jax.ShapeDtypeStruct((depth, rows, d6), F32),
        compiler_params=_cparams(("arbitrary", "arbitrary")),
        name="modulation",
    )(s_in, w_mod, b_mod.reshape(depth, 1, d6))


def _dup_heads(x):
    r = pltpu.roll(x, HEAD_DIM, 1)
    lo = lax.broadcasted_iota(jnp.int32, x.shape, 1) < HEAD_DIM
    return jnp.concatenate([jnp.where(lo, x, r), jnp.where(lo, r, x)], axis=1)


def _inproj_outputs(is_ctx):
    gqa = lambda g: [("q" + g, "row" if is_ctx else "col", GROUP_W, BF16), ("k" + g, "row", KV_W, BF16),
                     ("v" + g + "t", "col", KV_W, BF16)]
    dup = lambda g: [("k" + g + "x", "row", GROUP_W, BF16), ("v" + g + "x", "row", GROUP_W, BF16)]
    outs = gqa("a") + gqa("b")
    if is_ctx:
        outs += dup("a") + dup("b")
    return outs + [("c", "row", C_WIDTH, F32), ("qd", "row", GROUP_W, BF16), ("kd", "row", GROUP_W, BF16),
                   ("vd", "row", GROUP_W, BF16)]


def _inproj_kernel(h_ref, sc_ref, sh_ref, w_ref, rc_ref, ra_ref, rb_ref, gq_ref, gk_ref, bd_ref,
                   *outs, is_ctx):
    o = {name: ref for (name, _, _, _), ref in zip(_inproj_outputs(is_ctx), outs)}
    hm = (h_ref[0] * (1.0 + sc_ref[0]) + sh_ref[0]).astype(BF16)
    rc, ra, rb = rc_ref[...], ra_ref[...], rb_ref[...]
    bd = bd_ref[...]
    q_scale = HEAD_DIM ** -0.5 * (1.0 if is_ctx else LOG2_E)

    def proj(c0, width):
        return _dot(hm, w_ref[0, :, c0:c0 + width])

    def rope(x):
        return x * rc + pltpu.roll(x, V7X_LANES - 16, 1) * ra + pltpu.roll(x, 16, 1) * rb

    def rms(x, gain):
        return x * lax.rsqrt(_head_mean_sq(x, bd) + EPS) * gain

    def gqa_group(g, c_q, c_k, c_v, q_fn, k_fn):
        xq = proj(c_q, GROUP_W)
        for j in range(2):
            sl = slice(j * V7X_LANES, (j + 1) * V7X_LANES)
            y = rope(q_fn(xq[:, sl])) * q_scale
            if is_ctx:
                o["q" + g][0, :, sl] = y.astype(BF16)
            else:
                o["q" + g][0, sl, :] = y.T.astype(BF16)
        k = rope(k_fn(proj(c_k, KV_W)))
        v = proj(c_v, KV_W)
        o["k" + g][0] = k.astype(BF16)
        o["v" + g + "t"][0] = v.T.astype(BF16)
        if is_ctx:
            o["k" + g + "x"][0] = _dup_heads(k).astype(BF16)
            o["v" + g + "x"][0] = _dup_heads(v).astype(BF16)

    gq, gk = gq_ref[...], gk_ref[...]
    gqa_group("a", COL_QA, COL_KA, COL_VA, lambda x: x, lambda x: x)
    gqa_group("b", COL_QB, COL_KB, COL_VB, lambda x: rms(x, gq), lambda x: rms(x, gk))

    for j in range(C_WIDTH // GROUP_W):
        o["c"][0, :, j * GROUP_W:(j + 1) * GROUP_W] = proj(COL_C + j * GROUP_W, GROUP_W)

    o["qd"][0] = (proj(COL_QD, GROUP_W) * HEAD_DIM ** -0.5).astype(BF16)
    o["kd"][0] = proj(COL_KD, GROUP_W).astype(BF16)
    o["vd"][0] = proj(COL_VD, GROUP_W).astype(BF16)


def _inproj(h, shift, scale, w_in, layer, rope_tabs, gq, gk, bd128, *, is_ctx, tm):
    b, s, d = h.shape
    assert s % tm == 0
    rc, ra, rb = rope_tabs
    outs = _inproj_outputs(is_ctx)
    out_specs = [pl.BlockSpec((1, tm, w), lambda bi, i: (bi, i, 0)) if lay == "row"
                 else pl.BlockSpec((1, w, tm), lambda bi, i: (bi, 0, i)) for _, lay, w, _ in outs]
    out_shape = [jax.ShapeDtypeStruct((b, s, w) if lay == "row" else (b, w, s), dt) for _, lay, w, dt in outs]
    tab = pl.BlockSpec((tm, V7X_LANES), lambda bi, i: (i, 0))
    vec = pl.BlockSpec((1, 1, d), lambda bi, i: (bi, 0, 0))
    res = pl.pallas_call(
        functools.partial(_inproj_kernel, is_ctx=is_ctx),
        grid=(b, s // tm),
        in_specs=[pl.BlockSpec((1, tm, d), lambda bi, i: (bi, i, 0)), vec, vec,
                  _layer_spec(w_in.shape, layer), tab, tab, tab,
                  _const_spec(gq.shape), _const_spec(gk.shape), _const_spec(bd128.shape)],
        out_specs=out_specs,
        out_shape=out_shape,
        compiler_params=_cparams(("parallel", "arbitrary")),
        name="inproj_ctx" if is_ctx else "inproj",
    )(h, scale, shift, w_in, rc, ra, rb, gq, gk, bd128)
    return {name: r for (name, _, _, _), r in zip(outs, res)}


def _attend_many(items, wave=None):
    n = len(items)
    wave = wave or n
    scores, maxes, probs, dens, outs = {}, {}, {}, {}, {}

    def stage_scores(i):
        qs, parts, sink_col = items[i]
        ss, m = [], sink_col
        for k, _, bias in parts:
            s = _dot_nt(qs, k)
            if bias is not None:
                s = s + bias
            ss.append(s)
            smax = jnp.max(s, axis=1, keepdims=True)
            m = smax if m is None else jnp.maximum(m, smax)
        scores[i], maxes[i] = ss, m

    def stage_probs(i):
        sink_col, m = items[i][2], maxes[i]
        den = None if sink_col is None else jnp.exp(sink_col - m)
        ps = []
        for s in scores.pop(i):
            p = jnp.exp(s - m)
            psum = jnp.sum(p, axis=1, keepdims=True)
            den = psum if den is None else den + psum
            ps.append(p.astype(BF16))
        probs[i], dens[i] = ps, den

    def stage_pv(i):
        out = None
        for p, (_, v, _) in zip(probs.pop(i), items[i][1]):
            pv = _dot(p, v)
            out = pv if out is None else out + pv
        outs[i] = out * (1.0 / dens[i])

    n_waves = -(-n // wave)
    for w in range(n_waves + 2):
        for stage, lag in ((stage_scores, 0), (stage_probs, 1), (stage_pv, 2)):
            ww = w - lag
            if 0 <= ww < n_waves:
                for i in range(ww * wave, min((ww + 1) * wave, n)):
                    stage(i)
    return [outs[i] for i in range(n)]


def _window_kernel(sink_ref, q_ref, *refs, n_pairs, n_blocks):
    n_kblk = 2 * n_pairs + 2
    k_refs, vt_refs = refs[:n_kblk], refs[n_kblk:2 * n_kblk]
    kc_ref, vtc_ref, o_ref = refs[2 * n_kblk:]
    step = pl.program_id(1)
    tq = 2 * BLOCK
    first_blk = step * 2 * n_pairs - 1
    ones_blk = jnp.ones((ONES_ROWS, BLOCK), BF16)
    ones_ctx = jnp.ones((ONES_ROWS, kc_ref.shape[1]), BF16)
    ki = lax.broadcasted_iota(jnp.int32, (BLOCK, tq), 0)
    qi = lax.broadcasted_iota(jnp.int32, (BLOCK, tq), 1)
    lane = lax.broadcasted_iota(jnp.int32, (1, 2 * tq), 1)
    zeros_q = jnp.zeros((HEAD_DIM, tq), BF16)

    def band_bias(jj, j):
        ok = jnp.abs((1 - jj) * BLOCK + qi - ki) <= WIN
        if j == 0:
            ok = ok & (first_blk >= 0)
        elif j == n_kblk - 1:
            ok = ok & (first_blk + j < n_blocks)
        bias = jnp.where(ok, 0.0, NEG_INF).astype(F32)
        return jnp.concatenate([bias, bias], axis=1)

    items = [(p, kv) for p in range(n_pairs) for kv in range(N_KV)]
    biases = {(p, jj): band_bias(jj, 2 * p + jj) for p in range(n_pairs) for jj in range(4)}

    scores, maxes, probs = {}, {}, {}

    def stage_scores(it):
        p, kv = it
        qt = q_ref[0, 2 * HEAD_DIM * kv:2 * HEAD_DIM * (kv + 1), p * tq:(p + 1) * tq]

        def padded(g):
            qg = qt[g * HEAD_DIM:(g + 1) * HEAD_DIM, :]
            return jnp.concatenate([qg, zeros_q] if kv == 0 else [zeros_q, qg], axis=0)

        rhs = jnp.concatenate([padded(0), padded(1)], axis=1)
        ss = [_dot(k_refs[2 * p + jj][0], rhs) + biases[(p, jj)] for jj in range(4)]
        ss.append(_dot(kc_ref[0], rhs))
        sink2 = jnp.where(lane < tq, sink_ref[0, 2 * kv], sink_ref[0, 2 * kv + 1]) * LOG2_E
        m = sink2
        for s in ss:
            m = jnp.maximum(m, jnp.max(s, axis=0, keepdims=True))
        scores[it], maxes[it] = ss, (m, sink2)

    def stage_probs(it):
        m = maxes[it][0]
        probs[it] = [jnp.exp2(s - m).astype(BF16) for s in scores.pop(it)]

    def stage_pv(it):
        p, kv = it
        ps = probs.pop(it)
        acc = None
        for jj in range(4):
            vt = vt_refs[2 * p + jj][0, kv * HEAD_DIM:(kv + 1) * HEAD_DIM, :]
            pv = _dot(jnp.concatenate([vt, ones_blk], axis=0), ps[jj])
            acc = pv if acc is None else acc + pv
        vtc = vtc_ref[0, kv * HEAD_DIM:(kv + 1) * HEAD_DIM, :]
        acc = acc + _dot(jnp.concatenate([vtc, ones_ctx], axis=0), ps[4])
        m, sink2 = maxes[(p, kv)]
        den = acc[HEAD_DIM:HEAD_DIM + 1] + jnp.exp2(sink2 - m)
        o = acc[0:HEAD_DIM] * (1.0 / den)
        ot = jnp.concatenate([o[:, :tq], o[:, tq:]], axis=0)
        o_ref[0, p * tq:(p + 1) * tq, 2 * HEAD_DIM * kv:2 * HEAD_DIM * (kv + 1)] = ot.T.astype(BF16)

    wave = min(WINDOW_ITEMS_PER_WAVE, len(items))
    n_waves = len(items) // wave
    for w in range(n_waves + 2):
        for stage, lag in ((stage_scores, 0), (stage_probs, 1), (stage_pv, 2)):
            if 0 <= w - lag < n_waves:
                for it in items[(w - lag) * wave:(w - lag + 1) * wave]:
                    stage(it)


def _window_attn(qt, k, vt, kc, vtc, sink, *, n_pairs):
    b, w, s = qt.shape
    t = kc.shape[1]
    rows = n_pairs * 2 * BLOCK
    n_blocks = s // BLOCK
    n_kblk = 2 * n_pairs + 2
    assert s % rows == 0

    def blk_idx(i, j):
        return jnp.clip(i * 2 * n_pairs - 1 + j, 0, n_blocks - 1)

    k_specs = [pl.BlockSpec((1, BLOCK, KV_W), functools.partial(lambda bi, i, j: (bi, blk_idx(i, j), 0), j=j))
               for j in range(n_kblk)]
    vt_specs = [pl.BlockSpec((1, KV_W, BLOCK), functools.partial(lambda bi, i, j: (bi, 0, blk_idx(i, j)), j=j))
                for j in range(n_kblk)]
    return pl.pallas_call(
        functools.partial(_window_kernel, n_pairs=n_pairs, n_blocks=n_blocks),
        grid=(b, s // rows),
        in_specs=[pl.BlockSpec(memory_space=pltpu.SMEM),
                  pl.BlockSpec((1, w, rows), lambda bi, i: (bi, 0, i))] + k_specs + vt_specs +
                 [pl.BlockSpec((1, t, KV_W), lambda bi, i: (bi, 0, 0)),
                  pl.BlockSpec((1, KV_W, t), lambda bi, i: (bi, 0, 0))],
        out_specs=pl.BlockSpec((1, rows, w), lambda bi, i: (bi, i, 0)),
        out_shape=jax.ShapeDtypeStruct((b, s, w), BF16),
        compiler_params=_cparams(("parallel", "arbitrary")),
        name="window_attn",
    )(sink.reshape(1, N_HEADS).astype(F32), qt, *([k] * n_kblk), *([vt] * n_kblk), kc, vtc)


def _na_kernel(q_ref, k_ref, v_ref, kc_ref, vc_ref, bias_ref, o_ref, *, rows, n_rows):
    step = pl.program_id(1)
    span = NA_ROWS * GRID_W
    ctx_part = (kc_ref[0], vc_ref[0], None)
    items = []
    for j in range(n_rows):
        r = step * n_rows + j
        r0 = jnp.clip(r - NA_ROWS // 2, 0, rows - NA_ROWS)
        start = pl.multiple_of(r0 * GRID_W, GRID_W)
        kw = k_ref[0, pl.ds(start, span), :]
        vw = v_ref[0, pl.ds(start, span), :]
        qs = _stack_heads(q_ref[0, j * GRID_W:(j + 1) * GRID_W, :])
        off0 = r0 - r + (NA_ROWS - 1)
        bias = jnp.concatenate([bias_ref[off0 + 2 * i] for i in range(NA_ROWS // 2)], axis=1)
        items.append((qs, [(kw, vw, bias), ctx_part], None))
    for j, o in enumerate(_attend_many(items, wave=NA_ROWS_PER_WAVE)):
        o_ref[0, j * GRID_W:(j + 1) * GRID_W, :] = _unstack_heads(o, GRID_W).astype(BF16)


def _na_bias_table(rpb):
    n_h, n_ro, n_co = rpb.shape
    q = np.arange(GRID_W)
    kc = np.arange(GRID_W)
    coff = np.clip(kc[None, :] - q[:, None], -(NA_COLS - 1), NA_COLS - 1) + (NA_COLS - 1)
    c0 = np.clip(q - NA_COLS // 2, 0, GRID_W - NA_COLS)
    ok = (kc[None, :] >= c0[:, None]) & (kc[None, :] < c0[:, None] + NA_COLS)
    onehot = np.zeros((n_co, GRID_W, GRID_W), np.float32)
    onehot[coff, q[:, None], kc[None, :]] = 1.0
    tab = jnp.einsum("hrc,cqk->hrqk", rpb.astype(F32), jnp.asarray(onehot), precision=lax.Precision.HIGHEST)
    tab = jnp.where(ok[None, None], tab, NEG_INF)
    pairs = jnp.concatenate([tab[:, :-1], tab[:, 1:]], axis=-1)
    return pairs.transpose(1, 0, 2, 3).reshape(n_ro - 1, n_h * GRID_W, 2 * GRID_W)


def _na_attn(q, k, v, kc, vc, bias_tab, *, n_rows):
    b, s, w = q.shape
    t = kc.shape[1]
    rows = s // GRID_W
    assert s % GRID_W == 0 and rows >= NA_ROWS and rows % n_rows == 0
    full = lambda n: pl.BlockSpec((1, n, w), lambda bi, i: (bi, 0, 0))
    blk = n_rows * GRID_W
    return pl.pallas_call(
        functools.partial(_na_kernel, rows=rows, n_rows=n_rows),
        grid=(b, rows // n_rows),
        in_specs=[pl.BlockSpec((1, blk, w), lambda bi, r: (bi, r, 0)),
                  full(s), full(s), full(t), full(t), _const_spec(bias_tab.shape)],
        out_specs=pl.BlockSpec((1, blk, w), lambda bi, r: (bi, r, 0)),
        out_shape=jax.ShapeDtypeStruct((b, s, w), BF16),
        compiler_params=_cparams(("parallel", "arbitrary")),
        name="na_attn",
    )(q, k, v, kc, vc, bias_tab)


def _ctx_attn_kernel(sink_ref, qa_ref, ka_ref, va_ref, qb_ref, kb_ref, vb_ref, qd_ref, kd_ref, vd_ref,
                     oa_ref, ob_ref, od_ref):
    t = qa_ref.shape[1]
    sink_col = jnp.concatenate([jnp.full((t, 1), sink_ref[0, h], F32) for h in range(N_HEADS)], axis=0)
    groups = ((qa_ref, ka_ref, va_ref, sink_col), (qb_ref, kb_ref, vb_ref, None), (qd_ref, kd_ref, vd_ref, None))
    outs = _attend_many([(_stack_heads(q_ref[0]), [(k_ref[0], v_ref[0], None)], sink)
                         for q_ref, k_ref, v_ref, sink in groups])
    for o_ref, o in zip((oa_ref, ob_ref, od_ref), outs):
        o_ref[0] = _unstack_heads(o, t).astype(BF16)


def _ctx_attn(sink, qa, ka, va, qb, kb, vb, qd, kd, vd):
    b, t, w = qa.shape
    blk = pl.BlockSpec((1, t, w), lambda bi: (bi, 0, 0))
    out = jax.ShapeDtypeStruct((b, t, w), BF16)
    return pl.pallas_call(
        _ctx_attn_kernel,
        grid=(b,),
        in_specs=[pl.BlockSpec(memory_space=pltpu.SMEM)] + [blk] * 9,
        out_specs=[blk] * 3,
        out_shape=[out] * 3,
        compiler_params=_cparams(("arbitrary",)),
        name="ctx_attn",
    )(sink.reshape(1, N_HEADS).astype(F32), qa, ka, va, qb, kb, vb, qd, kd, vd)


def _global_kernel(q_ref, k_ref, vt_ref, kc_ref, vtc_ref, o_ref, *, tq, kt):
    kv = pl.program_id(1)
    n_blk = q_ref.shape[2] // tq
    sel = (lax.broadcasted_iota(jnp.int32, (2 * HEAD_DIM, tq), 0) // HEAD_DIM) == kv

    def block_rhs(blk):
        qt = q_ref[0, :, blk * tq:(blk + 1) * tq]
        zero = jnp.zeros_like(qt)

        def padded(g):
            qg = qt[g * HEAD_DIM:(g + 1) * HEAD_DIM, :]
            return jnp.where(sel, jnp.concatenate([qg, qg], axis=0), zero)

        return jnp.concatenate([padded(0), padded(1)], axis=1)

    rhs = [block_rhs(blk) for blk in range(n_blk)]
    n_lat = k_ref.shape[1] // kt
    n_tiles = n_lat + kc_ref.shape[1] // kt
    ones = jnp.ones((ONES_ROWS, kt), BF16)

    def k_tile(i):
        return k_ref[0, i * kt:(i + 1) * kt, :] if i < n_lat else kc_ref[0, (i - n_lat) * kt:(i - n_lat + 1) * kt, :]

    def vt_tile(i):
        vt = (vt_ref[0, :, i * kt:(i + 1) * kt] if i < n_lat
              else vtc_ref[0, :, (i - n_lat) * kt:(i - n_lat + 1) * kt])
        return jnp.concatenate([vt, ones], axis=0)

    scores, tile_max, run_max, probs = {}, {}, {}, {}
    acc = [None] * n_blk
    n_items = n_blk * n_tiles
    for i in range(n_items + 2):
        if i < n_items:
            blk, c = divmod(i, n_tiles)
            scores[i] = _dot(k_tile(c), rhs[blk])
            tile_max[i] = jnp.max(scores[i], axis=0, keepdims=True)
        j = i - 1
        if 0 <= j < n_items:
            first = j % n_tiles == 0
            run_max[j] = tile_max[j] if first else jnp.maximum(run_max[j - 1], tile_max[j])
            probs[j] = jnp.exp2(scores.pop(j) - run_max[j]).astype(BF16)
        j = i - 2
        if 0 <= j < n_items:
            blk, c = divmod(j, n_tiles)
            pv = _dot(vt_tile(c), probs.pop(j))
            acc[blk] = pv if c == 0 else acc[blk] * jnp.exp2(run_max[j - 1] - run_max[j]) + pv
            if c == n_tiles - 1:
                o = acc[blk][0:HEAD_DIM] * (1.0 / acc[blk][HEAD_DIM:HEAD_DIM + 1])
                ot = jnp.concatenate([o[:, :tq], o[:, tq:]], axis=0)
                o_ref[0, blk * tq:(blk + 1) * tq, :] = ot.T.astype(BF16)


def _global_attn(qt, k, vt, kc, vtc, *, tq, kt, n_blk):
    b, w, s = qt.shape
    t = kc.shape[1]
    rows = n_blk * tq
    assert s % rows == 0 and s % kt == 0 and t % kt == 0
    return pl.pallas_call(
        functools.partial(_global_kernel, tq=tq, kt=kt),
        grid=(b, N_KV, s // rows),
        in_specs=[pl.BlockSpec((1, 2 * HEAD_DIM, rows), lambda bi, kv, i: (bi, kv, i)),
                  pl.BlockSpec((1, s, KV_W), lambda bi, kv, i: (bi, 0, 0)),
                  pl.BlockSpec((1, HEAD_DIM, s), lambda bi, kv, i: (bi, kv, 0)),
                  pl.BlockSpec((1, t, KV_W), lambda bi, kv, i: (bi, 0, 0)),
                  pl.BlockSpec((1, HEAD_DIM, t), lambda bi, kv, i: (bi, kv, 0))],
        out_specs=pl.BlockSpec((1, rows, 2 * HEAD_DIM), lambda bi, kv, i: (bi, i, kv)),
        out_shape=jax.ShapeDtypeStruct((b, s, w), BF16),
        compiler_params=_cparams(("parallel", "arbitrary", "arbitrary")),
        name="global_attn",
    )(qt, k, vt, kc, vtc)


def _hgrn_constants():
    n = HGRN_L
    tm = np.zeros((2, 2 * n, n), np.float32)
    mk = np.zeros((2, HGRN_LEVELS + 1, n, n), np.float32)
    for d in range(2):
        pos = np.arange(n) if d == 0 else n - 1 - np.arange(n)
        p, q = pos[:, None], pos[None, :]
        for li in range(HGRN_LEVELS):
            h = n >> (li + 1)
            same_blk = (p // (2 * h)) == (q // (2 * h))
            up_p, up_q = (p % (2 * h)) >= h, (q % (2 * h)) >= h
            mk[d, li] = same_blk & up_p & ~up_q
            if h == 2:
                same_half = same_blk & (up_p == up_q)
                tm[d, 0:n] = np.where(up_p, same_half & (q <= p), same_half & (q > p))
        mk[d, HGRN_LEVELS] = np.eye(n)
        tm[d, n:2 * n] = q <= p
    mkt = np.tile(mk[:, [HGRN_LEVELS, HGRN_LEVELS - 1]].transpose(0, 1, 3, 2), (1, 1, 1, N_HEADS))
    mk = np.tile(mk[:, :HGRN_LEVELS - 1], (1, 1, N_HEADS, 1))
    return jnp.asarray(tm, BF16), jnp.asarray(mk, F32), jnp.asarray(mkt, F32)


def _hgrn_seq_kernel(qf_ref, vf_ref, zf_ref, qb_ref, vb_ref, zb_ref, lb_ref, t_ref, mk_ref, mkt_ref, bd_ref,
                     *rest, g_chunks, has_init, emit_state):
    rest = list(rest)
    init_ref = rest.pop(0) if has_init else None
    of_ref, ob_ref = rest.pop(0), rest.pop(0)
    so_ref = rest.pop(0) if emit_state else None
    st_ref = rest.pop(0)
    step = pl.program_id(1)
    n = HGRN_L

    @pl.when(step == 0)
    def _():
        st_ref[...] = init_ref[0] if has_init else jnp.zeros_like(st_ref)

    ins = ((qf_ref, vf_ref, zf_ref), (qb_ref, vb_ref, zb_ref))
    outs = (of_ref, ob_ref)
    items = [(d, (g if d == 0 else g_chunks - 1 - g) * n) for g in range(g_chunks) for d in (0, 1)]

    row = lax.broadcasted_iota(jnp.int32, (n, GROUP_W), 0)
    work = {}

    def stage_gates(i):
        d, r0 = items[i]
        q = ins[d][0][0, r0:r0 + n, :]
        v = ins[d][1][0, r0:r0 + n, :]
        z = ins[d][2][0, r0:r0 + n, :]
        lb = lb_ref[d]
        f = lb + (1.0 - lb) * jax.nn.sigmoid(z)
        g = jnp.log(f)
        g_hi = g.astype(BF16)
        g_lo = (g - g_hi.astype(F32)).astype(BF16)
        e = _dot(t_ref[d], g_hi) + _dot(t_ref[d], g_lo)
        work[i] = dict(qq=_silu(q), kk=1.0 - f, f=f, v=v, e2=e[0:n], b=e[n:2 * n],
                       a=jnp.zeros((N_HEADS * n, n), F32))

    def level_scale(w, d, h):
        if h == 2:
            return jnp.exp(w["e2"])
        b = w["b"]
        blocks = []
        for j in range(n // (2 * h)):
            m = 2 * h * j + (h - 1 if d == 0 else h)
            blocks.append(jnp.broadcast_to(b[m:m + 1, :], (2 * h, GROUP_W)))
        bm = blocks[0] if len(blocks) == 1 else jnp.concatenate(blocks, axis=0)
        later = ((row // h) % 2) == (1 if d == 0 else 0)
        return jnp.exp(jnp.where(later, b - bm, bm - b))

    def stage_level(i, li):
        d, w = items[i][0], work[i]
        h = n >> (li + 1)
        sc = level_scale(w, d, h)
        qs, ks = w["qq"] * sc, (w["kk"] * sc).astype(BF16)
        if h % 8:
            w["a"] = w["a"] + _dot_nt(_stack_heads(qs.astype(BF16)), ks) * mk_ref[d, li]
            return
        later = [j for j in range(n // h) if j % 2 == (1 if d == 0 else 0)]
        rows = lambda x, r0: jnp.concatenate([x[r0 + j * h:r0 + (j + 1) * h] for j in later], axis=0)
        res = _dot_nt(_stack_heads(rows(qs, 0).astype(BF16)), ks)
        res = res * jnp.concatenate([rows(mk_ref[d, li], hh * n) for hh in range(N_HEADS)], axis=0)
        zero = jnp.zeros((h, n), F32)
        pieces = []
        for hh in range(N_HEADS):
            for j in range(n // h):
                k = hh * (n // 2) + later.index(j) * h if j in later else None
                pieces.append(zero if k is None else res[k:k + h])
        w["a"] = w["a"] + jnp.concatenate(pieces, axis=0)

    head_ones = bd_ref[...].astype(BF16)

    def stage_pairs(i):
        d, w = items[i][0], work[i]
        qf = w["qq"] * w["f"]
        nb = pltpu.roll(qf, n - 1, 0) if d == 0 else pltpu.roll(qf, 1, 0)
        prod = jnp.concatenate([w["qq"] * w["kk"], nb * w["kk"]], axis=0).astype(BF16)
        sums = _dot(prod, head_ones)
        a_t = sums[0:n] * mkt_ref[d, 0] + sums[n:2 * n] * mkt_ref[d, 1]
        w["a"] = w["a"] + a_t.T

    st = [st_ref[0], st_ref[1]]

    def stage_out(i):
        (d, r0), w = items[i], work.pop(i)
        b_incl = w["b"]
        b_tot = b_incl[n - 1:n] if d == 0 else b_incl[0:1]
        vb = w["v"].astype(BF16)
        q_hat = (w["qq"] * jnp.exp(b_incl)).astype(BF16)
        k_hat = (w["kk"] * jnp.exp(b_tot - b_incl)).astype(BF16)
        o = _unstack_heads(_dot(w["a"].astype(BF16), vb), n)
        upd = _dot(w["v"].T.astype(BF16), k_hat) * bd_ref[...]
        outs[d][0, r0:r0 + n, :] = o + _dot_nt(q_hat, st[d].astype(BF16))
        st[d] = st[d] * jnp.exp(b_tot) + upd

    wave = min(HGRN_ITEMS_PER_WAVE, len(items))
    n_waves = len(items) // wave
    for w in range(n_waves + 2):
        if w < n_waves:
            for i in range(w * wave, (w + 1) * wave):
                stage_gates(i)
        if 0 <= w - 1 < n_waves:
            for li in range(HGRN_LEVELS - 1):
                for i in range((w - 1) * wave, w * wave):
                    stage_level(i, li)
            for i in range((w - 1) * wave, w * wave):
                stage_pairs(i)
        if 0 <= w - 2 < n_waves:
            for i in range((w - 2) * wave, (w - 1) * wave):
                stage_out(i)
    st_ref[0] = st[0]
    st_ref[1] = st[1]
    if emit_state:
        so_ref[0, 0] = st[0]
        so_ref[0, 1] = st[1]


def _hgrn_seq(c_raw, lb, consts, init_state, *, g_chunks, emit_state):
    tmat, masks, masks_t, bd = consts
    b, s, _ = c_raw.shape
    rows = g_chunks * HGRN_L
    assert s % rows == 0
    n_steps = s // rows
    has_init = init_state is not None
    fwd = lambda col: pl.BlockSpec((1, rows, GROUP_W), lambda bi, st: (bi, st, col))
    bwd = lambda col: pl.BlockSpec((1, rows, GROUP_W), lambda bi, st: (bi, n_steps - 1 - st, col))
    state_spec = pl.BlockSpec((1, 2, GROUP_W, GROUP_W), lambda bi, st: (bi, 0, 0, 0))
    in_specs = [fwd(0), fwd(1), fwd(2), bwd(0), bwd(1), bwd(3),
                _const_spec((2, 1, GROUP_W)), _const_spec(tmat.shape), _const_spec(masks.shape),
                _const_spec(masks_t.shape), _const_spec(bd.shape)]
    args = [c_raw] * 6 + [lb.reshape(2, 1, GROUP_W), tmat, masks, masks_t, bd]
    if has_init:
        in_specs.append(state_spec)
        args.append(init_state)
    out_specs = [fwd(0), bwd(0)]
    out_shape = [jax.ShapeDtypeStruct((b, s, GROUP_W), F32)] * 2
    if emit_state:
        out_specs.append(state_spec)
        out_shape.append(jax.ShapeDtypeStruct((b, 2, GROUP_W, GROUP_W), F32))
    return pl.pallas_call(
        functools.partial(_hgrn_seq_kernel, g_chunks=g_chunks, has_init=has_init, emit_state=emit_state),
        grid=(b, n_steps),
        in_specs=in_specs,
        out_specs=out_specs,
        out_shape=out_shape,
        scratch_shapes=[pltpu.VMEM((2, GROUP_W, GROUP_W), F32)],
        compiler_params=_cparams(("parallel", "arbitrary")),
        name="hgrn_ctx" if emit_state else "hgrn",
    )(*args)


def _outproj_kernel(x_ref, oa_ref, ob_ref, of_ref, obw_ref, gate_ref, od_ref, w_ref, onorm_ref, bd_ref,
                    ga_ref, lng_ref, lnb_ref, o_ref, *, alpha):
    tm = x_ref.shape[1]
    sub = min(tm, OUTPROJ_SUB_ROWS)
    acc = {}
    for i in range(tm // sub + 1):
        if i < tm // sub:
            r = slice(i * sub, (i + 1) * sub)
            oc = of_ref[0, r, :] + obw_ref[0, r, :]
            y = oc * lax.rsqrt(_head_mean_sq(oc, bd_ref[...]) + EPS) * onorm_ref[...]
            y = y * _silu(gate_ref[0, r, :])
            mix = jnp.concatenate([oa_ref[0, r, :], ob_ref[0, r, :], y.astype(BF16), od_ref[0, r, :]], axis=1)
            acc[i] = _dot(mix, w_ref[0])
        if i >= 1:
            r = slice((i - 1) * sub, i * sub)
            z = alpha * x_ref[0, r, :] + ga_ref[0] * acc.pop(i - 1)
            o_ref[0, r, :] = _layer_norm(z, lng_ref[...], lnb_ref[...])


def _outproj(x, oa, ob, oc_f, oc_b, c_raw, od, w_out, layer, onorm, bd256, gate_a, ln_g, ln_b, *, alpha, tm):
    b, s, d = x.shape
    assert s % tm == 0
    grp = pl.BlockSpec((1, tm, GROUP_W), lambda bi, i: (bi, i, 0))
    vec = pl.BlockSpec((1, 1, d), lambda bi, i: (bi, 0, 0))
    return pl.pallas_call(
        functools.partial(_outproj_kernel, alpha=alpha),
        grid=(b, s // tm),
        in_specs=[pl.BlockSpec((1, tm, d), lambda bi, i: (bi, i, 0)), grp, grp, grp, grp,
                  pl.BlockSpec((1, tm, GROUP_W), lambda bi, i: (bi, i, 4)),
                  grp, _layer_spec(w_out.shape, layer), _const_spec(onorm.shape), _const_spec(bd256.shape),
                  vec, _const_spec(ln_g.shape), _const_spec(ln_b.shape)],
        out_specs=pl.BlockSpec((1, tm, d), lambda bi, i: (bi, i, 0)),
        out_shape=jax.ShapeDtypeStruct((b, s, d), F32),
        compiler_params=_cparams(("parallel", "arbitrary")),
        name="outproj",
    )(x, oa, ob, oc_f, oc_b, c_raw, od, w_out, onorm, bd256, gate_a, ln_g, ln_b)


def _ffn_kernel(h_ref, hp_ref, hn_ref, sc_ref, sh_ref, gt_ref, wup_ref, cw_ref, cb_ref, wdn_ref,
                lng_ref, lnb_ref, o_ref, g_scr, *, alpha, d_ff, cwid, n_tiles):
    i = pl.program_id(1)
    tm = h_ref.shape[1]
    sc = 1.0 + sc_ref[0]
    sh = sh_ref[0]
    h = h_ref[0]
    halo = jnp.concatenate([hp_ref[0], hn_ref[0]], axis=0)
    lhs = jnp.concatenate([(h * sc + sh).astype(BF16), (halo * sc + sh).astype(BF16)], axis=0)
    has_prev = (i > 0).astype(F32)
    has_next = (i < n_tiles - 1).astype(F32)
    row = lax.broadcasted_iota(jnp.int32, (tm, cwid), 0)
    first, last = row == 0, row == tm - 1

    def up_cols(c0):
        return _dot(lhs, wup_ref[0, :, c0:c0 + cwid])

    def conv_cols(u, c0):
        um = u[:tm]
        u_prev = u[tm + 7:tm + 8] * has_prev
        u_next = u[tm + 8:tm + 9] * has_next
        dn = jnp.where(first, u_prev, pltpu.roll(um, 1, 0))
        up = jnp.where(last, u_next, pltpu.roll(um, tm - 1, 0))
        w = cw_ref[:, c0:c0 + cwid]
        return dn * w[0:1] + um * w[1:2] + up * w[2:3] + cb_ref[:, c0:c0 + cwid]

    for j in range(d_ff // cwid):
        a = conv_cols(up_cols(j * cwid), j * cwid)
        v = conv_cols(up_cols(d_ff + j * cwid), d_ff + j * cwid)
        g_scr[:, j * cwid:(j + 1) * cwid] = (_silu(a) * v).astype(BF16)
    z = alpha * h + gt_ref[0] * _dot(g_scr[...], wdn_ref[0])
    o_ref[0] = _layer_norm(z, lng_ref[...], lnb_ref[...])


def _ffn(h, scale, shift, gate, w_up, conv_w, conv_b, w_down, layer, ln_g, ln_b, *, alpha, tm, cwid):
    b, s, d = h.shape
    d_ff = w_down.shape[1]
    assert s % tm == 0 and tm % 16 == 0 and d_ff % cwid == 0
    n_tiles = s // tm
    r8 = tm // 8
    vec = pl.BlockSpec((1, 1, d), lambda bi, i: (bi, 0, 0))
    return pl.pallas_call(
        functools.partial(_ffn_kernel, alpha=alpha, d_ff=d_ff, cwid=cwid, n_tiles=n_tiles),
        grid=(b, n_tiles),
        in_specs=[pl.BlockSpec((1, tm, d), lambda bi, i: (bi, i, 0)),
                  pl.BlockSpec((1, 8, d), lambda bi, i: (bi, jnp.maximum(i * r8 - 1, 0), 0)),
                  pl.BlockSpec((1, 8, d), lambda bi, i: (bi, jnp.minimum((i + 1) * r8, s // 8 - 1), 0)),
                  vec, vec, vec,
                  _layer_spec(w_up.shape, layer), _const_spec(conv_w.shape), _const_spec(conv_b.shape),
                  _layer_spec(w_down.shape, layer), _const_spec(ln_g.shape), _const_spec(ln_b.shape)],
        out_specs=pl.BlockSpec((1, tm, d), lambda bi, i: (bi, i, 0)),
        out_shape=jax.ShapeDtypeStruct((b, s, d), F32),
        scratch_shapes=[pltpu.VMEM((tm, d_ff), BF16)],
        compiler_params=_cparams(("parallel", "arbitrary")),
        name="conv_ffn",
    )(h, h, h, scale, shift, gate, w_up, conv_w, conv_b, w_down, ln_g, ln_b)


def _rope_tables(n, with_rotation):
    lane = np.arange(V7X_LANES)
    if not with_rotation:
        one = jnp.ones((n, V7X_LANES), F32)
        zero = jnp.zeros((n, V7X_LANES), F32)
        return one, zero, zero
    t = np.arange(n)
    pos = np.stack([t // GRID_W, t % GRID_W], -1).astype(np.float64)
    n_freq = HEAD_DIM // 4
    inv = ROPE_BASE ** (-np.arange(n_freq, dtype=np.float64) / n_freq)
    ang = pos[:, :, None] * inv
    axis = (lane % HEAD_DIM) // (HEAD_DIM // 2)
    freq = lane % n_freq
    ang_l = ang[:, axis, freq]
    first = (lane % (2 * n_freq)) < n_freq
    cos, sin = np.cos(ang_l), np.sin(ang_l)
    tabs = (cos, np.where(first, -sin, 0.0), np.where(first, 0.0, sin))
    return tuple(jnp.asarray(a, F32) for a in tabs)


def _block_diag_mean(width):
    lane = np.arange(width)
    same = (lane[:, None] // HEAD_DIM) == (lane[None, :] // HEAD_DIM)
    return jnp.asarray(same / HEAD_DIM, BF16)


def kernel(x, c, ctx, c_ctx, w_mod, b_mod, w_in, w_out, ln_g, ln_b, attn_sink, qk_norm, hgrn_lb_logits,
           hgrn_onorm, na_rpb, ffn_w_up, ffn_conv_w, ffn_conv_b, ffn_w_down):
    depth = w_mod.shape[0]
    b, s, d = x.shape
    t = ctx.shape[1]
    alpha = (2 * depth) ** 0.25

    mods = _modulation(c, c_ctx, w_mod, b_mod)
    rope_lat = _rope_tables(s, True)
    rope_ctx = _rope_tables(t, False)
    bd128, bd256 = _block_diag_mean(KV_W), _block_diag_mean(GROUP_W)
    lane = np.arange(GROUP_W)
    head_bd = jnp.asarray((lane[:, None] // HEAD_DIM) == (lane[None, :] // HEAD_DIM), F32)
    hgrn_consts = _hgrn_constants() + (head_bd,)
    p_lb = jax.nn.softmax(hgrn_lb_logits.astype(F32), axis=1)
    lb_all = jnp.cumsum(p_lb, axis=1) - p_lb[:, :1]
    w_in_b, w_out_b = w_in.astype(BF16), w_out.astype(BF16)
    w_up_b, w_dn_b = ffn_w_up.astype(BF16), ffn_w_down.astype(BF16)

    h_x, h_c = x, ctx
    for l in range(depth):
        need_ctx = l < depth - 1
        mod = mods[l]
        sa, ca, ga, sf, cf, gf = (mod[:, i * d:(i + 1) * d] for i in range(6))
        lat = lambda a: a[:b, None, :]
        con = lambda a: jnp.broadcast_to(a[b][None, None, :], (b, 1, d))
        gq = jnp.tile(qk_norm[l, 0], 2)[None, :]
        gk = jnp.tile(qk_norm[l, 1], 2)[None, :]

        px = _inproj(h_x, lat(sa), lat(ca), w_in_b, l, rope_lat, gq, gk, bd128, is_ctx=False, tm=1024)
        pc = _inproj(h_c, con(sa), con(ca), w_in_b, l, rope_ctx, gq, gk, bd128, is_ctx=True, tm=t)
        c_raw, c_raw_c = px["c"], pc["c"]

        o_a = _window_attn(px["qa"], px["ka"], px["vat"], pc["ka"], pc["vat"], attn_sink[l],
                           n_pairs=WINDOW_PAIRS_PER_STEP)

        o_b = _global_attn(px["qb"], px["kb"], px["vbt"], pc["kb"], pc["vbt"], tq=256, kt=256,
                           n_blk=GLOBAL_BLOCKS_PER_STEP)

        ocf_c, ocb_c, ctx_state = _hgrn_seq(c_raw_c, lb_all[:, l], hgrn_consts, None,
                                            g_chunks=t // HGRN_L, emit_state=True)
        oc_f, oc_b = _hgrn_seq(c_raw, lb_all[:, l], hgrn_consts, ctx_state,
                               g_chunks=HGRN_CHUNKS_PER_STEP, emit_state=False)

        o_d = _na_attn(px["qd"], px["kd"], px["vd"], pc["kd"], pc["vd"], _na_bias_table(na_rpb[l]),
                       n_rows=NA_ROWS_PER_STEP)

        onorm = hgrn_onorm[l][None, :]
        lng0, lnb0 = ln_g[l, 0][None, :], ln_b[l, 0][None, :]
        lng1, lnb1 = ln_g[l, 1][None, :], ln_b[l, 1][None, :]
        cw, cb = ffn_conv_w[l], ffn_conv_b[l][None, :]
        h_x = _outproj(h_x, o_a, o_b, oc_f, oc_b, c_raw, o_d, w_out_b, l, onorm, bd256, lat(ga), lng0, lnb0,
                       alpha=alpha, tm=1024)
        h_x = _ffn(h_x, lat(cf), lat(sf), lat(gf), w_up_b, cw, cb, w_dn_b, l, lng1, lnb1,
                   alpha=alpha, tm=512, cwid=256)
        if need_ctx:
            oa_c, ob_c, od_c = _ctx_attn(attn_sink[l], pc["qa"], pc["kax"], pc["vax"], pc["qb"], pc["kbx"],
                                         pc["vbx"], pc["qd"], pc["kd"], pc["vd"])
            h_c = _outproj(h_c, oa_c, ob_c, ocf_c, ocb_c, c_raw_c, od_c, w_out_b, l, onorm, bd256, con(ga),
                           lng0, lnb0, alpha=alpha, tm=t)
            h_c = _ffn(h_c, con(cf), con(sf), con(gf), w_up_b, cw, cb, w_dn_b, l, lng1, lnb1,
                       alpha=alpha, tm=t, cwid=256)
    return h_x
```

```python
import functools

import numpy as np
import jax
import jax.numpy as jnp
from jax import lax
from jax.experimental import pallas as pl
from jax.experimental.pallas import tpu as pltpu

F32 = jnp.float32
BF16 = jnp.bfloat16

HEAD_DIM = 64
N_HEADS = 4
N_KV = 2
GROUP_W = N_HEADS * HEAD_DIM
KV_W = N_KV * HEAD_DIM
GRID_W = 64
WIN = 128
BLOCK = 128
NA_ROWS = 8
NA_COLS = 16
ROPE_BASE = 10000.0
EPS = 1e-6
HGRN_L = 64
HGRN_LEVELS = 6
HGRN_CHUNKS_PER_STEP = 8
HGRN_ITEMS_PER_WAVE = 2
WINDOW_PAIRS_PER_STEP = 4
WINDOW_ITEMS_PER_WAVE = 4
NA_ROWS_PER_STEP = 8
NA_ROWS_PER_WAVE = 2
GLOBAL_BLOCKS_PER_STEP = 4
OUTPROJ_SUB_ROWS = 256
ONES_ROWS = 16

V7X_LANES = 128
V7X_VMEM_LIMIT_BYTES = 56 * 1024 * 1024

COL_QA, COL_KA, COL_VA = 0, 256, 384
COL_QB, COL_KB, COL_VB = 512, 768, 896
COL_C = 1024
COL_QD, COL_KD, COL_VD = 2304, 2560, 2816
IN_WIDTH = 3072
C_WIDTH = 5 * GROUP_W

NEG_INF = float("-inf")
LOG2_E = 1.4426950408889634


def _cparams(sem):
    return pltpu.CompilerParams(dimension_semantics=sem, vmem_limit_bytes=V7X_VMEM_LIMIT_BYTES)


def _const_spec(shape):
    nd = len(shape)
    return pl.BlockSpec(shape, lambda *_: (0,) * nd, pipeline_mode=pl.Buffered(1))


def _layer_spec(shape, layer):
    nd = len(shape)
    return pl.BlockSpec((1,) + tuple(shape[1:]), lambda *_: (layer,) + (0,) * (nd - 1),
                        pipeline_mode=pl.Buffered(1))


def _silu(x):
    return x * jax.nn.sigmoid(x)


def _dot(a, b):
    return jnp.dot(a, b, preferred_element_type=F32)


def _dot_nt(a, b):
    return lax.dot_general(a, b, (((1,), (1,)), ((), ())), preferred_element_type=F32)


def _head_mask(shape, h):
    lane = lax.broadcasted_iota(jnp.int32, shape, len(shape) - 1)
    return (lane // HEAD_DIM) == h


def _stack_heads(q):
    zero = jnp.zeros_like(q)
    return jnp.concatenate([jnp.where(_head_mask(q.shape, h), q, zero) for h in range(N_HEADS)], axis=0)


def _unstack_heads(o, r):
    out = jnp.zeros((r, o.shape[1]), o.dtype)
    for h in range(N_HEADS):
        blk = o[h * r:(h + 1) * r]
        out = jnp.where(_head_mask(blk.shape, h), blk, out)
    return out


def _head_mean_sq(x, bd):
    sq = x * x
    hi = sq.astype(BF16)
    lo = (sq - hi.astype(F32)).astype(BF16)
    return _dot(hi, bd) + _dot(lo, bd)


def _layer_norm(z, g, b):
    mu = jnp.mean(z, axis=-1, keepdims=True)
    zc = z - mu
    var = jnp.mean(zc * zc, axis=-1, keepdims=True)
    return zc * lax.rsqrt(var + EPS) * g + b


def _mod_kernel(s_ref, w_ref, b_ref, o_ref):
    s = _silu(s_ref[...])
    s_hi = s.astype(BF16).astype(F32)
    s_mid = (s - s_hi).astype(BF16).astype(F32)
    s_lo = s - s_hi - s_mid
    w = w_ref[0]
    w_hi = w.astype(BF16)
    w_lo = (w - w_hi.astype(F32)).astype(BF16)
    rows = s.shape[0]
    lhs = jnp.concatenate([s_hi, s_mid, s_lo, jnp.zeros_like(s)], axis=0).astype(BF16)
    p_hi = _dot(lhs, w_hi)
    p_lo = _dot(lhs[0:2 * rows], w_lo)
    o_ref[0] = (p_hi[0:rows] + p_hi[rows:2 * rows] + p_hi[2 * rows:3 * rows]
                + p_lo[0:rows] + p_lo[rows:2 * rows] + b_ref[0])


def _modulation(c, c_ctx, w_mod, b_mod):
    depth, d, d6 = w_mod.shape
    b = c.shape[0]
    rows = 8
    assert b < rows
    s_in = jnp.zeros((rows, d), F32).at[:b].set(c).at[b].set(c_ctx)
    tn = d6 // 4
    return pl.pallas_call(
        _mod_kernel,
        grid=(depth, d6 // tn),
        in_specs=[pl.BlockSpec((rows, d), lambda l, j: (0, 0)),
                  pl.BlockSpec((1, d, tn), lambda l, j: (l, 0, j)),
                  pl.BlockSpec((1, 1, tn), lambda l, j: (l, 0, j))],
        out_specs=pl.BlockSpec((1, rows, tn), lambda l, j: (l, 0, j)),
        out_shape=jax.ShapeDtypeStruct((depth, rows, d6), F32),
        compiler_params=_cparams(("arbitrary", "arbitrary")),
        name="modulation",
    )(s_in, w_mod, b_mod.reshape(depth, 1, d6))


def _dup_heads(x):
    r = pltpu.roll(x, HEAD_DIM, 1)
    lo = lax.broadcasted_iota(jnp.int32, x.shape, 1) < HEAD_DIM
    return jnp.concatenate([jnp.where(lo, x, r), jnp.where(lo, r, x)], axis=1)


def _inproj_outputs(is_ctx):
    gqa = lambda g: [("q" + g, "row" if is_ctx else "col", GROUP_W, BF16), ("k" + g, "row", KV_W, BF16),
                     ("v" + g + "t", "col", KV_W, BF16)]
    dup = lambda g: [("k" + g + "x", "row", GROUP_W, BF16), ("v" + g + "x", "row", GROUP_W, BF16)]
    outs = gqa("a") + gqa("b")
    if is_ctx:
        outs += dup("a") + dup("b")
    return outs + [("c", "row", C_WIDTH, F32), ("qd", "row", GROUP_W, BF16), ("kd", "row", GROUP_W, BF16),
                   ("vd", "row", GROUP_W, BF16)]


def _inproj_kernel(h_ref, sc_ref, sh_ref, w_ref, rc_ref, ra_ref, rb_ref, gq_ref, gk_ref, bd_ref,
                   *outs, is_ctx):
    o = {name: ref for (name, _, _, _), ref in zip(_inproj_outputs(is_ctx), outs)}
    hm = (h_ref[0] * (1.0 + sc_ref[0]) + sh_ref[0]).astype(BF16)
    rc, ra, rb = rc_ref[...], ra_ref[...], rb_ref[...]
    bd = bd_ref[...]
    q_scale = HEAD_DIM ** -0.5 * (1.0 if is_ctx else LOG2_E)

    def proj(c0, width):
        return _dot(hm, w_ref[0, :, c0:c0 + width].astype(BF16))

    def rope(x):
        return x * rc + pltpu.roll(x, V7X_LANES - 16, 1) * ra + pltpu.roll(x, 16, 1) * rb

    def rms(x, gain):
        return x * lax.rsqrt(_head_mean_sq(x, bd) + EPS) * gain

    def gqa_group(g, c_q, c_k, c_v, q_fn, k_fn):
        xq = proj(c_q, GROUP_W)
        for j in range(2):
            sl = slice(j * V7X_LANES, (j + 1) * V7X_LANES)
            y = rope(q_fn(xq[:, sl])) * q_scale
            if is_ctx:
                o["q" + g][0, :, sl] = y.astype(BF16)
            else:
                o["q" + g][0, sl, :] = y.T.astype(BF16)
        k = rope(k_fn(proj(c_k, KV_W)))
        v = proj(c_v, KV_W)
        o["k" + g][0] = k.astype(BF16)
        o["v" + g + "t"][0] = v.T.astype(BF16)
        if is_ctx:
            o["k" + g + "x"][0] = _dup_heads(k).astype(BF16)
            o["v" + g + "x"][0] = _dup_heads(v).astype(BF16)

    gq, gk = gq_ref[...], gk_ref[...]
    gqa_group("a", COL_QA, COL_KA, COL_VA, lambda x: x, lambda x: x)
    gqa_group("b", COL_QB, COL_KB, COL_VB, lambda x: rms(x, gq), lambda x: rms(x, gk))

    for j in range(C_WIDTH // GROUP_W):
        o["c"][0, :, j * GROUP_W:(j + 1) * GROUP_W] = proj(COL_C + j * GROUP_W, GROUP_W)

    o["qd"][0] = (proj(COL_QD, GROUP_W) * HEAD_DIM ** -0.5).astype(BF16)
    o["kd"][0] = proj(COL_KD, GROUP_W).astype(BF16)
    o["vd"][0] = proj(COL_VD, GROUP_W).astype(BF16)


def _inproj(h, shift, scale, w_in, layer, rope_tabs, gq, gk, bd128, *, is_ctx, tm):
    b, s, d = h.shape
    assert s % tm == 0
    rc, ra, rb = rope_tabs
    outs = _inproj_outputs(is_ctx)
    out_specs = [pl.BlockSpec((1, tm, w), lambda bi, i: (bi, i, 0)) if lay == "row"
                 else pl.BlockSpec((1, w, tm), lambda bi, i: (bi, 0, i)) for _, lay, w, _ in outs]
    out_shape = [jax.ShapeDtypeStruct((b, s, w) if lay == "row" else (b, w, s), dt) for _, lay, w, dt in outs]
    tab = pl.BlockSpec((tm, V7X_LANES), lambda bi, i: (i, 0))
    vec = pl.BlockSpec((1, 1, d), lambda bi, i: (bi, 0, 0))
    res = pl.pallas_call(
        functools.partial(_inproj_kernel, is_ctx=is_ctx),
        grid=(b, s // tm),
        in_specs=[pl.BlockSpec((1, tm, d), lambda bi, i: (bi, i, 0)), vec, vec,
                  _layer_spec(w_in.shape, layer), tab, tab, tab,
                  _const_spec(gq.shape), _const_spec(gk.shape), _const_spec(bd128.shape)],
        out_specs=out_specs,
        out_shape=out_shape,
        compiler_params=_cparams(("parallel", "arbitrary")),
        name="inproj_ctx" if is_ctx else "inproj",
    )(h, scale, shift, w_in, rc, ra, rb, gq, gk, bd128)
    return {name: r for (name, _, _, _), r in zip(outs, res)}


def _attend_many(items, wave=None):
    n = len(items)
    wave = wave or n
    scores, maxes, probs, dens, outs = {}, {}, {}, {}, {}

    def stage_scores(i):
        qs, parts, sink_col = items[i]
        ss, m = [], sink_col
        for k, _, bias in parts:
            s = _dot_nt(qs, k)
            if bias is not None:
                s = s + bias
            ss.append(s)
            smax = jnp.max(s, axis=1, keepdims=True)
            m = smax if m is None else jnp.maximum(m, smax)
        scores[i], maxes[i] = ss, m

    def stage_probs(i):
        sink_col, m = items[i][2], maxes[i]
        den = None if sink_col is None else jnp.exp(sink_col - m)
        ps = []
        for s in scores.pop(i):
            p = jnp.exp(s - m)
            psum = jnp.sum(p, axis=1, keepdims=True)
            den = psum if den is None else den + psum
            ps.append(p.astype(BF16))
        probs[i], dens[i] = ps, den

    def stage_pv(i):
        out = None
        for p, (_, v, _) in zip(probs.pop(i), items[i][1]):
            pv = _dot(p, v)
            out = pv if out is None else out + pv
        outs[i] = out * (1.0 / dens[i])

    n_waves = -(-n // wave)
    for w in range(n_waves + 2):
        for stage, lag in ((stage_scores, 0), (stage_probs, 1), (stage_pv, 2)):
            ww = w - lag
            if 0 <= ww < n_waves:
                for i in range(ww * wave, min((ww + 1) * wave, n)):
                    stage(i)
    return [outs[i] for i in range(n)]


def _window_kernel(sink_ref, q_ref, *refs, n_pairs, n_blocks):
    n_kblk = 2 * n_pairs + 2
    k_refs, vt_refs = refs[:n_kblk], refs[n_kblk:2 * n_kblk]
    kc_ref, vtc_ref, o_ref = refs[2 * n_kblk:]
    step = pl.program_id(1)
    tq = 2 * BLOCK
    first_blk = step * 2 * n_pairs - 1
    ones_blk = jnp.ones((ONES_ROWS, BLOCK), BF16)
    ones_ctx = jnp.ones((ONES_ROWS, kc_ref.shape[1]), BF16)
    ki = lax.broadcasted_iota(jnp.int32, (BLOCK, tq), 0)
    qi = lax.broadcasted_iota(jnp.int32, (BLOCK, tq), 1)
    lane = lax.broadcasted_iota(jnp.int32, (1, 2 * tq), 1)
    zeros_q = jnp.zeros((HEAD_DIM, tq), BF16)

    def band_bias(jj, j):
        ok = jnp.abs((1 - jj) * BLOCK + qi - ki) <= WIN
        if j == 0:
            ok = ok & (first_blk >= 0)
        elif j == n_kblk - 1:
            ok = ok & (first_blk + j < n_blocks)
        bias = jnp.where(ok, 0.0, NEG_INF).astype(F32)
        return jnp.concatenate([bias, bias], axis=1)

    items = [(p, kv) for p in range(n_pairs) for kv in range(N_KV)]
    biases = {(p, jj): band_bias(jj, 2 * p + jj) for p in range(n_pairs) for jj in range(4)}

    scores, maxes, probs = {}, {}, {}

    def stage_scores(it):
        p, kv = it
        qt = q_ref[0, 2 * HEAD_DIM * kv:2 * HEAD_DIM * (kv + 1), p * tq:(p + 1) * tq]

        def padded(g):
            qg = qt[g * HEAD_DIM:(g + 1) * HEAD_DIM, :]
            return jnp.concatenate([qg, zeros_q] if kv == 0 else [zeros_q, qg], axis=0)

        rhs = jnp.concatenate([padded(0), padded(1)], axis=1)
        ss = [_dot(k_refs[2 * p + jj][0], rhs) + biases[(p, jj)] for jj in range(4)]
        ss.append(_dot(kc_ref[0], rhs))
        sink2 = jnp.where(lane < tq, sink_ref[0, 2 * kv], sink_ref[0, 2 * kv + 1]) * LOG2_E
        m = sink2
        for s in ss:
            m = jnp.maximum(m, jnp.max(s, axis=0, keepdims=True))
        scores[it], maxes[it] = ss, (m, sink2)

    def stage_probs(it):
        m = maxes[it][0]
        probs[it] = [jnp.exp2(s - m).astype(BF16) for s in scores.pop(it)]

    def stage_pv(it):
        p, kv = it
        ps = probs.pop(it)
        acc = None
        for jj in range(4):
            vt = vt_refs[2 * p + jj][0, kv * HEAD_DIM:(kv + 1) * HEAD_DIM, :]
            pv = _dot(jnp.concatenate([vt, ones_blk], axis=0), ps[jj])
            acc = pv if acc is None else acc + pv
        vtc = vtc_ref[0, kv * HEAD_DIM:(kv + 1) * HEAD_DIM, :]
        acc = acc + _dot(jnp.concatenate([vtc, ones_ctx], axis=0), ps[4])
        m, sink2 = maxes[(p, kv)]
        den = acc[HEAD_DIM:HEAD_DIM + 1] + jnp.exp2(sink2 - m)
        o = acc[0:HEAD_DIM] * (1.0 / den)
        ot = jnp.concatenate([o[:, :tq], o[:, tq:]], axis=0)
        o_ref[0, p * tq:(p + 1) * tq, 2 * HEAD_DIM * kv:2 * HEAD_DIM * (kv + 1)] = ot.T.astype(BF16)

    wave = min(WINDOW_ITEMS_PER_WAVE, len(items))
    n_waves = len(items) // wave
    for w in range(n_waves + 2):
        for stage, lag in ((stage_scores, 0), (stage_probs, 1), (stage_pv, 2)):
            if 0 <= w - lag < n_waves:
                for it in items[(w - lag) * wave:(w - lag + 1) * wave]:
                    stage(it)


def _window_attn(qt, k, vt, kc, vtc, sink, *, n_pairs):
    b, w, s = qt.shape
    t = kc.shape[1]
    rows = n_pairs * 2 * BLOCK
    n_blocks = s // BLOCK
    n_kblk = 2 * n_pairs + 2
    assert s % rows == 0

    def blk_idx(i, j):
        return jnp.clip(i * 2 * n_pairs - 1 + j, 0, n_blocks - 1)

    k_specs = [pl.BlockSpec((1, BLOCK, KV_W), functools.partial(lambda bi, i, j: (bi, blk_idx(i, j), 0), j=j))
               for j in range(n_kblk)]
    vt_specs = [pl.BlockSpec((1, KV_W, BLOCK), functools.partial(lambda bi, i, j: (bi, 0, blk_idx(i, j)), j=j))
                for j in range(n_kblk)]
    return pl.pallas_call(
        functools.partial(_window_kernel, n_pairs=n_pairs, n_blocks=n_blocks),
        grid=(b, s // rows),
        in_specs=[pl.BlockSpec(memory_space=pltpu.SMEM),
                  pl.BlockSpec((1, w, rows), lambda bi, i: (bi, 0, i))] + k_specs + vt_specs +
                 [pl.BlockSpec((1, t, KV_W), lambda bi, i: (bi, 0, 0)),
                  pl.BlockSpec((1, KV_W, t), lambda bi, i: (bi, 0, 0))],
        out_specs=pl.BlockSpec((1, rows, w), lambda bi, i: (bi, i, 0)),
        out_shape=jax.ShapeDtypeStruct((b, s, w), BF16),
        compiler_params=_cparams(("parallel", "arbitrary")),
        name="window_attn",
    )(sink.reshape(1, N_HEADS).astype(F32), qt, *([k] * n_kblk), *([vt] * n_kblk), kc, vtc)


def _na_kernel(q_ref, k_ref, v_ref, kc_ref, vc_ref, bias_ref, o_ref, *, rows, n_rows):
    step = pl.program_id(1)
    span = NA_ROWS * GRID_W
    ctx_part = (kc_ref[0], vc_ref[0], None)
    items = []
    for j in range(n_rows):
        r = step * n_rows + j
        r0 = jnp.clip(r - NA_ROWS // 2, 0, rows - NA_ROWS)
        start = pl.multiple_of(r0 * GRID_W, GRID_W)
        kw = k_ref[0, pl.ds(start, span), :]
        vw = v_ref[0, pl.ds(start, span), :]
        qs = _stack_heads(q_ref[0, j * GRID_W:(j + 1) * GRID_W, :])
        off0 = r0 - r + (NA_ROWS - 1)
        bias = jnp.concatenate([bias_ref[off0 + 2 * i] for i in range(NA_ROWS // 2)], axis=1)
        items.append((qs, [(kw, vw, bias), ctx_part], None))
    for j, o in enumerate(_attend_many(items, wave=NA_ROWS_PER_WAVE)):
        o_ref[0, j * GRID_W:(j + 1) * GRID_W, :] = _unstack_heads(o, GRID_W).astype(BF16)


def _na_bias_table(rpb):
    n_h, n_ro, n_co = rpb.shape
    q = np.arange(GRID_W)
    kc = np.arange(GRID_W)
    coff = np.clip(kc[None, :] - q[:, None], -(NA_COLS - 1), NA_COLS - 1) + (NA_COLS - 1)
    c0 = np.clip(q - NA_COLS // 2, 0, GRID_W - NA_COLS)
    ok = (kc[None, :] >= c0[:, None]) & (kc[None, :] < c0[:, None] + NA_COLS)
    onehot = np.zeros((n_co, GRID_W, GRID_W), np.float32)
    onehot[coff, q[:, None], kc[None, :]] = 1.0
    tab = jnp.einsum("hrc,cqk->hrqk", rpb.astype(F32), jnp.asarray(onehot), precision=lax.Precision.HIGHEST)
    tab = jnp.where(ok[None, None], tab, NEG_INF)
    pairs = jnp.concatenate([tab[:, :-1], tab[:, 1:]], axis=-1)
    return pairs.transpose(1, 0, 2, 3).reshape(n_ro - 1, n_h * GRID_W, 2 * GRID_W)


def _na_attn(q, k, v, kc, vc, bias_tab, *, n_rows):
    b, s, w = q.shape
    t = kc.shape[1]
    rows = s // GRID_W
    assert s % GRID_W == 0 and rows >= NA_ROWS and rows % n_rows == 0
    full = lambda n: pl.BlockSpec((1, n, w), lambda bi, i: (bi, 0, 0))
    blk = n_rows * GRID_W
    return pl.pallas_call(
        functools.partial(_na_kernel, rows=rows, n_rows=n_rows),
        grid=(b, rows // n_rows),
        in_specs=[pl.BlockSpec((1, blk, w), lambda bi, r: (bi, r, 0)),
                  full(s), full(s), full(t), full(t), _const_spec(bias_tab.shape)],
        out_specs=pl.BlockSpec((1, blk, w), lambda bi, r: (bi, r, 0)),
        out_shape=jax.ShapeDtypeStruct((b, s, w), BF16),
        compiler_params=_cparams(("parallel", "arbitrary")),
        name="na_attn",
    )(q, k, v, kc, vc, bias_tab)


def _ctx_attn_kernel(sink_ref, qa_ref, ka_ref, va_ref, qb_ref, kb_ref, vb_ref, qd_ref, kd_ref, vd_ref,
                     oa_ref, ob_ref, od_ref):
    t = qa_ref.shape[1]
    sink_col = jnp.concatenate([jnp.full((t, 1), sink_ref[0, h], F32) for h in range(N_HEADS)], axis=0)
    groups = ((qa_ref, ka_ref, va_ref, sink_col), (qb_ref, kb_ref, vb_ref, None), (qd_ref, kd_ref, vd_ref, None))
    outs = _attend_many([(_stack_heads(q_ref[0]), [(k_ref[0], v_ref[0], None)], sink)
                         for q_ref, k_ref, v_ref, sink in groups])
    for o_ref, o in zip((oa_ref, ob_ref, od_ref), outs):
        o_ref[0] = _unstack_heads(o, t).astype(BF16)


def _ctx_attn(sink, qa, ka, va, qb, kb, vb, qd, kd, vd):
    b, t, w = qa.shape
    blk = pl.BlockSpec((1, t, w), lambda bi: (bi, 0, 0))
    out = jax.ShapeDtypeStruct((b, t, w), BF16)
    return pl.pallas_call(
        _ctx_attn_kernel,
        grid=(b,),
        in_specs=[pl.BlockSpec(memory_space=pltpu.SMEM)] + [blk] * 9,
        out_specs=[blk] * 3,
        out_shape=[out] * 3,
        compiler_params=_cparams(("arbitrary",)),
        name="ctx_attn",
    )(sink.reshape(1, N_HEADS).astype(F32), qa, ka, va, qb, kb, vb, qd, kd, vd)


def _global_kernel(q_ref, k_ref, vt_ref, kc_ref, vtc_ref, o_ref, *, tq, kt):
    kv = pl.program_id(1)
    n_blk = q_ref.shape[2] // tq
    sel = (lax.broadcasted_iota(jnp.int32, (2 * HEAD_DIM, tq), 0) // HEAD_DIM) == kv

    def block_rhs(blk):
        qt = q_ref[0, :, blk * tq:(blk + 1) * tq]
        zero = jnp.zeros_like(qt)

        def padded(g):
            qg = qt[g * HEAD_DIM:(g + 1) * HEAD_DIM, :]
            return jnp.where(sel, jnp.concatenate([qg, qg], axis=0), zero)

        return jnp.concatenate([padded(0), padded(1)], axis=1)

    rhs = [block_rhs(blk) for blk in range(n_blk)]
    n_lat = k_ref.shape[1] // kt
    n_tiles = n_lat + kc_ref.shape[1] // kt
    ones = jnp.ones((ONES_ROWS, kt), BF16)

    def k_tile(i):
        return k_ref[0, i * kt:(i + 1) * kt, :] if i < n_lat else kc_ref[0, (i - n_lat) * kt:(i - n_lat + 1) * kt, :]

    def vt_tile(i):
        vt = (vt_ref[0, :, i * kt:(i + 1) * kt] if i < n_lat
              else vtc_ref[0, :, (i - n_lat) * kt:(i - n_lat + 1) * kt])
        return jnp.concatenate([vt, ones], axis=0)

    scores, tile_max, run_max, probs = {}, {}, {}, {}
    acc = [None] * n_blk
    n_items = n_blk * n_tiles
    for i in range(n_items + 2):
        if i < n_items:
            blk, c = divmod(i, n_tiles)
            scores[i] = _dot(k_tile(c), rhs[blk])
            tile_max[i] = jnp.max(scores[i], axis=0, keepdims=True)
        j = i - 1
        if 0 <= j < n_items:
            first = j % n_tiles == 0
            run_max[j] = tile_max[j] if first else jnp.maximum(run_max[j - 1], tile_max[j])
            probs[j] = jnp.exp2(scores.pop(j) - run_max[j]).astype(BF16)
        j = i - 2
        if 0 <= j < n_items:
            blk, c = divmod(j, n_tiles)
            pv = _dot(vt_tile(c), probs.pop(j))
            acc[blk] = pv if c == 0 else acc[blk] * jnp.exp2(run_max[j - 1] - run_max[j]) + pv
            if c == n_tiles - 1:
                o = acc[blk][0:HEAD_DIM] * (1.0 / acc[blk][HEAD_DIM:HEAD_DIM + 1])
                ot = jnp.concatenate([o[:, :tq], o[:, tq:]], axis=0)
                o_ref[0, blk * tq:(blk + 1) * tq, :] = ot.T.astype(BF16)


def _global_attn(qt, k, vt, kc, vtc, *, tq, kt, n_blk):
    b, w, s = qt.shape
    t = kc.shape[1]
    rows = n_blk * tq
    assert s % rows == 0 and s % kt == 0 and t % kt == 0
    return pl.pallas_call(
        functools.partial(_global_kernel, tq=tq, kt=kt),
        grid=(b, N_KV, s // rows),
        in_specs=[pl.BlockSpec((1, 2 * HEAD_DIM, rows), lambda bi, kv, i: (bi, kv, i)),
                  pl.BlockSpec((1, s, KV_W), lambda bi, kv, i: (bi, 0, 0)),
                  pl.BlockSpec((1, HEAD_DIM, s), lambda bi, kv, i: (bi, kv, 0)),
                  pl.BlockSpec((1, t, KV_W), lambda bi, kv, i: (bi, 0, 0)),
                  pl.BlockSpec((1, HEAD_DIM, t), lambda bi, kv, i: (bi, kv, 0))],
        out_specs=pl.BlockSpec((1, rows, 2 * HEAD_DIM), lambda bi, kv, i: (bi, i, kv)),
        out_shape=jax.ShapeDtypeStruct((b, s, w), BF16),
        compiler_params=_cparams(("parallel", "arbitrary", "arbitrary")),
        name="global_attn",
    )(qt, k, vt, kc, vtc)


def _hgrn_constants():
    n = HGRN_L
    tm = np.zeros((2, 2 * n, n), np.float32)
    mk = np.zeros((2, HGRN_LEVELS + 1, n, n), np.float32)
    for d in range(2):
        pos = np.arange(n) if d == 0 else n - 1 - np.arange(n)
        p, q = pos[:, None], pos[None, :]
        for li in range(HGRN_LEVELS):
            h = n >> (li + 1)
            same_blk = (p // (2 * h)) == (q // (2 * h))
            up_p, up_q = (p % (2 * h)) >= h, (q % (2 * h)) >= h
            mk[d, li] = same_blk & up_p & ~up_q
            if h == 2:
                same_half = same_blk & (up_p == up_q)
                tm[d, 0:n] = np.where(up_p, same_half & (q <= p), same_half & (q > p))
        mk[d, HGRN_LEVELS] = np.eye(n)
        tm[d, n:2 * n] = q <= p
    mkt = np.tile(mk[:, [HGRN_LEVELS, HGRN_LEVELS - 1]].transpose(0, 1, 3, 2), (1, 1, 1, N_HEADS))
    mk = np.tile(mk[:, :HGRN_LEVELS - 1], (1, 1, N_HEADS, 1))
    return jnp.asarray(tm, BF16), jnp.asarray(mk, F32), jnp.asarray(mkt, F32)


def _hgrn_seq_kernel(qf_ref, vf_ref, zf_ref, qb_ref, vb_ref, zb_ref, lb_ref, t_ref, mk_ref, mkt_ref, bd_ref,
                     *rest, g_chunks, has_init, emit_state):
    rest = list(rest)
    init_ref = rest.pop(0) if has_init else None
    of_ref, ob_ref = rest.pop(0), rest.pop(0)
    so_ref = rest.pop(0) if emit_state else None
    st_ref = rest.pop(0)
    step = pl.program_id(1)
    n = HGRN_L

    @pl.when(step == 0)
    def _():
        st_ref[...] = init_ref[0] if has_init else jnp.zeros_like(st_ref)

    ins = ((qf_ref, vf_ref, zf_ref), (qb_ref, vb_ref, zb_ref))
    outs = (of_ref, ob_ref)
    items = [(d, (g if d == 0 else g_chunks - 1 - g) * n) for g in range(g_chunks) for d in (0, 1)]

    row = lax.broadcasted_iota(jnp.int32, (n, GROUP_W), 0)
    work = {}

    def stage_gates(i):
        d, r0 = items[i]
        q = ins[d][0][0, r0:r0 + n, :]
        v = ins[d][1][0, r0:r0 + n, :]
        z = ins[d][2][0, r0:r0 + n, :]
        lb = lb_ref[d]
        f = lb + (1.0 - lb) * jax.nn.sigmoid(z)
        g = jnp.log(f)
        g_hi = g.astype(BF16)
        g_lo = (g - g_hi.astype(F32)).astype(BF16)
        e = _dot(t_ref[d], g_hi) + _dot(t_ref[d], g_lo)
        work[i] = dict(qq=_silu(q), kk=1.0 - f, f=f, v=v, e2=e[0:n], b=e[n:2 * n],
                       a=jnp.zeros((N_HEADS * n, n), F32))

    def level_scale(w, d, h):
        if h == 2:
            return jnp.exp(w["e2"])
        b = w["b"]
        blocks = []
        for j in range(n // (2 * h)):
            m = 2 * h * j + (h - 1 if d == 0 else h)
            blocks.append(jnp.broadcast_to(b[m:m + 1, :], (2 * h, GROUP_W)))
        bm = blocks[0] if len(blocks) == 1 else jnp.concatenate(blocks, axis=0)
        later = ((row // h) % 2) == (1 if d == 0 else 0)
        return jnp.exp(jnp.where(later, b - bm, bm - b))

    def stage_level(i, li):
        d, w = items[i][0], work[i]
        h = n >> (li + 1)
        sc = level_scale(w, d, h)
        qs, ks = w["qq"] * sc, (w["kk"] * sc).astype(BF16)
        if h % 8:
            w["a"] = w["a"] + _dot_nt(_stack_heads(qs.astype(BF16)), ks) * mk_ref[d, li]
            return
        later = [j for j in range(n // h) if j % 2 == (1 if d == 0 else 0)]
        rows = lambda x, r0: jnp.concatenate([x[r0 + j * h:r0 + (j + 1) * h] for j in later], axis=0)
        res = _dot_nt(_stack_heads(rows(qs, 0).astype(BF16)), ks)
        res = res * jnp.concatenate([rows(mk_ref[d, li], hh * n) for hh in range(N_HEADS)], axis=0)
        zero = jnp.zeros((h, n), F32)
        pieces = []
        for hh in range(N_HEADS):
            for j in range(n // h):
                k = hh * (n // 2) + later.index(j) * h if j in later else None
                pieces.append(zero if k is None else res[k:k + h])
        w["a"] = w["a"] + jnp.concatenate(pieces, axis=0)

    head_ones = bd_ref[...].astype(BF16)

    def stage_pairs(i):
        d, w = items[i][0], work[i]
        qf = w["qq"] * w["f"]
        nb = pltpu.roll(qf, n - 1, 0) if d == 0 else pltpu.roll(qf, 1, 0)
        prod = jnp.concatenate([w["qq"] * w["kk"], nb * w["kk"]], axis=0).astype(BF16)
        sums = _dot(prod, head_ones)
        a_t = sums[0:n] * mkt_ref[d, 0] + sums[n:2 * n] * mkt_ref[d, 1]
        w["a"] = w["a"] + a_t.T

    st = [st_ref[0], st_ref[1]]

    def stage_out(i):
        (d, r0), w = items[i], work.pop(i)
        b_incl = w["b"]
        b_tot = b_incl[n - 1:n] if d == 0 else b_incl[0:1]
        vb = w["v"].astype(BF16)
        q_hat = (w["qq"] * jnp.exp(b_incl)).astype(BF16)
        k_hat = (w["kk"] * jnp.exp(b_tot - b_incl)).astype(BF16)
        o = _unstack_heads(_dot(w["a"].astype(BF16), vb), n)
        upd = _dot(w["v"].T.astype(BF16), k_hat) * bd_ref[...]
        outs[d][0, r0:r0 + n, :] = o + _dot_nt(q_hat, st[d].astype(BF16))
        st[d] = st[d] * jnp.exp(b_tot) + upd

    wave = min(HGRN_ITEMS_PER_WAVE, len(items))
    n_waves = len(items) // wave
    for w in range(n_waves + 2):
        if w < n_waves:
            for i in range(w * wave, (w + 1) * wave):
                stage_gates(i)
        if 0 <= w - 1 < n_waves:
            for li in range(HGRN_LEVELS - 1):
                for i in range((w - 1) * wave, w * wave):
                    stage_level(i, li)
            for i in range((w - 1) * wave, w * wave):
                stage_pairs(i)
        if 0 <= w - 2 < n_waves:
            for i in range((w - 2) * wave, (w - 1) * wave):
                stage_out(i)
    st_ref[0] = st[0]
    st_ref[1] = st[1]
    if emit_state:
        so_ref[0, 0] = st[0]
        so_ref[0, 1] = st[1]


def _hgrn_seq(c_raw, lb, consts, init_state, *, g_chunks, emit_state):
    tmat, masks, masks_t, bd = consts
    b, s, _ = c_raw.shape
    rows = g_chunks * HGRN_L
    assert s % rows == 0
    n_steps = s // rows
    has_init = init_state is not None
    fwd = lambda col: pl.BlockSpec((1, rows, GROUP_W), lambda bi, st: (bi, st, col))
    bwd = lambda col: pl.BlockSpec((1, rows, GROUP_W), lambda bi, st: (bi, n_steps - 1 - st, col))
    state_spec = pl.BlockSpec((1, 2, GROUP_W, GROUP_W), lambda bi, st: (bi, 0, 0, 0))
    in_specs = [fwd(0), fwd(1), fwd(2), bwd(0), bwd(1), bwd(3),
                _const_spec((2, 1, GROUP_W)), _const_spec(tmat.shape), _const_spec(masks.shape),
                _const_spec(masks_t.shape), _const_spec(bd.shape)]
    args = [c_raw] * 6 + [lb.reshape(2, 1, GROUP_W), tmat, masks, masks_t, bd]
    if has_init:
        in_specs.append(state_spec)
        args.append(init_state)
    out_specs = [fwd(0), bwd(0)]
    out_shape = [jax.ShapeDtypeStruct((b, s, GROUP_W), F32)] * 2
    if emit_state:
        out_specs.append(state_spec)
        out_shape.append(jax.ShapeDtypeStruct((b, 2, GROUP_W, GROUP_W), F32))
    return pl.pallas_call(
        functools.partial(_hgrn_seq_kernel, g_chunks=g_chunks, has_init=has_init, emit_state=emit_state),
        grid=(b, n_steps),
        in_specs=in_specs,
        out_specs=out_specs,
        out_shape=out_shape,
        scratch_shapes=[pltpu.VMEM((2, GROUP_W, GROUP_W), F32)],
        compiler_params=_cparams(("parallel", "arbitrary")),
        name="hgrn_ctx" if emit_state else "hgrn",
    )(*args)


def _outproj_kernel(x_ref, oa_ref, ob_ref, of_ref, obw_ref, gate_ref, od_ref, w_ref, onorm_ref, bd_ref,
                    ga_ref, lng_ref, lnb_ref, o_ref, *, alpha):
    tm = x_ref.shape[1]
    sub = min(tm, OUTPROJ_SUB_ROWS)
    acc = {}
    w = w_ref[0].astype(BF16)
    for i in range(tm // sub + 1):
        if i < tm // sub:
            r = slice(i * sub, (i + 1) * sub)
            oc = of_ref[0, r, :] + obw_ref[0, r, :]
            y = oc * lax.rsqrt(_head_mean_sq(oc, bd_ref[...]) + EPS) * onorm_ref[...]
            y = y * _silu(gate_ref[0, r, :])
            mix = jnp.concatenate([oa_ref[0, r, :], ob_ref[0, r, :], y.astype(BF16), od_ref[0, r, :]], axis=1)
            acc[i] = _dot(mix, w)
        if i >= 1:
            r = slice((i - 1) * sub, i * sub)
            z = alpha * x_ref[0, r, :] + ga_ref[0] * acc.pop(i - 1)
            o_ref[0, r, :] = _layer_norm(z, lng_ref[...], lnb_ref[...])


def _outproj(x, oa, ob, oc_f, oc_b, c_raw, od, w_out, layer, onorm, bd256, gate_a, ln_g, ln_b, *, alpha, tm):
    b, s, d = x.shape
    assert s % tm == 0
    grp = pl.BlockSpec((1, tm, GROUP_W), lambda bi, i: (bi, i, 0))
    vec = pl.BlockSpec((1, 1, d), lambda bi, i: (bi, 0, 0))
    return pl.pallas_call(
        functools.partial(_outproj_kernel, alpha=alpha),
        grid=(b, s // tm),
        in_specs=[pl.BlockSpec((1, tm, d), lambda bi, i: (bi, i, 0)), grp, grp, grp, grp,
                  pl.BlockSpec((1, tm, GROUP_W), lambda bi, i: (bi, i, 4)),
                  grp, _layer_spec(w_out.shape, layer), _const_spec(onorm.shape), _const_spec(bd256.shape),
                  vec, _const_spec(ln_g.shape), _const_spec(ln_b.shape)],
        out_specs=pl.BlockSpec((1, tm, d), lambda bi, i: (bi, i, 0)),
        out_shape=jax.ShapeDtypeStruct((b, s, d), F32),
        compiler_params=_cparams(("parallel", "arbitrary")),
        name="outproj",
    )(x, oa, ob, oc_f, oc_b, c_raw, od, w_out, onorm, bd256, gate_a, ln_g, ln_b)


def _ffn_kernel(h_ref, hp_ref, hn_ref, sc_ref, sh_ref, gt_ref, wup_ref, cw_ref, cb_ref, wdn_ref,
                lng_ref, lnb_ref, o_ref, g_scr, *, alpha, d_ff, cwid, n_tiles):
    i = pl.program_id(1)
    tm = h_ref.shape[1]
    sc = 1.0 + sc_ref[0]
    sh = sh_ref[0]
    h = h_ref[0]
    halo = jnp.concatenate([hp_ref[0], hn_ref[0]], axis=0)
    lhs = jnp.concatenate([(h * sc + sh).astype(BF16), (halo * sc + sh).astype(BF16)], axis=0)
    has_prev = (i > 0).astype(F32)
    has_next = (i < n_tiles - 1).astype(F32)
    row = lax.broadcasted_iota(jnp.int32, (tm, cwid), 0)
    first, last = row == 0, row == tm - 1

    def up_cols(c0):
        return _dot(lhs, wup_ref[0, :, c0:c0 + cwid])

    def conv_cols(u, c0):
        um = u[:tm]
        u_prev = u[tm + 7:tm + 8] * has_prev
        u_next = u[tm + 8:tm + 9] * has_next
        dn = jnp.where(first, u_prev, pltpu.roll(um, 1, 0))
        up = jnp.where(last, u_next, pltpu.roll(um, tm - 1, 0))
        w = cw_ref[:, c0:c0 + cwid]
        return dn * w[0:1] + um * w[1:2] + up * w[2:3] + cb_ref[:, c0:c0 + cwid]

    for j in range(d_ff // cwid):
        a = conv_cols(up_cols(j * cwid), j * cwid)
        v = conv_cols(up_cols(d_ff + j * cwid), d_ff + j * cwid)
        g_scr[:, j * cwid:(j + 1) * cwid] = (_silu(a) * v).astype(BF16)
    z = alpha * h + gt_ref[0] * _dot(g_scr[...], wdn_ref[0])
    o_ref[0] = _layer_norm(z, lng_ref[...], lnb_ref[...])


def _ffn(h, scale, shift, gate, w_up, conv_w, conv_b, w_down, layer, ln_g, ln_b, *, alpha, tm, cwid):
    b, s, d = h.shape
    d_ff = w_down.shape[1]
    assert s % tm == 0 and tm % 16 == 0 and d_ff % cwid == 0
    n_tiles = s // tm
    r8 = tm // 8
    vec = pl.BlockSpec((1, 1, d), lambda bi, i: (bi, 0, 0))
    return pl.pallas_call(
        functools.partial(_ffn_kernel, alpha=alpha, d_ff=d_ff, cwid=cwid, n_tiles=n_tiles),
        grid=(b, n_tiles),
        in_specs=[pl.BlockSpec((1, tm, d), lambda bi, i: (bi, i, 0)),
                  pl.BlockSpec((1, 8, d), lambda bi, i: (bi, jnp.maximum(i * r8 - 1, 0), 0)),
                  pl.BlockSpec((1, 8, d), lambda bi, i: (bi, jnp.minimum((i + 1) * r8, s // 8 - 1), 0)),
                  vec, vec, vec,
                  _layer_spec(w_up.shape, layer), _const_spec(conv_w.shape), _const_spec(conv_b.shape),
                  _layer_spec(w_down.shape, layer), _const_spec(ln_g.shape), _const_spec(ln_b.shape)],
        out_specs=pl.BlockSpec((1, tm, d), lambda bi, i: (bi, i, 0)),
        out_shape=jax.ShapeDtypeStruct((b, s, d), F32),
        scratch_shapes=[pltpu.VMEM((tm, d_ff), BF16)],
        compiler_params=_cparams(("parallel", "arbitrary")),
        name="conv_ffn",
    )(h, h, h, scale, shift, gate, w_up, conv_w, conv_b, w_down, ln_g, ln_b)


def _rope_tables(n, with_rotation):
    lane = np.arange(V7X_LANES)
    if not with_rotation:
        one = jnp.ones((n, V7X_LANES), F32)
        zero = jnp.zeros((n, V7X_LANES), F32)
        return one, zero, zero
    t = np.arange(n)
    pos = np.stack([t // GRID_W, t % GRID_W], -1).astype(np.float64)
    n_freq = HEAD_DIM // 4
    inv = ROPE_BASE ** (-np.arange(n_freq, dtype=np.float64) / n_freq)
    ang = pos[:, :, None] * inv
    axis = (lane % HEAD_DIM) // (HEAD_DIM // 2)
    freq = lane % n_freq
    ang_l = ang[:, axis, freq]
    first = (lane % (2 * n_freq)) < n_freq
    cos, sin = np.cos(ang_l), np.sin(ang_l)
    tabs = (cos, np.where(first, -sin, 0.0), np.where(first, 0.0, sin))
    return tuple(jnp.asarray(a, F32) for a in tabs)


def _block_diag_mean(width):
    lane = np.arange(width)
    same = (lane[:, None] // HEAD_DIM) == (lane[None, :] // HEAD_DIM)
    return jnp.asarray(same / HEAD_DIM, BF16)


def kernel(x, c, ctx, c_ctx, w_mod, b_mod, w_in, w_out, ln_g, ln_b, attn_sink, qk_norm, hgrn_lb_logits,
           hgrn_onorm, na_rpb, ffn_w_up, ffn_conv_w, ffn_conv_b, ffn_w_down):
    depth = w_mod.shape[0]
    b, s, d = x.shape
    t = ctx.shape[1]
    alpha = (2 * depth) ** 0.25

    mods = _modulation(c, c_ctx, w_mod, b_mod)
    rope_lat = _rope_tables(s, True)
    rope_ctx = _rope_tables(t, False)
    bd128, bd256 = _block_diag_mean(KV_W), _block_diag_mean(GROUP_W)
    lane = np.arange(GROUP_W)
    head_bd = jnp.asarray((lane[:, None] // HEAD_DIM) == (lane[None, :] // HEAD_DIM), F32)
    hgrn_consts = _hgrn_constants() + (head_bd,)
    p_lb = jax.nn.softmax(hgrn_lb_logits.astype(F32), axis=1)
    lb_all = jnp.cumsum(p_lb, axis=1) - p_lb[:, :1]
    w_in_b, w_out_b = w_in, w_out
    w_up_b, w_dn_b = ffn_w_up.astype(BF16), ffn_w_down.astype(BF16)

    h_x, h_c = x, ctx
    for l in range(depth):
        need_ctx = l < depth - 1
        mod = mods[l]
        sa, ca, ga, sf, cf, gf = (mod[:, i * d:(i + 1) * d] for i in range(6))
        lat = lambda a: a[:b, None, :]
        con = lambda a: jnp.broadcast_to(a[b][None, None, :], (b, 1, d))
        gq = jnp.tile(qk_norm[l, 0], 2)[None, :]
        gk = jnp.tile(qk_norm[l, 1], 2)[None, :]

        px = _inproj(h_x, lat(sa), lat(ca), w_in_b, l, rope_lat, gq, gk, bd128, is_ctx=False, tm=1024)
        pc = _inproj(h_c, con(sa), con(ca), w_in_b, l, rope_ctx, gq, gk, bd128, is_ctx=True, tm=t)
        c_raw, c_raw_c = px["c"], pc["c"]

        o_a = _window_attn(px["qa"], px["ka"], px["vat"], pc["ka"], pc["vat"], attn_sink[l],
                           n_pairs=WINDOW_PAIRS_PER_STEP)

        o_b = _global_attn(px["qb"], px["kb"], px["vbt"], pc["kb"], pc["vbt"], tq=256, kt=256,
                           n_blk=GLOBAL_BLOCKS_PER_STEP)

        ocf_c, ocb_c, ctx_state = _hgrn_seq(c_raw_c, lb_all[:, l], hgrn_consts, None,
                                            g_chunks=t // HGRN_L, emit_state=True)
        oc_f, oc_b = _hgrn_seq(c_raw, lb_all[:, l], hgrn_consts, ctx_state,
                               g_chunks=HGRN_CHUNKS_PER_STEP, emit_state=False)

        o_d = _na_attn(px["qd"], px["kd"], px["vd"], pc["kd"], pc["vd"], _na_bias_table(na_rpb[l]),
                       n_rows=NA_ROWS_PER_STEP)

        onorm = hgrn_onorm[l][None, :]
        lng0, lnb0 = ln_g[l, 0][None, :], ln_b[l, 0][None, :]
        lng1, lnb1 = ln_g[l, 1][None, :], ln_b[l, 1][None, :]
        cw, cb = ffn_conv_w[l], ffn_conv_b[l][None, :]
        h_x = _outproj(h_x, o_a, o_b, oc_f, oc_b, c_raw, o_d, w_out_b, l, onorm, bd256, lat(ga), lng0, lnb0,
                       alpha=alpha, tm=1024)
        h_x = _ffn(h_x, lat(cf), lat(sf), lat(gf), w_up_b, cw, cb, w_dn_b, l, lng1, lnb1,
                   alpha=alpha, tm=512, cwid=256)
        if need_ctx:
            oa_c, ob_c, od_c = _ctx_attn(attn_sink[l], pc["qa"], pc["kax"], pc["vax"], pc["qb"], pc["kbx"],
                                         pc["vbx"], pc["qd"], pc["kd"], pc["vd"])
            h_c = _outproj(h_c, oa_c, ob_c, ocf_c, ocb_c, c_raw_c, od_c, w_out_b, l, onorm, bd256, con(ga),
                           lng0, lnb0, alpha=alpha, tm=t)
            h_c = _ffn(h_c, con(cf), con(sf), con(gf), w_up_b, cw, cb, w_dn_b, l, lng1, lnb1,
                       alpha=alpha, tm=t, cwid=256)
    return h_x
```

```python
import functools

import numpy as np
import jax
import jax.numpy as jnp
from jax import lax
from jax.experimental import pallas as pl
from jax.experimental.pallas import tpu as pltpu

F32 = jnp.float32
BF16 = jnp.bfloat16

HEAD_DIM = 64
N_HEADS = 4
N_KV = 2
GROUP_W = N_HEADS * HEAD_DIM
KV_W = N_KV * HEAD_DIM
GRID_W = 64
WIN = 128
BLOCK = 128
NA_ROWS = 8
NA_COLS = 16
ROPE_BASE = 10000.0
EPS = 1e-6
HGRN_L = 64
HGRN_LEVELS = 6
HGRN_CHUNKS_PER_STEP = 8
HGRN_ITEMS_PER_WAVE = 2
WINDOW_PAIRS_PER_STEP = 4
WINDOW_ITEMS_PER_WAVE = 4
NA_ROWS_PER_STEP = 8
NA_ROWS_PER_WAVE = 2
GLOBAL_BLOCKS_PER_STEP = 4
OUTPROJ_SUB_ROWS = 256
ONES_ROWS = 16

V7X_LANES = 128
V7X_VMEM_LIMIT_BYTES = 56 * 1024 * 1024

COL_QA, COL_KA, COL_VA = 0, 256, 384
COL_QB, COL_KB, COL_VB = 512, 768, 896
COL_C = 1024
COL_QD, COL_KD, COL_VD = 2304, 2560, 2816
IN_WIDTH = 3072
C_WIDTH = 5 * GROUP_W

NEG_INF = float("-inf")
LOG2_E = 1.4426950408889634


def _cparams(sem):
    return pltpu.CompilerParams(dimension_semantics=sem, vmem_limit_bytes=V7X_VMEM_LIMIT_BYTES)


def _const_spec(shape):
    nd = len(shape)
    return pl.BlockSpec(shape, lambda *_: (0,) * nd, pipeline_mode=pl.Buffered(1))


def _layer_spec(shape, layer):
    nd = len(shape)
    return pl.BlockSpec((1,) + tuple(shape[1:]), lambda *_: (layer,) + (0,) * (nd - 1),
                        pipeline_mode=pl.Buffered(1))


def _silu(x):
    return x * jax.nn.sigmoid(x)


def _dot(a, b):
    return jnp.dot(a, b, preferred_element_type=F32)


def _dot_nt(a, b):
    return lax.dot_general(a, b, (((1,), (1,)), ((), ())), preferred_element_type=F32)


def _head_mask(shape, h):
    lane = lax.broadcasted_iota(jnp.int32, shape, len(shape) - 1)
    return (lane // HEAD_DIM) == h


def _stack_heads(q):
    zero = jnp.zeros_like(q)
    return jnp.concatenate([jnp.where(_head_mask(q.shape, h), q, zero) for h in range(N_HEADS)], axis=0)


def _unstack_heads(o, r):
    out = jnp.zeros((r, o.shape[1]), o.dtype)
    for h in range(N_HEADS):
        blk = o[h * r:(h + 1) * r]
        out = jnp.where(_head_mask(blk.shape, h), blk, out)
    return out


def _head_mean_sq(x, bd):
    sq = x * x
    hi = sq.astype(BF16)
    lo = (sq - hi.astype(F32)).astype(BF16)
    return _dot(hi, bd) + _dot(lo, bd)


def _layer_norm(z, g, b):
    mu = jnp.mean(z, axis=-1, keepdims=True)
    zc = z - mu
    var = jnp.mean(zc * zc, axis=-1, keepdims=True)
    return zc * lax.rsqrt(var + EPS) * g + b


def _mod_kernel(s_ref, w_ref, b_ref, o_ref):
    s = _silu(s_ref[...])
    s_hi = s.astype(BF16).astype(F32)
    s_mid = (s - s_hi).astype(BF16).astype(F32)
    s_lo = s - s_hi - s_mid
    w = w_ref[0]
    w_hi = w.astype(BF16)
    w_lo = (w - w_hi.astype(F32)).astype(BF16)
    rows = s.shape[0]
    lhs = jnp.concatenate([s_hi, s_mid, s_lo, jnp.zeros_like(s)], axis=0).astype(BF16)
    p_hi = _dot(lhs, w_hi)
    p_lo = _dot(lhs[0:2 * rows], w_lo)
    o_ref[0] = (p_hi[0:rows] + p_hi[rows:2 * rows] + p_hi[2 * rows:3 * rows]
                + p_lo[0:rows] + p_lo[rows:2 * rows] + b_ref[0])


def _modulation(c, c_ctx, w_mod, b_mod):
    depth, d, d6 = w_mod.shape
    b = c.shape[0]
    rows = 8
    assert b < rows
    s_in = jnp.zeros((rows, d), F32).at[:b].set(c).at[b].set(c_ctx)
    tn = d6 // 4
    return pl.pallas_call(
        _mod_kernel,
        grid=(depth, d6 // tn),
        in_specs=[pl.BlockSpec((rows, d), lambda l, j: (0, 0)),
                  pl.BlockSpec((1, d, tn), lambda l, j: (l, 0, j)),
                  pl.BlockSpec((1, 1, tn), lambda l, j: (l, 0, j))],
        out_specs=pl.BlockSpec((1, rows, tn), lambda l, j: (l, 0, j)),
        out_shape=jax.ShapeDtypeStruct((depth, rows, d6), F32),
        compiler_params=_cparams(("arbitrary", "arbitrary")),
        name="modulation",
    )(s_in, w_mod, b_mod.reshape(depth, 1, d6))


def _dup_heads(x):
    r = pltpu.roll(x, HEAD_DIM, 1)
    lo = lax.broadcasted_iota(jnp.int32, x.shape, 1) < HEAD_DIM
    return jnp.concatenate([jnp.where(lo, x, r), jnp.where(lo, r, x)], axis=1)


def _inproj_outputs(is_ctx):
    gqa = lambda g: [("q" + g, "row" if is_ctx else "col", GROUP_W, BF16), ("k" + g, "row", KV_W, BF16),
                     ("v" + g + "t", "col", KV_W, BF16)]
    dup = lambda g: [("k" + g + "x", "row", GROUP_W, BF16), ("v" + g + "x", "row", GROUP_W, BF16)]
    outs = gqa("a") + gqa("b")
    if is_ctx:
        outs += dup("a") + dup("b")
    return outs + [("c", "row", C_WIDTH, F32), ("qd", "row", GROUP_W, BF16), ("kd", "row", GROUP_W, BF16),
                   ("vd", "row", GROUP_W, BF16)]


def _inproj_kernel(h_ref, sc_ref, sh_ref, w_ref, rc_ref, ra_ref, rb_ref, gq_ref, gk_ref, bd_ref,
                   *outs, is_ctx):
    o = {name: ref for (name, _, _, _), ref in zip(_inproj_outputs(is_ctx), outs)}
    hm = (h_ref[0] * (1.0 + sc_ref[0]) + sh_ref[0]).astype(BF16)
    rc, ra, rb = rc_ref[...], ra_ref[...], rb_ref[...]
    bd = bd_ref[...]
    q_scale = HEAD_DIM ** -0.5 * (1.0 if is_ctx else LOG2_E)

    def proj(c0, width):
        return _dot(hm, w_ref[0, :, c0:c0 + width].astype(BF16))

    def rope(x):
        return x * rc + pltpu.roll(x, V7X_LANES - 16, 1) * ra + pltpu.roll(x, 16, 1) * rb

    def rms(x, gain):
        return x * lax.rsqrt(_head_mean_sq(x, bd) + EPS) * gain

    def gqa_group(g, c_q, c_k, c_v, q_fn, k_fn):
        xq = proj(c_q, GROUP_W)
        for j in range(2):
            sl = slice(j * V7X_LANES, (j + 1) * V7X_LANES)
            y = rope(q_fn(xq[:, sl])) * q_scale
            if is_ctx:
                o["q" + g][0, :, sl] = y.astype(BF16)
            else:
                o["q" + g][0, sl, :] = y.T.astype(BF16)
        k = rope(k_fn(proj(c_k, KV_W)))
        v = proj(c_v, KV_W)
        o["k" + g][0] = k.astype(BF16)
        o["v" + g + "t"][0] = v.T.astype(BF16)
        if is_ctx:
            o["k" + g + "x"][0] = _dup_heads(k).astype(BF16)
            o["v" + g + "x"][0] = _dup_heads(v).astype(BF16)

    gq, gk = gq_ref[...], gk_ref[...]
    gqa_group("a", COL_QA, COL_KA, COL_VA, lambda x: x, lambda x: x)
    gqa_group("b", COL_QB, COL_KB, COL_VB, lambda x: rms(x, gq), lambda x: rms(x, gk))

    for j in range(C_WIDTH // GROUP_W):
        o["c"][0, :, j * GROUP_W:(j + 1) * GROUP_W] = proj(COL_C + j * GROUP_W, GROUP_W)

    o["qd"][0] = (proj(COL_QD, GROUP_W) * HEAD_DIM ** -0.5).astype(BF16)
    o["kd"][0] = proj(COL_KD, GROUP_W).astype(BF16)
    o["vd"][0] = proj(COL_VD, GROUP_W).astype(BF16)


def _inproj(h, shift, scale, w_in, layer, rope_tabs, gq, gk, bd128, *, is_ctx, tm):
    b, s, d = h.shape
    assert s % tm == 0
    rc, ra, rb = rope_tabs
    outs = _inproj_outputs(is_ctx)
    out_specs = [pl.BlockSpec((1, tm, w), lambda bi, i: (bi, i, 0)) if lay == "row"
                 else pl.BlockSpec((1, w, tm), lambda bi, i: (bi, 0, i)) for _, lay, w, _ in outs]
    out_shape = [jax.ShapeDtypeStruct((b, s, w) if lay == "row" else (b, w, s), dt) for _, lay, w, dt in outs]
    tab = pl.BlockSpec((tm, V7X_LANES), lambda bi, i: (i, 0))
    vec = pl.BlockSpec((1, 1, d), lambda bi, i: (bi, 0, 0))
    res = pl.pallas_call(
        functools.partial(_inproj_kernel, is_ctx=is_ctx),
        grid=(b, s // tm),
        in_specs=[pl.BlockSpec((1, tm, d), lambda bi, i: (bi, i, 0)), vec, vec,
                  _layer_spec(w_in.shape, layer), tab, tab, tab,
                  _const_spec(gq.shape), _const_spec(gk.shape), _const_spec(bd128.shape)],
        out_specs=out_specs,
        out_shape=out_shape,
        compiler_params=_cparams(("parallel", "arbitrary")),
        name="inproj_ctx" if is_ctx else "inproj",
    )(h, scale, shift, w_in, rc, ra, rb, gq, gk, bd128)
    return {name: r for (name, _, _, _), r in zip(outs, res)}


def _attend_many(items, wave=None):
    n = len(items)
    wave = wave or n
    scores, maxes, probs, dens, outs = {}, {}, {}, {}, {}

    def stage_scores(i):
        qs, parts, sink_col = items[i]
        ss, m = [], sink_col
        for k, _, bias in parts:
            s = _dot_nt(qs, k)
            if bias is not None:
                s = s + bias
            ss.append(s)
            smax = jnp.max(s, axis=1, keepdims=True)
            m = smax if m is None else jnp.maximum(m, smax)
        scores[i], maxes[i] = ss, m

    def stage_probs(i):
        sink_col, m = items[i][2], maxes[i]
        den = None if sink_col is None else jnp.exp(sink_col - m)
        ps = []
        for s in scores.pop(i):
            p = jnp.exp(s - m)
            psum = jnp.sum(p, axis=1, keepdims=True)
            den = psum if den is None else den + psum
            ps.append(p.astype(BF16))
        probs[i], dens[i] = ps, den

    def stage_pv(i):
        out = None
        for p, (_, v, _) in zip(probs.pop(i), items[i][1]):
            pv = _dot(p, v)
            out = pv if out is None else out + pv
        outs[i] = out * (1.0 / dens[i])

    n_waves = -(-n // wave)
    for w in range(n_waves + 2):
        for stage, lag in ((stage_scores, 0), (stage_probs, 1), (stage_pv, 2)):
            ww = w - lag
            if 0 <= ww < n_waves:
                for i in range(ww * wave, min((ww + 1) * wave, n)):
                    stage(i)
    return [outs[i] for i in range(n)]


def _window_kernel(sink_ref, q_ref, *refs, n_pairs, n_blocks):
    n_kblk = 2 * n_pairs + 2
    k_refs, vt_refs = refs[:n_kblk], refs[n_kblk:2 * n_kblk]
    kc_ref, vtc_ref, o_ref = refs[2 * n_kblk:]
    step = pl.program_id(1)
    tq = 2 * BLOCK
    first_blk = step * 2 * n_pairs - 1
    ones_blk = jnp.ones((ONES_ROWS, BLOCK), BF16)
    ones_ctx = jnp.ones((ONES_ROWS, kc_ref.shape[1]), BF16)
    ki = lax.broadcasted_iota(jnp.int32, (BLOCK, tq), 0)
    qi = lax.broadcasted_iota(jnp.int32, (BLOCK, tq), 1)
    lane = lax.broadcasted_iota(jnp.int32, (1, 2 * tq), 1)
    zeros_q = jnp.zeros((HEAD_DIM, tq), BF16)

    def band_bias(jj, j):
        ok = jnp.abs((1 - jj) * BLOCK + qi - ki) <= WIN
        if j == 0:
            ok = ok & (first_blk >= 0)
        elif j == n_kblk - 1:
            ok = ok & (first_blk + j < n_blocks)
        bias = jnp.where(ok, 0.0, NEG_INF).astype(F32)
        return jnp.concatenate([bias, bias], axis=1)

    items = [(p, kv) for p in range(n_pairs) for kv in range(N_KV)]
    biases = {(p, jj): band_bias(jj, 2 * p + jj) for p in range(n_pairs) for jj in range(4)}

    scores, maxes, probs = {}, {}, {}

    def stage_scores(it):
        p, kv = it
        qt = q_ref[0, 2 * HEAD_DIM * kv:2 * HEAD_DIM * (kv + 1), p * tq:(p + 1) * tq]

        def padded(g):
            qg = qt[g * HEAD_DIM:(g + 1) * HEAD_DIM, :]
            return jnp.concatenate([qg, zeros_q] if kv == 0 else [zeros_q, qg], axis=0)

        rhs = jnp.concatenate([padded(0), padded(1)], axis=1)
        ss = [_dot(k_refs[2 * p + jj][0], rhs) + biases[(p, jj)] for jj in range(4)]
        ss.append(_dot(kc_ref[0], rhs))
        sink2 = jnp.where(lane < tq, sink_ref[0, 2 * kv], sink_ref[0, 2 * kv + 1]) * LOG2_E
        m = sink2
        for s in ss:
            m = jnp.maximum(m, jnp.max(s, axis=0, keepdims=True))
        scores[it], maxes[it] = ss, (m, sink2)

    def stage_probs(it):
        m = maxes[it][0]
        probs[it] = [jnp.exp2(s - m).astype(BF16) for s in scores.pop(it)]

    def stage_pv(it):
        p, kv = it
        ps = probs.pop(it)
        acc = None
        for jj in range(4):
            vt = vt_refs[2 * p + jj][0, kv * HEAD_DIM:(kv + 1) * HEAD_DIM, :]
            pv = _dot(jnp.concatenate([vt, ones_blk], axis=0), ps[jj])
            acc = pv if acc is None else acc + pv
        vtc = vtc_ref[0, kv * HEAD_DIM:(kv + 1) * HEAD_DIM, :]
        acc = acc + _dot(jnp.concatenate([vtc, ones_ctx], axis=0), ps[4])
        m, sink2 = maxes[(p, kv)]
        den = acc[HEAD_DIM:HEAD_DIM + 1] + jnp.exp2(sink2 - m)
        o = acc[0:HEAD_DIM] * (1.0 / den)
        ot = jnp.concatenate([o[:, :tq], o[:, tq:]], axis=0)
        o_ref[0, p * tq:(p + 1) * tq, 2 * HEAD_DIM * kv:2 * HEAD_DIM * (kv + 1)] = ot.T.astype(BF16)

    wave = min(WINDOW_ITEMS_PER_WAVE, len(items))
    n_waves = len(items) // wave
    for w in range(n_waves + 2):
        for stage, lag in ((stage_scores, 0), (stage_probs, 1), (stage_pv, 2)):
            if 0 <= w - lag < n_waves:
                for it in items[(w - lag) * wave:(w - lag + 1) * wave]:
                    stage(it)


def _window_attn(qt, k, vt, kc, vtc, sink, *, n_pairs):
    b, w, s = qt.shape
    t = kc.shape[1]
    rows = n_pairs * 2 * BLOCK
    n_blocks = s // BLOCK
    n_kblk = 2 * n_pairs + 2
    assert s % rows == 0

    def blk_idx(i, j):
        return jnp.clip(i * 2 * n_pairs - 1 + j, 0, n_blocks - 1)

    k_specs = [pl.BlockSpec((1, BLOCK, KV_W), functools.partial(lambda bi, i, j: (bi, blk_idx(i, j), 0), j=j))
               for j in range(n_kblk)]
    vt_specs = [pl.BlockSpec((1, KV_W, BLOCK), functools.partial(lambda bi, i, j: (bi, 0, blk_idx(i, j)), j=j))
                for j in range(n_kblk)]
    return pl.pallas_call(
        functools.partial(_window_kernel, n_pairs=n_pairs, n_blocks=n_blocks),
        grid=(b, s // rows),
        in_specs=[pl.BlockSpec(memory_space=pltpu.SMEM),
                  pl.BlockSpec((1, w, rows), lambda bi, i: (bi, 0, i))] + k_specs + vt_specs +
                 [pl.BlockSpec((1, t, KV_W), lambda bi, i: (bi, 0, 0)),
                  pl.BlockSpec((1, KV_W, t), lambda bi, i: (bi, 0, 0))],
        out_specs=pl.BlockSpec((1, rows, w), lambda bi, i: (bi, i, 0)),
        out_shape=jax.ShapeDtypeStruct((b, s, w), BF16),
        compiler_params=_cparams(("parallel", "arbitrary")),
        name="window_attn",
    )(sink.reshape(1, N_HEADS).astype(F32), qt, *([k] * n_kblk), *([vt] * n_kblk), kc, vtc)


def _na_kernel(q_ref, k_ref, v_ref, kc_ref, vc_ref, bias_ref, o_ref, *, rows, n_rows):
    step = pl.program_id(1)
    span = NA_ROWS * GRID_W
    ctx_part = (kc_ref[0], vc_ref[0], None)
    items = []
    for j in range(n_rows):
        r = step * n_rows + j
        r0 = jnp.clip(r - NA_ROWS // 2, 0, rows - NA_ROWS)
        start = pl.multiple_of(r0 * GRID_W, GRID_W)
        kw = k_ref[0, pl.ds(start, span), :]
        vw = v_ref[0, pl.ds(start, span), :]
        qs = _stack_heads(q_ref[0, j * GRID_W:(j + 1) * GRID_W, :])
        off0 = r0 - r + (NA_ROWS - 1)
        bias = jnp.concatenate([bias_ref[off0 + 2 * i] for i in range(NA_ROWS // 2)], axis=1)
        items.append((qs, [(kw, vw, bias), ctx_part], None))
    for j, o in enumerate(_attend_many(items, wave=NA_ROWS_PER_WAVE)):
        o_ref[0, j * GRID_W:(j + 1) * GRID_W, :] = _unstack_heads(o, GRID_W).astype(BF16)


def _na_bias_table(rpb):
    n_h, n_ro, n_co = rpb.shape
    q = np.arange(GRID_W)
    kc = np.arange(GRID_W)
    coff = np.clip(kc[None, :] - q[:, None], -(NA_COLS - 1), NA_COLS - 1) + (NA_COLS - 1)
    c0 = np.clip(q - NA_COLS // 2, 0, GRID_W - NA_COLS)
    ok = (kc[None, :] >= c0[:, None]) & (kc[None, :] < c0[:, None] + NA_COLS)
    onehot = np.zeros((n_co, GRID_W, GRID_W), np.float32)
    onehot[coff, q[:, None], kc[None, :]] = 1.0
    tab = jnp.einsum("hrc,cqk->hrqk", rpb.astype(F32), jnp.asarray(onehot), precision=lax.Precision.HIGHEST)
    tab = jnp.where(ok[None, None], tab, NEG_INF)
    pairs = jnp.concatenate([tab[:, :-1], tab[:, 1:]], axis=-1)
    return pairs.transpose(1, 0, 2, 3).reshape(n_ro - 1, n_h * GRID_W, 2 * GRID_W)


def _na_attn(q, k, v, kc, vc, bias_tab, *, n_rows):
    b, s, w = q.shape
    t = kc.shape[1]
    rows = s // GRID_W
    assert s % GRID_W == 0 and rows >= NA_ROWS and rows % n_rows == 0
    full = lambda n: pl.BlockSpec((1, n, w), lambda bi, i: (bi, 0, 0))
    blk = n_rows * GRID_W
    return pl.pallas_call(
        functools.partial(_na_kernel, rows=rows, n_rows=n_rows),
        grid=(b, rows // n_rows),
        in_specs=[pl.BlockSpec((1, blk, w), lambda bi, r: (bi, r, 0)),
                  full(s), full(s), full(t), full(t), _const_spec(bias_tab.shape)],
        out_specs=pl.BlockSpec((1, blk, w), lambda bi, r: (bi, r, 0)),
        out_shape=jax.ShapeDtypeStruct((b, s, w), BF16),
        compiler_params=_cparams(("parallel", "arbitrary")),
        name="na_attn",
    )(q, k, v, kc, vc, bias_tab)


def _ctx_attn_kernel(sink_ref, qa_ref, ka_ref, va_ref, qb_ref, kb_ref, vb_ref, qd_ref, kd_ref, vd_ref,
                     oa_ref, ob_ref, od_ref):
    t = qa_ref.shape[1]
    sink_col = jnp.concatenate([jnp.full((t, 1), sink_ref[0, h], F32) for h in range(N_HEADS)], axis=0)
    groups = ((qa_ref, ka_ref, va_ref, sink_col), (qb_ref, kb_ref, vb_ref, None), (qd_ref, kd_ref, vd_ref, None))
    outs = _attend_many([(_stack_heads(q_ref[0]), [(k_ref[0], v_ref[0], None)], sink)
                         for q_ref, k_ref, v_ref, sink in groups])
    for o_ref, o in zip((oa_ref, ob_ref, od_ref), outs):
        o_ref[0] = _unstack_heads(o, t).astype(BF16)


def _ctx_attn(sink, qa, ka, va, qb, kb, vb, qd, kd, vd):
    b, t, w = qa.shape
    blk = pl.BlockSpec((1, t, w), lambda bi: (bi, 0, 0))
    out = jax.ShapeDtypeStruct((b, t, w), BF16)
    return pl.pallas_call(
        _ctx_attn_kernel,
        grid=(b,),
        in_specs=[pl.BlockSpec(memory_space=pltpu.SMEM)] + [blk] * 9,
        out_specs=[blk] * 3,
        out_shape=[out] * 3,
        compiler_params=_cparams(("arbitrary",)),
        name="ctx_attn",
    )(sink.reshape(1, N_HEADS).astype(F32), qa, ka, va, qb, kb, vb, qd, kd, vd)


def _global_kernel(q_ref, k_ref, vt_ref, kc_ref, vtc_ref, o_ref, *, tq, kt):
    kv = pl.program_id(1)
    n_blk = q_ref.shape[2] // tq
    sel = (lax.broadcasted_iota(jnp.int32, (2 * HEAD_DIM, tq), 0) // HEAD_DIM) == kv

    def block_rhs(blk):
        qt = q_ref[0, :, blk * tq:(blk + 1) * tq]
        zero = jnp.zeros_like(qt)

        def padded(g):
            qg = qt[g * HEAD_DIM:(g + 1) * HEAD_DIM, :]
            return jnp.where(sel, jnp.concatenate([qg, qg], axis=0), zero)

        return jnp.concatenate([padded(0), padded(1)], axis=1)

    rhs = [block_rhs(blk) for blk in range(n_blk)]
    n_lat = k_ref.shape[1] // kt
    n_tiles = n_lat + kc_ref.shape[1] // kt
    ones = jnp.ones((ONES_ROWS, kt), BF16)

    def k_tile(i):
        return k_ref[0, i * kt:(i + 1) * kt, :] if i < n_lat else kc_ref[0, (i - n_lat) * kt:(i - n_lat + 1) * kt, :]

    def vt_tile(i):
        vt = (vt_ref[0, :, i * kt:(i + 1) * kt] if i < n_lat
              else vtc_ref[0, :, (i - n_lat) * kt:(i - n_lat + 1) * kt])
        return jnp.concatenate([vt, ones], axis=0)

    scores, tile_max, run_max, probs = {}, {}, {}, {}
    acc = [None] * n_blk
    n_items = n_blk * n_tiles
    for i in range(n_items + 2):
        if i < n_items:
            blk, c = divmod(i, n_tiles)
            scores[i] = _dot(k_tile(c), rhs[blk])
            tile_max[i] = jnp.max(scores[i], axis=0, keepdims=True)
        j = i - 1
        if 0 <= j < n_items:
            first = j % n_tiles == 0
            run_max[j] = tile_max[j] if first else jnp.maximum(run_max[j - 1], tile_max[j])
            probs[j] = jnp.exp2(scores.pop(j) - run_max[j]).astype(BF16)
        j = i - 2
        if 0 <= j < n_items:
            blk, c = divmod(j, n_tiles)
            pv = _dot(vt_tile(c), probs.pop(j))
            acc[blk] = pv if c == 0 else acc[blk] * jnp.exp2(run_max[j - 1] - run_max[j]) + pv
            if c == n_tiles - 1:
                o = acc[blk][0:HEAD_DIM] * (1.0 / acc[blk][HEAD_DIM:HEAD_DIM + 1])
                ot = jnp.concatenate([o[:, :tq], o[:, tq:]], axis=0)
                o_ref[0, blk * tq:(blk + 1) * tq, :] = ot.T.astype(BF16)


def _global_attn(qt, k, vt, kc, vtc, *, tq, kt, n_blk):
    b, w, s = qt.shape
    t = kc.shape[1]
    rows = n_blk * tq
    assert s % rows == 0 and s % kt == 0 and t % kt == 0
    return pl.pallas_call(
        functools.partial(_global_kernel, tq=tq, kt=kt),
        grid=(b, N_KV, s // rows),
        in_specs=[pl.BlockSpec((1, 2 * HEAD_DIM, rows), lambda bi, kv, i: (bi, kv, i)),
                  pl.BlockSpec((1, s, KV_W), lambda bi, kv, i: (bi, 0, 0)),
                  pl.BlockSpec((1, HEAD_DIM, s), lambda bi, kv, i: (bi, kv, 0)),
                  pl.BlockSpec((1, t, KV_W), lambda bi, kv, i: (bi, 0, 0)),
                  pl.BlockSpec((1, HEAD_DIM, t), lambda bi, kv, i: (bi, kv, 0))],
        out_specs=pl.BlockSpec((1, rows, 2 * HEAD_DIM), lambda bi, kv, i: (bi, i, kv)),
        out_shape=jax.ShapeDtypeStruct((b, s, w), BF16),
        compiler_params=_cparams(("parallel", "arbitrary", "arbitrary")),
        name="global_attn",
    )(qt, k, vt, kc, vtc)


def _hgrn_constants():
    n = HGRN_L
    tm = np.zeros((2, 2 * n, n), np.float32)
    mk = np.zeros((2, HGRN_LEVELS + 1, n, n), np.float32)
    for d in range(2):
        pos = np.arange(n) if d == 0 else n - 1 - np.arange(n)
        p, q = pos[:, None], pos[None, :]
        for li in range(HGRN_LEVELS):
            h = n >> (li + 1)
            same_blk = (p // (2 * h)) == (q // (2 * h))
            up_p, up_q = (p % (2 * h)) >= h, (q % (2 * h)) >= h
            mk[d, li] = same_blk & up_p & ~up_q
            if h == 2:
                same_half = same_blk & (up_p == up_q)
                tm[d, 0:n] = np.where(up_p, same_half & (q <= p), same_half & (q > p))
        mk[d, HGRN_LEVELS] = np.eye(n)
        tm[d, n:2 * n] = q <= p
    mkt = np.tile(mk[:, [HGRN_LEVELS, HGRN_LEVELS - 1]].transpose(0, 1, 3, 2), (1, 1, 1, N_HEADS))
    mk = np.tile(mk[:, :HGRN_LEVELS - 1], (1, 1, N_HEADS, 1))
    return jnp.asarray(tm, BF16), jnp.asarray(mk, F32), jnp.asarray(mkt, F32)


def _hgrn_seq_kernel(qf_ref, vf_ref, zf_ref, qb_ref, vb_ref, zb_ref, lb_ref, t_ref, mk_ref, mkt_ref, bd_ref,
                     *rest, g_chunks, has_init, emit_state):
    rest = list(rest)
    init_ref = rest.pop(0) if has_init else None
    of_ref, ob_ref = rest.pop(0), rest.pop(0)
    so_ref = rest.pop(0) if emit_state else None
    st_ref = rest.pop(0)
    step = pl.program_id(1)
    n = HGRN_L

    @pl.when(step == 0)
    def _():
        st_ref[...] = init_ref[0] if has_init else jnp.zeros_like(st_ref)

    ins = ((qf_ref, vf_ref, zf_ref), (qb_ref, vb_ref, zb_ref))
    outs = (of_ref, ob_ref)
    items = [(d, (g if d == 0 else g_chunks - 1 - g) * n) for g in range(g_chunks) for d in (0, 1)]

    row = lax.broadcasted_iota(jnp.int32, (n, GROUP_W), 0)
    work = {}

    def stage_gates(i):
        d, r0 = items[i]
        q = ins[d][0][0, r0:r0 + n, :]
        v = ins[d][1][0, r0:r0 + n, :]
        z = ins[d][2][0, r0:r0 + n, :]
        lb = lb_ref[d]
        f = lb + (1.0 - lb) * jax.nn.sigmoid(z)
        g = jnp.log(f)
        g_hi = g.astype(BF16)
        g_lo = (g - g_hi.astype(F32)).astype(BF16)
        e = _dot(t_ref[d], g_hi) + _dot(t_ref[d], g_lo)
        work[i] = dict(qq=_silu(q), kk=1.0 - f, f=f, v=v, e2=e[0:n], b=e[n:2 * n],
                       a=jnp.zeros((N_HEADS * n, n), F32))

    def level_scale(w, d, h):
        if h == 2:
            return jnp.exp(w["e2"])
        b = w["b"]
        blocks = []
        for j in range(n // (2 * h)):
            m = 2 * h * j + (h - 1 if d == 0 else h)
            blocks.append(jnp.broadcast_to(b[m:m + 1, :], (2 * h, GROUP_W)))
        bm = blocks[0] if len(blocks) == 1 else jnp.concatenate(blocks, axis=0)
        later = ((row // h) % 2) == (1 if d == 0 else 0)
        return jnp.exp(jnp.where(later, b - bm, bm - b))

    def stage_level(i, li):
        d, w = items[i][0], work[i]
        h = n >> (li + 1)
        sc = level_scale(w, d, h)
        qs, ks = w["qq"] * sc, (w["kk"] * sc).astype(BF16)
        if h % 8:
            w["a"] = w["a"] + _dot_nt(_stack_heads(qs.astype(BF16)), ks) * mk_ref[d, li]
            return
        later = [j for j in range(n // h) if j % 2 == (1 if d == 0 else 0)]
        rows = lambda x, r0: jnp.concatenate([x[r0 + j * h:r0 + (j + 1) * h] for j in later], axis=0)
        res = _dot_nt(_stack_heads(rows(qs, 0).astype(BF16)), ks)
        res = res * jnp.concatenate([rows(mk_ref[d, li], hh * n) for hh in range(N_HEADS)], axis=0)
        zero = jnp.zeros((h, n), F32)
        pieces = []
        for hh in range(N_HEADS):
            for j in range(n // h):
                k = hh * (n // 2) + later.index(j) * h if j in later else None
                pieces.append(zero if k is None else res[k:k + h])
        w["a"] = w["a"] + jnp.concatenate(pieces, axis=0)

    head_ones = bd_ref[...].astype(BF16)

    def stage_pairs(i):
        d, w = items[i][0], work[i]
        qf = w["qq"] * w["f"]
        nb = pltpu.roll(qf, n - 1, 0) if d == 0 else pltpu.roll(qf, 1, 0)
        prod = jnp.concatenate([w["qq"] * w["kk"], nb * w["kk"]], axis=0).astype(BF16)
        sums = _dot(prod, head_ones)
        a_t = sums[0:n] * mkt_ref[d, 0] + sums[n:2 * n] * mkt_ref[d, 1]
        w["a"] = w["a"] + a_t.T

    st = [st_ref[0], st_ref[1]]

    def stage_out(i):
        (d, r0), w = items[i], work.pop(i)
        b_incl = w["b"]
        b_tot = b_incl[n - 1:n] if d == 0 else b_incl[0:1]
        vb = w["v"].astype(BF16)
        q_hat = (w["qq"] * jnp.exp(b_incl)).astype(BF16)
        k_hat = (w["kk"] * jnp.exp(b_tot - b_incl)).astype(BF16)
        o = _unstack_heads(_dot(w["a"].astype(BF16), vb), n)
        upd = _dot(w["v"].T.astype(BF16), k_hat) * bd_ref[...]
        outs[d][0, r0:r0 + n, :] = o + _dot_nt(q_hat, st[d].astype(BF16))
        st[d] = st[d] * jnp.exp(b_tot) + upd

    wave = min(HGRN_ITEMS_PER_WAVE, len(items))
    n_waves = len(items) // wave
    for w in range(n_waves + 2):
        if w < n_waves:
            for i in range(w * wave, (w + 1) * wave):
                stage_gates(i)
        if 0 <= w - 1 < n_waves:
            for li in range(HGRN_LEVELS - 1):
                for i in range((w - 1) * wave, w * wave):
                    stage_level(i, li)
            for i in range((w - 1) * wave, w * wave):
                stage_pairs(i)
        if 0 <= w - 2 < n_waves:
            for i in range((w - 2) * wave, (w - 1) * wave):
                stage_out(i)
    st_ref[0] = st[0]
    st_ref[1] = st[1]
    if emit_state:
        so_ref[0, 0] = st[0]
        so_ref[0, 1] = st[1]


def _hgrn_seq(c_raw, lb, consts, init_state, *, g_chunks, emit_state):
    tmat, masks, masks_t, bd = consts
    b, s, _ = c_raw.shape
    rows = g_chunks * HGRN_L
    assert s % rows == 0
    n_steps = s // rows
    has_init = init_state is not None
    fwd = lambda col: pl.BlockSpec((1, rows, GROUP_W), lambda bi, st: (bi, st, col))
    bwd = lambda col: pl.BlockSpec((1, rows, GROUP_W), lambda bi, st: (bi, n_steps - 1 - st, col))
    state_spec = pl.BlockSpec((1, 2, GROUP_W, GROUP_W), lambda bi, st: (bi, 0, 0, 0))
    in_specs = [fwd(0), fwd(1), fwd(2), bwd(0), bwd(1), bwd(3),
                _const_spec((2, 1, GROUP_W)), _const_spec(tmat.shape), _const_spec(masks.shape),
                _const_spec(masks_t.shape), _const_spec(bd.shape)]
    args = [c_raw] * 6 + [lb.reshape(2, 1, GROUP_W), tmat, masks, masks_t, bd]
    if has_init:
        in_specs.append(state_spec)
        args.append(init_state)
    out_specs = [fwd(0), bwd(0)]
    out_shape = [jax.ShapeDtypeStruct((b, s, GROUP_W), F32)] * 2
    if emit_state:
        out_specs.append(state_spec)
        out_shape.append(jax.ShapeDtypeStruct((b, 2, GROUP_W, GROUP_W), F32))
    return pl.pallas_call(
        functools.partial(_hgrn_seq_kernel, g_chunks=g_chunks, has_init=has_init, emit_state=emit_state),
        grid=(b, n_steps),
        in_specs=in_specs,
        out_specs=out_specs,
        out_shape=out_shape,
        scratch_shapes=[pltpu.VMEM((2, GROUP_W, GROUP_W), F32)],
        compiler_params=_cparams(("parallel", "arbitrary")),
        name="hgrn_ctx" if emit_state else "hgrn",
    )(*args)


def _outproj_kernel(x_ref, oa_ref, ob_ref, of_ref, obw_ref, gate_ref, od_ref, w_ref, onorm_ref, bd_ref,
                    ga_ref, lng_ref, lnb_ref, o_ref, *, alpha):
    tm = x_ref.shape[1]
    sub = min(tm, OUTPROJ_SUB_ROWS)
    acc = {}
    w = w_ref[0].astype(BF16)
    for i in range(tm // sub + 1):
        if i < tm // sub:
            r = slice(i * sub, (i + 1) * sub)
            oc = of_ref[0, r, :] + obw_ref[0, r, :]
            y = oc * lax.rsqrt(_head_mean_sq(oc, bd_ref[...]) + EPS) * onorm_ref[...]
            y = y * _silu(gate_ref[0, r, :])
            mix = jnp.concatenate([oa_ref[0, r, :], ob_ref[0, r, :], y.astype(BF16), od_ref[0, r, :]], axis=1)
            acc[i] = _dot(mix, w)
        if i >= 1:
            r = slice((i - 1) * sub, i * sub)
            z = alpha * x_ref[0, r, :] + ga_ref[0] * acc.pop(i - 1)
            o_ref[0, r, :] = _layer_norm(z, lng_ref[...], lnb_ref[...])


def _outproj(x, oa, ob, oc_f, oc_b, c_raw, od, w_out, layer, onorm, bd256, gate_a, ln_g, ln_b, *, alpha, tm):
    b, s, d = x.shape
    assert s % tm == 0
    grp = pl.BlockSpec((1, tm, GROUP_W), lambda bi, i: (bi, i, 0))
    vec = pl.BlockSpec((1, 1, d), lambda bi, i: (bi, 0, 0))
    return pl.pallas_call(
        functools.partial(_outproj_kernel, alpha=alpha),
        grid=(b, s // tm),
        in_specs=[pl.BlockSpec((1, tm, d), lambda bi, i: (bi, i, 0)), grp, grp, grp, grp,
                  pl.BlockSpec((1, tm, GROUP_W), lambda bi, i: (bi, i, 4)),
                  grp, _layer_spec(w_out.shape, layer), _const_spec(onorm.shape), _const_spec(bd256.shape),
                  vec, _const_spec(ln_g.shape), _const_spec(ln_b.shape)],
        out_specs=pl.BlockSpec((1, tm, d), lambda bi, i: (bi, i, 0)),
        out_shape=jax.ShapeDtypeStruct((b, s, d), F32),
        compiler_params=_cparams(("parallel", "arbitrary")),
        name="outproj",
    )(x, oa, ob, oc_f, oc_b, c_raw, od, w_out, onorm, bd256, gate_a, ln_g, ln_b)


def _ffn_kernel(h_ref, hp_ref, hn_ref, sc_ref, sh_ref, gt_ref, wup_ref, cw_ref, cb_ref, wdn_ref,
                lng_ref, lnb_ref, o_ref, g_scr, *, alpha, d_ff, cwid, n_tiles):
    i = pl.program_id(1)
    tm = h_ref.shape[1]
    sc = 1.0 + sc_ref[0]
    sh = sh_ref[0]
    h = h_ref[0]
    halo = jnp.concatenate([hp_ref[0], hn_ref[0]], axis=0)
    lhs = jnp.concatenate([(h * sc + sh).astype(BF16), (halo * sc + sh).astype(BF16)], axis=0)
    has_prev = (i > 0).astype(F32)
    has_next = (i < n_tiles - 1).astype(F32)
    row = lax.broadcasted_iota(jnp.int32, (tm, cwid), 0)
    first, last = row == 0, row == tm - 1

    def up_cols(c0):
        return _dot(lhs, wup_ref[0, :, c0:c0 + cwid])

    def conv_cols(u, c0):
        um = u[:tm]
        u_prev = u[tm + 7:tm + 8] * has_prev
        u_next = u[tm + 8:tm + 9] * has_next
        dn = jnp.where(first, u_prev, pltpu.roll(um, 1, 0))
        up = jnp.where(last, u_next, pltpu.roll(um, tm - 1, 0))
        w = cw_ref[:, c0:c0 + cwid]
        return dn * w[0:1] + um * w[1:2] + up * w[2:3] + cb_ref[:, c0:c0 + cwid]

    for j in range(d_ff // cwid):
        a = conv_cols(up_cols(j * cwid), j * cwid)
        v = conv_cols(up_cols(d_ff + j * cwid), d_ff + j * cwid)
        g_scr[:, j * cwid:(j + 1) * cwid] = (_silu(a) * v).astype(BF16)
    z = alpha * h + gt_ref[0] * _dot(g_scr[...], wdn_ref[0])
    o_ref[0] = _layer_norm(z, lng_ref[...], lnb_ref[...])


def _ffn(h, scale, shift, gate, w_up, conv_w, conv_b, w_down, layer, ln_g, ln_b, *, alpha, tm, cwid):
    b, s, d = h.shape
    d_ff = w_down.shape[1]
    assert s % tm == 0 and tm % 16 == 0 and d_ff % cwid == 0
    n_tiles = s // tm
    r8 = tm // 8
    vec = pl.BlockSpec((1, 1, d), lambda bi, i: (bi, 0, 0))
    return pl.pallas_call(
        functools.partial(_ffn_kernel, alpha=alpha, d_ff=d_ff, cwid=cwid, n_tiles=n_tiles),
        grid=(b, n_tiles),
        in_specs=[pl.BlockSpec((1, tm, d), lambda bi, i: (bi, i, 0)),
                  pl.BlockSpec((1, 8, d), lambda bi, i: (bi, jnp.maximum(i * r8 - 1, 0), 0)),
                  pl.BlockSpec((1, 8, d), lambda bi, i: (bi, jnp.minimum((i + 1) * r8, s // 8 - 1), 0)),
                  vec, vec, vec,
                  _layer_spec(w_up.shape, layer), _const_spec(conv_w.shape), _const_spec(conv_b.shape),
                  _layer_spec(w_down.shape, layer), _const_spec(ln_g.shape), _const_spec(ln_b.shape)],
        out_specs=pl.BlockSpec((1, tm, d), lambda bi, i: (bi, i, 0)),
        out_shape=jax.ShapeDtypeStruct((b, s, d), F32),
        scratch_shapes=[pltpu.VMEM((tm, d_ff), BF16)],
        compiler_params=pltpu.CompilerParams(
            dimension_semantics=("parallel", "arbitrary"), vmem_limit_bytes=V7X_VMEM_LIMIT_BYTES,
            allow_input_fusion=[i in (6, 9) for i in range(12)]),
        name="conv_ffn",
    )(h, h, h, scale, shift, gate, w_up, conv_w, conv_b, w_down, ln_g, ln_b)


def _rope_tables(n, with_rotation):
    lane = np.arange(V7X_LANES)
    if not with_rotation:
        one = jnp.ones((n, V7X_LANES), F32)
        zero = jnp.zeros((n, V7X_LANES), F32)
        return one, zero, zero
    t = np.arange(n)
    pos = np.stack([t // GRID_W, t % GRID_W], -1).astype(np.float64)
    n_freq = HEAD_DIM // 4
    inv = ROPE_BASE ** (-np.arange(n_freq, dtype=np.float64) / n_freq)
    ang = pos[:, :, None] * inv
    axis = (lane % HEAD_DIM) // (HEAD_DIM // 2)
    freq = lane % n_freq
    ang_l = ang[:, axis, freq]
    first = (lane % (2 * n_freq)) < n_freq
    cos, sin = np.cos(ang_l), np.sin(ang_l)
    tabs = (cos, np.where(first, -sin, 0.0), np.where(first, 0.0, sin))
    return tuple(jnp.asarray(a, F32) for a in tabs)


def _block_diag_mean(width):
    lane = np.arange(width)
    same = (lane[:, None] // HEAD_DIM) == (lane[None, :] // HEAD_DIM)
    return jnp.asarray(same / HEAD_DIM, BF16)


def kernel(x, c, ctx, c_ctx, w_mod, b_mod, w_in, w_out, ln_g, ln_b, attn_sink, qk_norm, hgrn_lb_logits,
           hgrn_onorm, na_rpb, ffn_w_up, ffn_conv_w, ffn_conv_b, ffn_w_down):
    depth = w_mod.shape[0]
    b, s, d = x.shape
    t = ctx.shape[1]
    alpha = (2 * depth) ** 0.25

    mods = _modulation(c, c_ctx, w_mod, b_mod)
    rope_lat = _rope_tables(s, True)
    rope_ctx = _rope_tables(t, False)
    bd128, bd256 = _block_diag_mean(KV_W), _block_diag_mean(GROUP_W)
    lane = np.arange(GROUP_W)
    head_bd = jnp.asarray((lane[:, None] // HEAD_DIM) == (lane[None, :] // HEAD_DIM), F32)
    hgrn_consts = _hgrn_constants() + (head_bd,)
    p_lb = jax.nn.softmax(hgrn_lb_logits.astype(F32), axis=1)
    lb_all = jnp.cumsum(p_lb, axis=1) - p_lb[:, :1]
    w_in_b, w_out_b = w_in, w_out
    w_up_b, w_dn_b = ffn_w_up.astype(BF16), ffn_w_down.astype(BF16)

    h_x, h_c = x, ctx
    for l in range(depth):
        need_ctx = l < depth - 1
        mod = mods[l]
        sa, ca, ga, sf, cf, gf = (mod[:, i * d:(i + 1) * d] for i in range(6))
        lat = lambda a: a[:b, None, :]
        con = lambda a: jnp.broadcast_to(a[b][None, None, :], (b, 1, d))
        gq = jnp.tile(qk_norm[l, 0], 2)[None, :]
        gk = jnp.tile(qk_norm[l, 1], 2)[None, :]

        px = _inproj(h_x, lat(sa), lat(ca), w_in_b, l, rope_lat, gq, gk, bd128, is_ctx=False, tm=1024)
        pc = _inproj(h_c, con(sa), con(ca), w_in_b, l, rope_ctx, gq, gk, bd128, is_ctx=True, tm=t)
        c_raw, c_raw_c = px["c"], pc["c"]

        o_a = _window_attn(px["qa"], px["ka"], px["vat"], pc["ka"], pc["vat"], attn_sink[l],
                           n_pairs=WINDOW_PAIRS_PER_STEP)

        o_b = _global_attn(px["qb"], px["kb"], px["vbt"], pc["kb"], pc["vbt"], tq=256, kt=256,
                           n_blk=GLOBAL_BLOCKS_PER_STEP)

        ocf_c, ocb_c, ctx_state = _hgrn_seq(c_raw_c, lb_all[:, l], hgrn_consts, None,
                                            g_chunks=t // HGRN_L, emit_state=True)
        oc_f, oc_b = _hgrn_seq(c_raw, lb_all[:, l], hgrn_consts, ctx_state,
                               g_chunks=HGRN_CHUNKS_PER_STEP, emit_state=False)

        o_d = _na_attn(px["qd"], px["kd"], px["vd"], pc["kd"], pc["vd"], _na_bias_table(na_rpb[l]),
                       n_rows=NA_ROWS_PER_STEP)

        onorm = hgrn_onorm[l][None, :]
        lng0, lnb0 = ln_g[l, 0][None, :], ln_b[l, 0][None, :]
        lng1, lnb1 = ln_g[l, 1][None, :], ln_b[l, 1][None, :]
        cw, cb = ffn_conv_w[l], ffn_conv_b[l][None, :]
        h_x = _outproj(h_x, o_a, o_b, oc_f, oc_b, c_raw, o_d, w_out_b, l, onorm, bd256, lat(ga), lng0, lnb0,
                       alpha=alpha, tm=1024)
        h_x = _ffn(h_x, lat(cf), lat(sf), lat(gf), w_up_b, cw, cb, w_dn_b, l, lng1, lnb1,
                   alpha=alpha, tm=512, cwid=256)
        if need_ctx:
            oa_c, ob_c, od_c = _ctx_attn(attn_sink[l], pc["qa"], pc["kax"], pc["vax"], pc["qb"], pc["kbx"],
                                         pc["vbx"], pc["qd"], pc["kd"], pc["vd"])
            h_c = _outproj(h_c, oa_c, ob_c, ocf_c, ocb_c, c_raw_c, od_c, w_out_b, l, onorm, bd256, con(ga),
                           lng0, lnb0, alpha=alpha, tm=t)
            h_c = _ffn(h_c, con(cf), con(sf), con(gf), w_up_b, cw, cb, w_dn_b, l, lng1, lnb1,
                       alpha=alpha, tm=t, cwid=256)
    return h_x
```
